```python
import jax, jax.numpy as jnp
from jax import lax
import numpy as np

D_MODEL = 1024
BATCH = 2
SEQ = 16384
DEPTH = 1
DEC_BATCH = 8
DEC_SEQ = 2048
PAST_LEN = 128

MIX_WIDTH = D_MODEL
FNET_WIDTH = D_MODEL // 2
FNET_GROUPS = 4
FNET_GROUP_DIM = FNET_WIDTH // FNET_GROUPS
RET_WIDTH = MIX_WIDTH - FNET_WIDTH
RET_HEADS = 4
RET_HEAD_DIM = RET_WIDTH // RET_HEADS
D_FF = 4 * D_MODEL
CHUNK = 128
ROPE_BASE = 10000.0
EPS = 1e-6
N_MOD = 6
IN_WIDTH = FNET_WIDTH + 4 * RET_WIDTH
SPLITS = (FNET_WIDTH, FNET_WIDTH + RET_WIDTH, FNET_WIDTH + 2 * RET_WIDTH, FNET_WIDTH + 3 * RET_WIDTH)
DECAY_OFFSET_FWD = 0.0
DECAY_OFFSET_BWD = 0.5

kernel_name = "hymba_fnet_retnet_adaln_encoder"


def rmsnorm(x, g):
    xf = x.astype(jnp.float32)
    y = xf * lax.rsqrt(jnp.mean(xf * xf, axis=-1, keepdims=True) + EPS)
    return (y * g.astype(jnp.float32)).astype(x.dtype)


def modulate(h, shift, scale):
    return h * (1.0 + scale[:, None, :]) + shift[:, None, :]


def rotary(x):
    S = x.shape[1]
    half = x.shape[-1] // 2
    inv = ROPE_BASE ** (-jnp.arange(half, dtype=jnp.float32) / half)
    ang = jnp.arange(S, dtype=jnp.float32)[:, None] * inv[None, :]
    cos = jnp.cos(ang)[None, :, None, :]
    sin = jnp.sin(ang)[None, :, None, :]
    xf = x.astype(jnp.float32)
    x1, x2 = xf[..., :half], xf[..., half:]
    return jnp.concatenate([x1 * cos - x2 * sin, x1 * sin + x2 * cos], axis=-1)


def log_gammas(offset):
    return jnp.log1p(-jnp.exp2(-5.0 - offset - jnp.arange(RET_HEADS, dtype=jnp.float32)))


def retention_chunkwise(q, k, v, log_g, strict):
    B, S, H, Dh = q.shape
    N = S // CHUNK
    qc = q.reshape(B, N, CHUNK, H, Dh)
    kc = k.reshape(B, N, CHUNK, H, Dh)
    vc = v.reshape(B, N, CHUNK, H, Dh)
    pos = jnp.arange(CHUNK, dtype=jnp.float32)
    diff = pos[:, None] - pos[None, :]
    mask = (diff > 0) if strict else (diff >= 0)
    decay = jnp.where(mask[None], jnp.exp(log_g[:, None, None] * jnp.maximum(diff, 0.0)[None]), 0.0)
    scores = jnp.einsum('bnihd,bnjhd->bnhij', qc, kc) * decay[None, None]
    inner = jnp.einsum('bnhij,bnjhe->bnihe', scores, vc)
    zeta = jnp.exp(log_g[:, None] * (CHUNK - 1.0 - pos)[None])
    kv = jnp.einsum('bnjhd,hj,bnjhe->bnhde', kc, zeta, vc)
    chunk_decay = jnp.exp(log_g * CHUNK)[None, :, None, None]

    def step(state, kv_n):
        return state * chunk_decay + kv_n, state

    _, prev = lax.scan(step, jnp.zeros((B, H, Dh, Dh), q.dtype), jnp.moveaxis(kv, 1, 0))
    prev = jnp.moveaxis(prev, 0, 1)
    xi = jnp.exp(log_g[:, None] * (pos + 1.0)[None])
    cross = jnp.einsum('bnihd,hi,bnhde->bnihe', qc, xi, prev)
    return (inner + cross).reshape(B, S, H, Dh)


def hybrid_mixer(h, w_in, w_fnet, w_out):
    B, S, _ = h.shape
    proj = h @ w_in
    u, q, k, v, g = jnp.split(proj, SPLITS, axis=-1)
    ug = u.reshape(B, S, FNET_GROUPS, FNET_GROUP_DIM).astype(jnp.float32)
    f = jnp.fft.fft2(ug, axes=(1, 3), norm='ortho').real
    f = jnp.einsum('bsgc,gcd->bsgd', f.astype(h.dtype), w_fnet).reshape(B, S, FNET_WIDTH)
    q = rotary(q.reshape(B, S, RET_HEADS, RET_HEAD_DIM))
    k = rotary(k.reshape(B, S, RET_HEADS, RET_HEAD_DIM)) * (RET_HEAD_DIM ** -0.5)
    v = v.reshape(B, S, RET_HEADS, RET_HEAD_DIM).astype(jnp.float32)
    fwd = retention_chunkwise(q, k, v, log_gammas(DECAY_OFFSET_FWD), False)
    bwd = jnp.flip(retention_chunkwise(jnp.flip(q, 1), jnp.flip(k, 1), jnp.flip(v, 1),
                                       log_gammas(DECAY_OFFSET_BWD), True), axis=1)
    r = fwd + bwd
    r = r * lax.rsqrt(jnp.mean(r * r, axis=-1, keepdims=True) + EPS)
    r = r.reshape(B, S, RET_WIDTH).astype(h.dtype) * jax.nn.silu(g)
    return jnp.concatenate([f, r], axis=-1) @ w_out


def encoder_layer(x, c, ada_w, ada_b, norm_mix, w_in, w_fnet, w_out, norm_mlp, w_mlp_in, w_mlp_out):
    mod = jax.nn.silu(c) @ ada_w + ada_b
    sh1, sc1, gt1, sh2, sc2, gt2 = jnp.split(mod, N_MOD, axis=-1)
    h = modulate(rmsnorm(x, norm_mix), sh1, sc1)
    x = x + gt1[:, None, :] * hybrid_mixer(h, w_in, w_fnet, w_out)
    h = modulate(rmsnorm(x, norm_mlp), sh2, sc2)
    x = x + gt2[:, None, :] * (jnp.square(jax.nn.relu(h @ w_mlp_in)) @ w_mlp_out)
    return x


def run_trunk(x, c, ada_w, ada_b, norm_mix, w_in, w_fnet, w_out, norm_mlp, w_mlp_in, w_mlp_out, norm_final):
    for l in range(DEPTH):
        x = encoder_layer(x, c, ada_w[l], ada_b[l], norm_mix[l], w_in[l], w_fnet[l], w_out[l],
                          norm_mlp[l], w_mlp_in[l], w_mlp_out[l])
    return rmsnorm(x, norm_final)


def setup_inputs(seed: int = 0) -> dict:
    key = jax.random.key(seed)
    ks = jax.random.split(key, 16)
    f32 = jnp.float32
    nrm = lambda k, shape, s: jax.random.normal(k, shape, f32) * s
    return {
        "x_prompt": nrm(ks[0], (BATCH, SEQ, D_MODEL), 1.0),
        "x_sample": nrm(ks[1], (DEC_BATCH, DEC_SEQ, D_MODEL), 1.0),
        "c_prompt": nrm(ks[2], (BATCH, D_MODEL), 1.0),
        "c_sample": nrm(ks[3], (DEC_BATCH, D_MODEL), 1.0),
        "ada_w": nrm(ks[4], (DEPTH, D_MODEL, N_MOD * D_MODEL), 0.5 * D_MODEL ** -0.5),
        "ada_b": nrm(ks[5], (DEPTH, N_MOD * D_MODEL), 0.02),
        "norm_mix": 1.0 + nrm(ks[6], (DEPTH, D_MODEL), 0.02),
        "w_in": nrm(ks[7], (DEPTH, D_MODEL, IN_WIDTH), D_MODEL ** -0.5),
        "w_fnet": nrm(ks[8], (DEPTH, FNET_GROUPS, FNET_GROUP_DIM, FNET_GROUP_DIM), FNET_GROUP_DIM ** -0.5),
        "w_out": nrm(ks[9], (DEPTH, MIX_WIDTH, D_MODEL), MIX_WIDTH ** -0.5),
        "norm_mlp": 1.0 + nrm(ks[10], (DEPTH, D_MODEL), 0.02),
        "w_mlp_in": nrm(ks[11], (DEPTH, D_MODEL, D_FF), D_MODEL ** -0.5),
        "w_mlp_out": nrm(ks[12], (DEPTH, D_FF, D_MODEL), D_FF ** -0.5),
        "norm_final": 1.0 + nrm(ks[13], (D_MODEL,), 0.02),
    }


def reference(x_prompt, x_sample, c_prompt, c_sample, ada_w, ada_b, norm_mix, w_in, w_fnet, w_out,
              norm_mlp, w_mlp_in, w_mlp_out, norm_final):
    y_prompt = run_trunk(x_prompt, c_prompt, ada_w, ada_b, norm_mix, w_in, w_fnet, w_out,
                         norm_mlp, w_mlp_in, w_mlp_out, norm_final)
    y_sample = run_trunk(x_sample, c_sample, ada_w, ada_b, norm_mix, w_in, w_fnet, w_out,
                         norm_mlp, w_mlp_in, w_mlp_out, norm_final)
    return (y_prompt, y_sample)
```

```python
import functools
import math

import numpy as np
import jax
import jax.numpy as jnp
from jax import lax
from jax.experimental import pallas as pl
from jax.experimental.pallas import tpu as pltpu

F32 = jnp.float32
BF16 = jnp.bfloat16

D_MODEL = 1024
FNET_WIDTH = 512
FNET_GROUPS = 4
GROUP_DIM = 128
RET_WIDTH = 512
RET_HEADS = 4
HEAD_DIM = 128
D_FF = 4096
CHUNK = 128
ROPE_BASE = 10000.0
EPS = 1e-6
N_MOD = 6
IN_WIDTH = FNET_WIDTH + 4 * RET_WIDTH
DECAY_OFFSET_FWD = 0.0
DECAY_OFFSET_BWD = 0.5

DFT_N1 = 128
MXU_DIM = 256
VMEM_LIMIT = 56 * 1024 * 1024

TM_INPROJ = 512
TM_MIXER = 512
TM_MLP = 512


def _params(semantics):
    return pltpu.CompilerParams(dimension_semantics=semantics, vmem_limit_bytes=VMEM_LIMIT)


def _retention_tables():
    scale = HEAD_DIM ** -0.5
    h = np.arange(RET_HEADS, dtype=np.float64)
    lg_f = np.log1p(-np.exp2(-5.0 - DECAY_OFFSET_FWD - h))[:, None, None]
    lg_b = np.log1p(-np.exp2(-5.0 - DECAY_OFFSET_BWD - h))[:, None, None]
    i = np.arange(CHUNK, dtype=np.float64)[None, :, None]
    j = np.arange(CHUNK, dtype=np.float64)[None, None, :]
    ones = np.ones((1, 1, HEAD_DIM))
    intra = scale * np.where(i >= j, np.exp(lg_f * np.maximum(i - j, 0.0)), np.exp(lg_b * np.maximum(j - i, 0.0)))
    xi_f = np.exp(lg_f * (i + 1.0)) * ones
    xi_b = np.exp(lg_b * (CHUNK - i)) * ones
    zeta_f = scale * np.exp(lg_f * (CHUNK - 1.0 - i)) * ones
    zeta_b = scale * np.exp(lg_b * i) * ones
    dec_f = tuple(float(v) for v in np.exp(lg_f[:, 0, 0] * CHUNK))
    dec_b = tuple(float(v) for v in np.exp(lg_b[:, 0, 0] * CHUNK))
    f = lambda a: jnp.asarray(a.astype(np.float32))
    return f(intra), f(xi_f), f(xi_b), f(zeta_f), f(zeta_b), dec_f, dec_b


def _rotary_tables(seq):
    half = HEAD_DIM // 2
    inv = ROPE_BASE ** (-jnp.arange(half, dtype=F32) / half)
    ang = jnp.arange(seq, dtype=F32)[:, None] * inv[None, :]
    cos, sin = jnp.cos(ang), jnp.sin(ang)
    return jnp.concatenate([cos, cos], axis=1), jnp.concatenate([-sin, sin], axis=1)


def _angle(num, den):
    return (2.0 * math.pi / den) * (num % den).astype(F32)


def _dft_stage_a(n1):
    k = jnp.arange(n1, dtype=jnp.int32)
    ang = _angle(k[:, None] * k[None, :], n1)
    return (jnp.concatenate([jnp.cos(ang), -jnp.sin(ang)], axis=0) * (n1 ** -0.5)).astype(BF16)


def _dft_stage_b(n1, n2):
    reps = MXU_DIM // (2 * n2)
    seq = n1 * n2
    idx = jnp.arange(MXU_DIM, dtype=jnp.int32)
    c_i, r_i, p_i = idx // (reps * n2), (idx // n2) % reps, idx % n2
    blk = jnp.arange(n1 // reps, dtype=jnp.int32)[:, None, None]
    k1 = blk * reps + r_i[None, :, None]
    ang = _angle(p_i[None, None, :] * (n1 * p_i[None, :, None] + k1), seq)
    cos, sin = jnp.cos(ang), jnp.sin(ang)
    c_out, c_in = c_i[None, :, None], c_i[None, None, :]
    val = jnp.where(c_out == c_in, cos, jnp.where(c_out < c_in, sin, -sin))
    val = jnp.where(r_i[None, :, None] == r_i[None, None, :], val, 0.0)
    return (val * ((n2 * GROUP_DIM) ** -0.5)).astype(BF16)


def _channel_dft():
    c = np.arange(GROUP_DIM, dtype=np.int64)
    ang = 2.0 * np.pi * ((c[:, None] * c[None, :]) % GROUP_DIM) / GROUP_DIM
    return jnp.asarray(np.concatenate([np.cos(ang), np.sin(ang)], axis=0).astype(np.float32))


def _mod_kernel(c_ref, w_ref, b_ref, o_ref):
    c = c_ref[...]
    s = c * jax.nn.sigmoid(c)
    o_ref[...] = jnp.dot(s, w_ref[...], preferred_element_type=F32,
                         precision=lax.Precision.HIGHEST) + b_ref[...]


def _modulation(c, ada_w, ada_b):
    rows = c.shape[0]
    tn = 1536
    return pl.pallas_call(
        _mod_kernel,
        grid=(N_MOD * D_MODEL // tn,),
        in_specs=[pl.BlockSpec((rows, D_MODEL), lambda n: (0, 0)),
                  pl.BlockSpec((D_MODEL, tn), lambda n: (0, n)),
                  pl.BlockSpec((1, tn), lambda n: (0, n))],
        out_specs=pl.BlockSpec((rows, tn), lambda n: (0, n)),
        out_shape=jax.ShapeDtypeStruct((rows, N_MOD * D_MODEL), F32),
        compiler_params=_params(("arbitrary",)),
        name="adaln_mod",
    )(c, ada_w, ada_b.reshape(1, -1))


def _fnet_weight_kernel(cs_ref, w_ref, o_ref):
    o_ref[0] = jnp.dot(cs_ref[...], w_ref[0], preferred_element_type=F32,
                       precision=lax.Precision.HIGHEST).astype(BF16)


def _fnet_channel_weights(w_fnet):
    return pl.pallas_call(
        _fnet_weight_kernel,
        grid=(FNET_GROUPS,),
        in_specs=[pl.BlockSpec((2 * GROUP_DIM, GROUP_DIM), lambda g: (0, 0)),
                  pl.BlockSpec((1, GROUP_DIM, GROUP_DIM), lambda g: (g, 0, 0))],
        out_specs=pl.BlockSpec((1, 2 * GROUP_DIM, GROUP_DIM), lambda g: (g, 0, 0)),
        out_shape=jax.ShapeDtypeStruct((FNET_GROUPS, 2 * GROUP_DIM, GROUP_DIM), BF16),
        compiler_params=_params(("arbitrary",)),
        name="fnet_channel_weights",
    )(_channel_dft(), w_fnet)


def _rms(x):
    return x * lax.rsqrt(jnp.mean(x * x, axis=-1, keepdims=True) + EPS)


def _kv_outer(k, vz):
    return lax.dot_general(k, vz, (((0,), (0,)), ((), ())), preferred_element_type=F32)


def _inproj_kernel(x_ref, mod_ref, gain_ref, w_ref, cos_ref, sin_ref, zeta_ref,
                   u_ref, q_ref, k_ref, v_ref, g_ref, sb_ref, state_ref, *, n_chunks, dec_b):
    @pl.when(pl.program_id(1) == 0)
    def _():
        state_ref[...] = jnp.zeros_like(state_ref)

    x = x_ref[0]
    a = gain_ref[...] * (1.0 + mod_ref[0, 1:2, :])
    h = (_rms(x) * a + mod_ref[0, 0:1, :]).astype(BF16)
    proj = jnp.dot(h, w_ref[...], preferred_element_type=F32)
    u_ref[0] = proj[:, :FNET_WIDTH].astype(BF16)
    cos, sin = cos_ref[...], sin_ref[...]
    for hd in range(RET_HEADS):
        lo = hd * HEAD_DIM
        for off, ref in ((FNET_WIDTH, q_ref), (FNET_WIDTH + RET_WIDTH, k_ref)):
            t = proj[:, off + lo:off + lo + HEAD_DIM]
            ref[0, :, lo:lo + HEAD_DIM] = (t * cos + pltpu.roll(t, HEAD_DIM // 2, 1) * sin).astype(BF16)
    v_ref[0] = proj[:, FNET_WIDTH + 2 * RET_WIDTH:FNET_WIDTH + 3 * RET_WIDTH].astype(BF16)
    g_ref[0] = proj[:, FNET_WIDTH + 3 * RET_WIDTH:].astype(BF16)

    for c in reversed(range(n_chunks)):
        rows = pl.ds(c * CHUNK, CHUNK)
        for hd in range(RET_HEADS):
            lanes = slice(hd * HEAD_DIM, (hd + 1) * HEAD_DIM)
            sb_ref[0, c, hd] = state_ref[hd].astype(BF16)
            vz = (v_ref[0, rows, lanes].astype(F32) * zeta_ref[hd]).astype(BF16)
            state_ref[hd] = state_ref[hd] * dec_b[hd] + _kv_outer(k_ref[0, rows, lanes], vz)


def _inproj(x, mod, gain, w_in, cos, sin, zeta_b, dec_b):
    b, s, _ = x.shape
    tm = min(TM_INPROJ, s)
    nt = s // tm
    nc = tm // CHUNK
    rev = lambda bi, ti: (bi, nt - 1 - ti, 0)
    tok = lambda width: pl.BlockSpec((1, tm, width), rev)
    const = lambda shape: pl.BlockSpec(shape, lambda bi, ti: (0,) * len(shape))
    act = jax.ShapeDtypeStruct((b, s, RET_WIDTH), BF16)
    return pl.pallas_call(
        functools.partial(_inproj_kernel, n_chunks=nc, dec_b=dec_b),
        grid=(b, nt),
        in_specs=[tok(D_MODEL),
                  pl.BlockSpec((1, N_MOD, D_MODEL), lambda bi, ti: (bi, 0, 0)),
                  const((1, D_MODEL)),
                  const((D_MODEL, IN_WIDTH)),
                  pl.BlockSpec((tm, HEAD_DIM), lambda bi, ti: (nt - 1 - ti, 0)),
                  pl.BlockSpec((tm, HEAD_DIM), lambda bi, ti: (nt - 1 - ti, 0)),
                  const((RET_HEADS, CHUNK, HEAD_DIM))],
        out_specs=[tok(FNET_WIDTH), tok(RET_WIDTH), tok(RET_WIDTH), tok(RET_WIDTH), tok(RET_WIDTH),
                   pl.BlockSpec((1, nc, RET_HEADS, HEAD_DIM, HEAD_DIM),
                                lambda bi, ti: (bi, nt - 1 - ti, 0, 0, 0))],
        out_shape=[jax.ShapeDtypeStruct((b, s, FNET_WIDTH), BF16), act, act, act, act,
                   jax.ShapeDtypeStruct((b, s // CHUNK, RET_HEADS, HEAD_DIM, HEAD_DIM), BF16)],
        scratch_shapes=[pltpu.VMEM((RET_HEADS, HEAD_DIM, HEAD_DIM), F32)],
        compiler_params=_params(("parallel", "arbitrary")),
        name="inproj",
    )(x, mod, gain, w_in, cos, sin, zeta_b)


def _fnet_a_kernel(x_ref, fa_ref, y_ref, *, n1):
    y = jnp.dot(fa_ref[...], x_ref[0], preferred_element_type=F32)
    y_ref[0, 0] = y[:n1].astype(BF16)
    y_ref[0, 1] = y[n1:].astype(BF16)


def _fnet_b_kernel(y_ref, t_ref, m_ref, o_ref, *, blocks, reps, n2):
    half = MXU_DIM // 2
    for j in range(blocks):
        pieces = [y_ref[0, c, j * reps + r] for c in range(2) for r in range(reps)]
        yb = pieces[0] if len(pieces) == 1 else jnp.concatenate(pieces, axis=0)
        z = jnp.dot(t_ref[j], yb, preferred_element_type=F32)
        for g in range(FNET_GROUPS):
            lanes = slice(g * GROUP_DIM, (g + 1) * GROUP_DIM)
            zz = jnp.concatenate([z[:half, lanes], z[half:, lanes]], axis=1).astype(BF16)
            res = jnp.dot(zz, m_ref[g], preferred_element_type=F32)
            for r in range(reps):
                col = (j * reps + r) * FNET_WIDTH + g * GROUP_DIM
                o_ref[0, :, col:col + GROUP_DIM] = res[r * n2:(r + 1) * n2].astype(BF16)


def _fnet(u, m_chan, n2):
    b, s, _ = u.shape
    n1 = s // n2
    reps = MXU_DIM // (2 * n2)
    row = n2 * FNET_WIDTH
    tn = min(row, 8192)
    y = pl.pallas_call(
        functools.partial(_fnet_a_kernel, n1=n1),
        grid=(b, row // tn),
        in_specs=[pl.BlockSpec((1, n1, tn), lambda bi, ni: (bi, 0, ni)),
                  pl.BlockSpec((2 * n1, n1), lambda bi, ni: (0, 0))],
        out_specs=pl.BlockSpec((1, 2, n1, tn), lambda bi, ni: (bi, 0, 0, ni)),
        out_shape=jax.ShapeDtypeStruct((b, 2, n1, row), BF16),
        compiler_params=_params(("parallel", "parallel")),
        name="fnet_stage_a",
    )(u.reshape(b, n1, row), _dft_stage_a(n1))

    k1_per_step = min(n1, 16)
    blocks = k1_per_step // reps
    steps = n1 // k1_per_step
    out = pl.pallas_call(
        functools.partial(_fnet_b_kernel, blocks=blocks, reps=reps, n2=n2),
        grid=(steps, b),
        in_specs=[pl.BlockSpec((1, 2, k1_per_step, n2, FNET_WIDTH), lambda ki, bi: (bi, 0, ki, 0, 0)),
                  pl.BlockSpec((blocks, MXU_DIM, MXU_DIM), lambda ki, bi: (ki, 0, 0)),
                  pl.BlockSpec((FNET_GROUPS, 2 * GROUP_DIM, GROUP_DIM), lambda ki, bi: (0, 0, 0))],
        out_specs=pl.BlockSpec((1, n2, k1_per_step * FNET_WIDTH), lambda ki, bi: (bi, 0, ki)),
        out_shape=jax.ShapeDtypeStruct((b, n2, n1 * FNET_WIDTH), BF16),
        compiler_params=_params(("parallel", "parallel")),
        name="fnet_stage_b",
    )(y.reshape(b, 2, n1, n2, FNET_WIDTH), _dft_stage_b(n1, n2), m_chan)
    return out.reshape(b, s, FNET_WIDTH)


def _mixer_kernel(x_ref, mod_ref, f_ref, q_ref, k_ref, v_ref, g_ref, sb_ref, w_ref,
                  intra_ref, xif_ref, xib_ref, zeta_ref, o_ref, state_ref, r_ref, *, n_chunks, dec_f):
    @pl.when(pl.program_id(1) == 0)
    def _():
        state_ref[...] = jnp.zeros_like(state_ref)

    for c in range(n_chunks):
        rows = pl.ds(c * CHUNK, CHUNK)
        for hd in range(RET_HEADS):
            lanes = slice(hd * HEAD_DIM, (hd + 1) * HEAD_DIM)
            q = q_ref[0, rows, lanes]
            k = k_ref[0, rows, lanes]
            v = v_ref[0, rows, lanes]
            qf = q.astype(F32)
            scores = lax.dot_general(q, k, (((1,), (1,)), ((), ())), preferred_element_type=F32)
            r = jnp.dot((scores * intra_ref[hd]).astype(BF16), v, preferred_element_type=F32)
            r += jnp.dot((qf * xif_ref[hd]).astype(BF16), state_ref[hd].astype(BF16),
                         preferred_element_type=F32)
            r += jnp.dot((qf * xib_ref[hd]).astype(BF16), sb_ref[0, c, hd], preferred_element_type=F32)
            vz = (v.astype(F32) * zeta_ref[hd]).astype(BF16)
            state_ref[hd] = state_ref[hd] * dec_f[hd] + _kv_outer(k, vz)
            gate = g_ref[0, rows, lanes].astype(F32)
            r_ref[rows, lanes] = (_rms(r) * (gate * jax.nn.sigmoid(gate))).astype(BF16)

    mix = jnp.dot(f_ref[0], w_ref[:FNET_WIDTH, :], preferred_element_type=F32)
    mix += jnp.dot(r_ref[...], w_ref[FNET_WIDTH:, :], preferred_element_type=F32)
    o_ref[0] = x_ref[0] + mod_ref[0, 2:3, :] * mix


def _mixer(x, mod, f, q, k, v, g, sb, w_out, tables):
    intra, xi_f, xi_b, zeta_f, dec_f = tables
    b, s, _ = x.shape
    tm = min(TM_MIXER, s)
    nc = tm // CHUNK
    tok = lambda width: pl.BlockSpec((1, tm, width), lambda bi, ti: (bi, ti, 0))
    const = lambda shape: pl.BlockSpec(shape, lambda bi, ti: (0,) * len(shape))
    tab = const((RET_HEADS, CHUNK, HEAD_DIM))
    return pl.pallas_call(
        functools.partial(_mixer_kernel, n_chunks=nc, dec_f=dec_f),
        grid=(b, s // tm),
        in_specs=[tok(D_MODEL),
                  pl.BlockSpec((1, N_MOD, D_MODEL), lambda bi, ti: (bi, 0, 0)),
                  tok(FNET_WIDTH), tok(RET_WIDTH), tok(RET_WIDTH), tok(RET_WIDTH), tok(RET_WIDTH),
                  pl.BlockSpec((1, nc, RET_HEADS, HEAD_DIM, HEAD_DIM), lambda bi, ti: (bi, ti, 0, 0, 0)),
                  const((D_MODEL, D_MODEL)), tab, tab, tab, tab],
        out_specs=tok(D_MODEL),
        out_shape=jax.ShapeDtypeStruct((b, s, D_MODEL), F32),
        scratch_shapes=[pltpu.VMEM((RET_HEADS, HEAD_DIM, HEAD_DIM), F32),
                        pltpu.VMEM((tm, RET_WIDTH), BF16)],
        compiler_params=_params(("parallel", "arbitrary")),
        name="mixer",
    )(x, mod, f, q, k, v, g, sb, w_out, intra, xi_f, xi_b, zeta_f)


def _mlp_kernel(x_ref, mod_ref, gain_ref, w1_ref, w2_ref, gfin_ref, o_ref):
    x = x_ref[0]
    a = gain_ref[...] * (1.0 + mod_ref[0, 4:5, :])
    h = (_rms(x) * a + mod_ref[0, 3:4, :]).astype(BF16)
    mid = jnp.dot(h, w1_ref[...], preferred_element_type=F32)
    mid = jnp.square(jnp.maximum(mid, 0.0)).astype(BF16)
    y = x + mod_ref[0, 5:6, :] * jnp.dot(mid, w2_ref[...], preferred_element_type=F32)
    o_ref[0] = _rms(y) * gfin_ref[...]


def _mlp(x, mod, gain, w1, w2, gain_final):
    b, s, _ = x.shape
    tm = min(TM_MLP, s)
    tok = pl.BlockSpec((1, tm, D_MODEL), lambda bi, ti: (bi, ti, 0))
    const = lambda shape: pl.BlockSpec(shape, lambda bi, ti: (0,) * len(shape))
    return pl.pallas_call(
        _mlp_kernel,
        grid=(b, s // tm),
        in_specs=[tok, pl.BlockSpec((1, N_MOD, D_MODEL), lambda bi, ti: (bi, 0, 0)),
                  const((1, D_MODEL)), const((D_MODEL, D_FF)), const((D_FF, D_MODEL)), const((1, D_MODEL))],
        out_specs=tok,
        out_shape=jax.ShapeDtypeStruct((b, s, D_MODEL), F32),
        compiler_params=_params(("parallel", "parallel")),
        name="mlp",
    )(x, mod, gain, w1, w2, gain_final)


def _dft_n2(seq):
    n2 = seq // DFT_N1
    assert seq == DFT_N1 * n2 and MXU_DIM % (2 * n2) == 0, seq
    return n2


def _trunk(x, mod, weights, tables, n2=None):
    gain_mix, w_in, m_chan, w_out, gain_mlp, w1, w2, gain_final = weights
    intra, xi_f, xi_b, zeta_f, zeta_b, dec_f, dec_b = tables
    seq = x.shape[1]
    cos, sin = _rotary_tables(seq)
    u, q, k, v, g, sb = _inproj(x, mod, gain_mix, w_in, cos, sin, zeta_b, dec_b)
    f = _fnet(u, m_chan, _dft_n2(seq) if n2 is None else n2)
    x1 = _mixer(x, mod, f, q, k, v, g, sb, w_out, (intra, xi_f, xi_b, zeta_f, dec_f))
    return _mlp(x1, mod, gain_mlp, w1, w2, gain_final)


def kernel(x_prompt, x_sample, c_prompt, c_sample, ada_w, ada_b, norm_mix, w_in, w_fnet, w_out,
           norm_mlp, w_mlp_in, w_mlp_out, norm_final):
    assert ada_w.shape[0] == 1, "one encoder layer"
    bp, bs = x_prompt.shape[0], x_sample.shape[0]
    c_all = jnp.concatenate([c_prompt, c_sample], axis=0)
    mod = _modulation(c_all, ada_w[0], ada_b[0]).reshape(bp + bs, N_MOD, D_MODEL)
    weights = (norm_mix[0].reshape(1, -1), w_in[0].astype(BF16), _fnet_channel_weights(w_fnet[0]),
               w_out[0].astype(BF16), norm_mlp[0].reshape(1, -1), w_mlp_in[0].astype(BF16),
               w_mlp_out[0].astype(BF16), norm_final.reshape(1, -1))
    tables = _retention_tables()
    y_prompt = _trunk(x_prompt, mod[:bp], weights, tables)
    y_sample = _trunk(x_sample, mod[bp:], weights, tables)
    return (y_prompt, y_sample)
```

```python
import functools

import numpy as np
import jax
import jax.numpy as jnp
from jax import lax
from jax.experimental import pallas as pl
from jax.experimental.pallas import tpu as pltpu

F32 = jnp.float32
BF16 = jnp.bfloat16

D_MODEL = 1024
FNET_WIDTH = 512
FNET_GROUPS = 4
GROUP_DIM = 128
RET_WIDTH = 512
RET_HEADS = 4
HEAD_DIM = 128
D_FF = 4096
CHUNK = 128
ROPE_BASE = 10000.0
EPS = 1e-6
N_MOD = 6
IN_WIDTH = FNET_WIDTH + 4 * RET_WIDTH
DECAY_OFFSET_FWD = 0.0
DECAY_OFFSET_BWD = 0.5

DFT_N1 = 128
MXU_DIM = 256
BF16_ROWS = 16
VMEM_LIMIT = 56 * 1024 * 1024

TM_INPROJ = 512
TM_MIXER = 512
TM_MLP = 512


def _params(semantics):
    return pltpu.CompilerParams(dimension_semantics=semantics, vmem_limit_bytes=VMEM_LIMIT)


def _retention_tables():
    scale = HEAD_DIM ** -0.5
    h = np.arange(RET_HEADS, dtype=np.float64)
    lg_f = np.log1p(-np.exp2(-5.0 - DECAY_OFFSET_FWD - h))[:, None, None]
    lg_b = np.log1p(-np.exp2(-5.0 - DECAY_OFFSET_BWD - h))[:, None, None]
    i = np.arange(CHUNK, dtype=np.float64)[None, :, None]
    j = np.arange(CHUNK, dtype=np.float64)[None, None, :]
    ones = np.ones((1, 1, HEAD_DIM))
    intra = scale * np.where(i >= j, np.exp(lg_f * np.maximum(i - j, 0.0)), np.exp(lg_b * np.maximum(j - i, 0.0)))
    xi_f = np.exp(lg_f * (i + 1.0)) * ones
    xi_b = np.exp(lg_b * (CHUNK - i)) * ones
    zeta_f = scale * np.exp(lg_f * (CHUNK - 1.0 - i)) * ones
    zeta_b = scale * np.exp(lg_b * i) * ones
    dec_f = tuple(float(v) for v in np.exp(lg_f[:, 0, 0] * CHUNK))
    dec_b = tuple(float(v) for v in np.exp(lg_b[:, 0, 0] * CHUNK))
    f = lambda a: jnp.asarray(a.astype(np.float32))
    return f(intra), f(xi_f), f(xi_b), f(zeta_f), f(zeta_b), dec_f, dec_b


def _f32(a):
    return jnp.asarray(np.asarray(a).astype(np.float32))


def _rotary_tables(seq, tm):
    half = HEAD_DIM // 2
    inv = ROPE_BASE ** (-np.arange(half, dtype=np.float64) / half)
    inv = np.concatenate([inv, inv])
    sign = np.concatenate([-np.ones(half), np.ones(half)])
    base = np.arange(0, seq, tm, dtype=np.float64)[:, None] * inv[None, :]
    local = np.arange(tm, dtype=np.float64)[:, None] * inv[None, :]
    base_rows = np.stack([np.cos(base), np.sin(base), sign * np.cos(base), sign * np.sin(base)], axis=1)
    return _f32(base_rows), _f32(np.stack([np.cos(local), np.sin(local)], axis=0))


def _dft_stage_a(n1):
    k = np.arange(n1, dtype=np.int64)
    ang = 2.0 * np.pi * ((k[:, None] * k[None, :]) % n1) / n1
    return _f32(np.concatenate([np.cos(ang), -np.sin(ang)], axis=0) * (n1 ** -0.5)).astype(BF16)


def _dft_stage_b(n1, n2):
    reps = MXU_DIM // (2 * n2)
    seq = n1 * n2
    idx = np.arange(MXU_DIM, dtype=np.int64)
    c_i, r_i, p_i = idx // (reps * n2), (idx // n2) % reps, idx % n2
    alpha = 2.0 * np.pi * ((p_i[:, None] * p_i[None, :]) % n2) / n2
    ca, sa = np.cos(alpha), np.sin(alpha)
    c_out, c_in = c_i[:, None], c_i[None, :]
    p_mat = np.where(c_out == c_in, ca, np.where(c_out < c_in, sa, -sa))
    q_mat = np.where(c_out == c_in, -sa, np.where(c_out < c_in, ca, -ca))
    keep = (r_i[:, None] == r_i[None, :]) * ((n2 * GROUP_DIM) ** -0.5)
    blk = np.arange(n1 // reps, dtype=np.int64)[:, None]
    beta = 2.0 * np.pi * ((p_i[None, :] * (blk * reps + r_i[None, :])) % seq) / seq
    return _f32(p_mat * keep), _f32(q_mat * keep), _f32(np.cos(beta)[:, None, :]), _f32(np.sin(beta)[:, None, :])


def _channel_dft():
    c = np.arange(GROUP_DIM, dtype=np.int64)
    ang = 2.0 * np.pi * ((c[:, None] * c[None, :]) % GROUP_DIM) / GROUP_DIM
    return jnp.asarray(np.concatenate([np.cos(ang), np.sin(ang)], axis=0).astype(np.float32))


def _mod_kernel(c_ref, w_ref, b_ref, o_ref):
    c = c_ref[...]
    s = c * jax.nn.sigmoid(c)
    o_ref[...] = jnp.dot(s, w_ref[...], preferred_element_type=F32,
                         precision=lax.Precision.HIGHEST) + b_ref[...]


def _modulation(c, ada_w, ada_b):
    rows = c.shape[0]
    tn = 1536
    return pl.pallas_call(
        _mod_kernel,
        grid=(N_MOD * D_MODEL // tn,),
        in_specs=[pl.BlockSpec((rows, D_MODEL), lambda n: (0, 0)),
                  pl.BlockSpec((D_MODEL, tn), lambda n: (0, n)),
                  pl.BlockSpec((1, tn), lambda n: (0, n))],
        out_specs=pl.BlockSpec((rows, tn), lambda n: (0, n)),
        out_shape=jax.ShapeDtypeStruct((rows, N_MOD * D_MODEL), F32),
        compiler_params=_params(("arbitrary",)),
        name="adaln_mod",
    )(c, ada_w, ada_b.reshape(1, -1))


def _fnet_weight_kernel(cs_ref, w_ref, o_ref):
    o_ref[0] = jnp.dot(cs_ref[...], w_ref[0], preferred_element_type=F32,
                       precision=lax.Precision.HIGHEST).astype(BF16)


def _fnet_channel_weights(w_fnet):
    return pl.pallas_call(
        _fnet_weight_kernel,
        grid=(FNET_GROUPS,),
        in_specs=[pl.BlockSpec((2 * GROUP_DIM, GROUP_DIM), lambda g: (0, 0)),
                  pl.BlockSpec((1, GROUP_DIM, GROUP_DIM), lambda g: (g, 0, 0))],
        out_specs=pl.BlockSpec((1, 2 * GROUP_DIM, GROUP_DIM), lambda g: (g, 0, 0)),
        out_shape=jax.ShapeDtypeStruct((FNET_GROUPS, 2 * GROUP_DIM, GROUP_DIM), BF16),
        compiler_params=_params(("arbitrary",)),
        name="fnet_channel_weights",
    )(_channel_dft(), w_fnet)


def _rms(x):
    return x * lax.rsqrt(jnp.mean(x * x, axis=-1, keepdims=True) + EPS)


def _kv_outer(k, vz):
    return lax.dot_general(k, vz, (((0,), (0,)), ((), ())), preferred_element_type=F32)


def _inproj_kernel(x_ref, mod_ref, gain_ref, w_ref, rbase_ref, rlocal_ref, zeta_ref,
                   u_ref, q_ref, k_ref, v_ref, g_ref, sb_ref, state_ref, *, n_chunks, dec_b):
    @pl.when(pl.program_id(1) == 0)
    def _():
        state_ref[...] = jnp.zeros_like(state_ref)

    x = x_ref[0]
    a = gain_ref[...] * (1.0 + mod_ref[0, 1:2, :])
    h = (_rms(x) * a + mod_ref[0, 0:1, :]).astype(BF16)
    proj = jnp.dot(h, w_ref[...], preferred_element_type=F32)
    u_ref[0] = proj[:, :FNET_WIDTH].astype(BF16)
    cos_l, sin_l = rlocal_ref[0], rlocal_ref[1]
    cos = rbase_ref[0, 0:1, :] * cos_l - rbase_ref[0, 1:2, :] * sin_l
    sin = rbase_ref[0, 3:4, :] * cos_l + rbase_ref[0, 2:3, :] * sin_l
    for hd in range(RET_HEADS):
        lo = hd * HEAD_DIM
        for off, ref in ((FNET_WIDTH, q_ref), (FNET_WIDTH + RET_WIDTH, k_ref)):
            t = proj[:, off + lo:off + lo + HEAD_DIM]
            ref[0, :, lo:lo + HEAD_DIM] = (t * cos + pltpu.roll(t, HEAD_DIM // 2, 1) * sin).astype(BF16)
    v_ref[0] = proj[:, FNET_WIDTH + 2 * RET_WIDTH:FNET_WIDTH + 3 * RET_WIDTH].astype(BF16)
    g_ref[0] = proj[:, FNET_WIDTH + 3 * RET_WIDTH:].astype(BF16)

    for c in reversed(range(n_chunks)):
        rows = pl.ds(c * CHUNK, CHUNK)
        for hd in range(RET_HEADS):
            lanes = slice(hd * HEAD_DIM, (hd + 1) * HEAD_DIM)
            sb_ref[0, c, hd] = state_ref[hd].astype(BF16)
            vz = (v_ref[0, rows, lanes].astype(F32) * zeta_ref[hd]).astype(BF16)
            state_ref[hd] = state_ref[hd] * dec_b[hd] + _kv_outer(k_ref[0, rows, lanes], vz)


def _inproj(x, mod, gain, w_in, zeta_b, dec_b):
    b, s, _ = x.shape
    tm = min(TM_INPROJ, s)
    nt = s // tm
    nc = tm // CHUNK
    rot_base, rot_local = _rotary_tables(s, tm)
    rev = lambda bi, ti: (bi, nt - 1 - ti, 0)
    tok = lambda width: pl.BlockSpec((1, tm, width), rev)
    const = lambda shape: pl.BlockSpec(shape, lambda bi, ti: (0,) * len(shape))
    act = jax.ShapeDtypeStruct((b, s, RET_WIDTH), BF16)
    return pl.pallas_call(
        functools.partial(_inproj_kernel, n_chunks=nc, dec_b=dec_b),
        grid=(b, nt),
        in_specs=[tok(D_MODEL),
                  pl.BlockSpec((1, N_MOD, D_MODEL), lambda bi, ti: (bi, 0, 0)),
                  const((1, D_MODEL)),
                  const((D_MODEL, IN_WIDTH)),
                  pl.BlockSpec((1, 4, HEAD_DIM), lambda bi, ti: (nt - 1 - ti, 0, 0)),
                  const((2, tm, HEAD_DIM)),
                  const((RET_HEADS, CHUNK, HEAD_DIM))],
        out_specs=[tok(FNET_WIDTH), tok(RET_WIDTH), tok(RET_WIDTH), tok(RET_WIDTH), tok(RET_WIDTH),
                   pl.BlockSpec((1, nc, RET_HEADS, HEAD_DIM, HEAD_DIM),
                                lambda bi, ti: (bi, nt - 1 - ti, 0, 0, 0))],
        out_shape=[jax.ShapeDtypeStruct((b, s, FNET_WIDTH), BF16), act, act, act, act,
                   jax.ShapeDtypeStruct((b, s // CHUNK, RET_HEADS, HEAD_DIM, HEAD_DIM), BF16)],
        scratch_shapes=[pltpu.VMEM((RET_HEADS, HEAD_DIM, HEAD_DIM), F32)],
        compiler_params=_params(("parallel", "arbitrary")),
        name="inproj",
    )(x, mod, gain, w_in, rot_base, rot_local, zeta_b)


def _fnet_a_kernel(x_ref, fa_ref, y_ref, xs_ref, ys_ref, *, n1, tb):
    groups = [slice(g * GROUP_DIM, (g + 1) * GROUP_DIM) for g in range(FNET_GROUPS)]
    for g, lanes in enumerate(groups):
        xs_ref[g] = x_ref[0, :, :, lanes].reshape(n1 * tb, GROUP_DIM).astype(F32)
    for j in range(tb):
        xj = jnp.concatenate([xs_ref[g, pl.ds(j, n1, stride=tb), :] for g in range(FNET_GROUPS)], axis=1)
        yj = jnp.dot(fa_ref[...], xj.astype(BF16), preferred_element_type=F32)
        for g, lanes in enumerate(groups):
            ys_ref[g, pl.ds(j, 2 * n1, stride=tb), :] = yj[:, lanes]
    for g, lanes in enumerate(groups):
        y_ref[0, :, :, :, lanes] = ys_ref[g].astype(BF16).reshape(2, n1, tb, GROUP_DIM)


def _fnet_b_kernel(y_ref, p_ref, q_ref, cb_ref, sb_ref, m_ref, o_ref, os_ref, *, blocks, reps, n2):
    half = MXU_DIM // 2
    k1_step = blocks * reps
    for j in range(blocks):
        t = (p_ref[...] * cb_ref[j] + q_ref[...] * sb_ref[j]).astype(BF16)
        pieces = [y_ref[0, c, j * reps + r] for c in range(2) for r in range(reps)]
        yb = pieces[0] if len(pieces) == 1 else jnp.concatenate(pieces, axis=0)
        z = jnp.dot(t, yb, preferred_element_type=F32)
        for g in range(FNET_GROUPS):
            lanes = slice(g * GROUP_DIM, (g + 1) * GROUP_DIM)
            zz = jnp.concatenate([z[:half, lanes], z[half:, lanes]], axis=1).astype(BF16)
            res = jnp.dot(zz, m_ref[g], preferred_element_type=F32)
            for r in range(reps):
                os_ref[g, pl.ds(j * reps + r, n2, stride=k1_step), :] = res[r * n2:(r + 1) * n2]
    for g in range(FNET_GROUPS):
        o_ref[0, :, :, g * GROUP_DIM:(g + 1) * GROUP_DIM] = (
            os_ref[g].astype(BF16).reshape(n2, k1_step, GROUP_DIM))


def _fnet(u, m_chan, n2):
    b, s, _ = u.shape
    n1 = s // n2
    reps = MXU_DIM // (2 * n2)
    tb = BF16_ROWS
    y = pl.pallas_call(
        functools.partial(_fnet_a_kernel, n1=n1, tb=tb),
        grid=(b, n2 // tb),
        in_specs=[pl.BlockSpec((1, n1, tb, FNET_WIDTH), lambda bi, ni: (bi, 0, ni, 0)),
                  pl.BlockSpec((2 * n1, n1), lambda bi, ni: (0, 0))],
        out_specs=pl.BlockSpec((1, 2, n1, tb, FNET_WIDTH), lambda bi, ni: (bi, 0, 0, ni, 0)),
        out_shape=jax.ShapeDtypeStruct((b, 2, n1, n2, FNET_WIDTH), BF16),
        scratch_shapes=[pltpu.VMEM((FNET_GROUPS, n1 * tb, GROUP_DIM), F32),
                        pltpu.VMEM((FNET_GROUPS, 2 * n1 * tb, GROUP_DIM), F32)],
        compiler_params=_params(("parallel", "parallel")),
        name="fnet_stage_a",
    )(u.reshape(b, n1, n2, FNET_WIDTH), _dft_stage_a(n1))

    k1_step = BF16_ROWS
    blocks = k1_step // reps
    p_mat, q_mat, cos_b, sin_b = _dft_stage_b(n1, n2)
    square = pl.BlockSpec((MXU_DIM, MXU_DIM), lambda ki, bi: (0, 0))
    phase = pl.BlockSpec((blocks, 1, MXU_DIM), lambda ki, bi: (ki, 0, 0))
    out = pl.pallas_call(
        functools.partial(_fnet_b_kernel, blocks=blocks, reps=reps, n2=n2),
        grid=(n1 // k1_step, b),
        in_specs=[pl.BlockSpec((1, 2, k1_step, n2, FNET_WIDTH), lambda ki, bi: (bi, 0, ki, 0, 0)),
                  square, square, phase, phase,
                  pl.BlockSpec((FNET_GROUPS, 2 * GROUP_DIM, GROUP_DIM), lambda ki, bi: (0, 0, 0))],
        out_specs=pl.BlockSpec((1, n2, k1_step, FNET_WIDTH), lambda ki, bi: (bi, 0, ki, 0)),
        out_shape=jax.ShapeDtypeStruct((b, n2, n1, FNET_WIDTH), BF16),
        scratch_shapes=[pltpu.VMEM((FNET_GROUPS, n2 * k1_step, GROUP_DIM), F32)],
        compiler_params=_params(("parallel", "parallel")),
        name="fnet_stage_b",
    )(y, p_mat, q_mat, cos_b, sin_b, m_chan)
    return out.reshape(b, s, FNET_WIDTH)


def _mixer_kernel(x_ref, mod_ref, f_ref, q_ref, k_ref, v_ref, g_ref, sb_ref, w_ref,
                  intra_ref, xif_ref, xib_ref, zeta_ref, o_ref, state_ref, r_ref, *, n_chunks, dec_f):
    @pl.when(pl.program_id(1) == 0)
    def _():
        state_ref[...] = jnp.zeros_like(state_ref)

    for c in range(n_chunks):
        rows = pl.ds(c * CHUNK, CHUNK)
        for hd in range(RET_HEADS):
            lanes = slice(hd * HEAD_DIM, (hd + 1) * HEAD_DIM)
            q = q_ref[0, rows, lanes]
            k = k_ref[0, rows, lanes]
            v = v_ref[0, rows, lanes]
            qf = q.astype(F32)
            scores = lax.dot_general(q, k, (((1,), (1,)), ((), ())), preferred_element_type=F32)
            r = jnp.dot((scores * intra_ref[hd]).astype(BF16), v, preferred_element_type=F32)
            r += jnp.dot((qf * xif_ref[hd]).astype(BF16), state_ref[hd].astype(BF16),
                         preferred_element_type=F32)
            r += jnp.dot((qf * xib_ref[hd]).astype(BF16), sb_ref[0, c, hd], preferred_element_type=F32)
            vz = (v.astype(F32) * zeta_ref[hd]).astype(BF16)
            state_ref[hd] = state_ref[hd] * dec_f[hd] + _kv_outer(k, vz)
            gate = g_ref[0, rows, lanes].astype(F32)
            r_ref[rows, lanes] = (_rms(r) * (gate * jax.nn.sigmoid(gate))).astype(BF16)

    mix = jnp.dot(f_ref[0], w_ref[:FNET_WIDTH, :], preferred_element_type=F32)
    mix += jnp.dot(r_ref[...], w_ref[FNET_WIDTH:, :], preferred_element_type=F32)
    o_ref[0] = x_ref[0] + mod_ref[0, 2:3, :] * mix


def _mixer(x, mod, f, q, k, v, g, sb, w_out, tables):
    intra, xi_f, xi_b, zeta_f, dec_f = tables
    b, s, _ = x.shape
    tm = min(TM_MIXER, s)
    nc = tm // CHUNK
    tok = lambda width: pl.BlockSpec((1, tm, width), lambda bi, ti: (bi, ti, 0))
    const = lambda shape: pl.BlockSpec(shape, lambda bi, ti: (0,) * len(shape))
    tab = const((RET_HEADS, CHUNK, HEAD_DIM))
    return pl.pallas_call(
        functools.partial(_mixer_kernel, n_chunks=nc, dec_f=dec_f),
        grid=(b, s // tm),
        in_specs=[tok(D_MODEL),
                  pl.BlockSpec((1, N_MOD, D_MODEL), lambda bi, ti: (bi, 0, 0)),
                  tok(FNET_WIDTH), tok(RET_WIDTH), tok(RET_WIDTH), tok(RET_WIDTH), tok(RET_WIDTH),
                  pl.BlockSpec((1, nc, RET_HEADS, HEAD_DIM, HEAD_DIM), lambda bi, ti: (bi, ti, 0, 0, 0)),
                  const((D_MODEL, D_MODEL)), tab, tab, tab, tab],
        out_specs=tok(D_MODEL),
        out_shape=jax.ShapeDtypeStruct((b, s, D_MODEL), F32),
        scratch_shapes=[pltpu.VMEM((RET_HEADS, HEAD_DIM, HEAD_DIM), F32),
                        pltpu.VMEM((tm, RET_WIDTH), BF16)],
        compiler_params=_params(("parallel", "arbitrary")),
        name="mixer",
    )(x, mod, f, q, k, v, g, sb, w_out, intra, xi_f, xi_b, zeta_f)


def _mlp_kernel(x_ref, mod_ref, gain_ref, w1_ref, w2_ref, gfin_ref, o_ref):
    x = x_ref[0]
    a = gain_ref[...] * (1.0 + mod_ref[0, 4:5, :])
    h = (_rms(x) * a + mod_ref[0, 3:4, :]).astype(BF16)
    mid = jnp.dot(h, w1_ref[...], preferred_element_type=F32)
    mid = jnp.square(jnp.maximum(mid, 0.0)).astype(BF16)
    y = x + mod_ref[0, 5:6, :] * jnp.dot(mid, w2_ref[...], preferred_element_type=F32)
    o_ref[0] = _rms(y) * gfin_ref[...]


def _mlp(x, mod, gain, w1, w2, gain_final):
    b, s, _ = x.shape
    tm = min(TM_MLP, s)
    tok = pl.BlockSpec((1, tm, D_MODEL), lambda bi, ti: (bi, ti, 0))
    const = lambda shape: pl.BlockSpec(shape, lambda bi, ti: (0,) * len(shape))
    return pl.pallas_call(
        _mlp_kernel,
        grid=(b, s // tm),
        in_specs=[tok, pl.BlockSpec((1, N_MOD, D_MODEL), lambda bi, ti: (bi, 0, 0)),
                  const((1, D_MODEL)), const((D_MODEL, D_FF)), const((D_FF, D_MODEL)), const((1, D_MODEL))],
        out_specs=tok,
        out_shape=jax.ShapeDtypeStruct((b, s, D_MODEL), F32),
        compiler_params=_params(("parallel", "parallel")),
        name="mlp",
    )(x, mod, gain, w1, w2, gain_final)


def _dft_n2(seq):
    n2 = seq // DFT_N1
    assert seq == DFT_N1 * n2 and MXU_DIM % (2 * n2) == 0, seq
    return n2


def _trunk(x, mod, weights, tables, n2=None):
    gain_mix, w_in, m_chan, w_out, gain_mlp, w1, w2, gain_final = weights
    intra, xi_f, xi_b, zeta_f, zeta_b, dec_f, dec_b = tables
    seq = x.shape[1]
    u, q, k, v, g, sb = _inproj(x, mod, gain_mix, w_in, zeta_b, dec_b)
    f = _fnet(u, m_chan, _dft_n2(seq) if n2 is None else n2)
    x1 = _mixer(x, mod, f, q, k, v, g, sb, w_out, (intra, xi_f, xi_b, zeta_f, dec_f))
    return _mlp(x1, mod, gain_mlp, w1, w2, gain_final)


def kernel(x_prompt, x_sample, c_prompt, c_sample, ada_w, ada_b, norm_mix, w_in, w_fnet, w_out,
           norm_mlp, w_mlp_in, w_mlp_out, norm_final):
    assert ada_w.shape[0] == 1, "one encoder layer"
    bp, bs = x_prompt.shape[0], x_sample.shape[0]
    c_all = jnp.concatenate([c_prompt, c_sample], axis=0)
    mod = _modulation(c_all, ada_w[0], ada_b[0]).reshape(bp + bs, N_MOD, D_MODEL)
    weights = (norm_mix[0].reshape(1, -1), w_in[0].astype(BF16), _fnet_channel_weights(w_fnet[0]),
               w_out[0].astype(BF16), norm_mlp[0].reshape(1, -1), w_mlp_in[0].astype(BF16),
               w_mlp_out[0].astype(BF16), norm_final.reshape(1, -1))
    tables = _retention_tables()
    y_prompt = _trunk(x_prompt, mod[:bp], weights, tables)
    y_sample = _trunk(x_sample, mod[bp:], weights, tables)
    return (y_prompt, y_sample)
```

```python
import functools

import numpy as np
import jax
import jax.numpy as jnp
from jax import lax
from jax.experimental import pallas as pl
from jax.experimental.pallas import tpu as pltpu

F32 = jnp.float32
BF16 = jnp.bfloat16

D_MODEL = 1024
FNET_WIDTH = 512
FNET_GROUPS = 4
GROUP_DIM = 128
RET_WIDTH = 512
RET_HEADS = 4
HEAD_DIM = 128
D_FF = 4096
CHUNK = 128
ROPE_BASE = 10000.0
EPS = 1e-6
N_MOD = 6
IN_WIDTH = FNET_WIDTH + 4 * RET_WIDTH
DECAY_OFFSET_FWD = 0.0
DECAY_OFFSET_BWD = 0.5

DFT_N1 = 128
MXU_DIM = 256
BF16_ROWS = 16
F32_ROWS = 8
STAGE_B_MXU_BLOCKS = 16
VMEM_LIMIT = 56 * 1024 * 1024

TM_INPROJ = 512
TM_MIXER = 512
TM_MLP = 512


def _params(semantics):
    return pltpu.CompilerParams(dimension_semantics=semantics, vmem_limit_bytes=VMEM_LIMIT)


def _retention_tables():
    scale = HEAD_DIM ** -0.5
    h = np.arange(RET_HEADS, dtype=np.float64)
    lg_f = np.log1p(-np.exp2(-5.0 - DECAY_OFFSET_FWD - h))[:, None, None]
    lg_b = np.log1p(-np.exp2(-5.0 - DECAY_OFFSET_BWD - h))[:, None, None]
    i = np.arange(CHUNK, dtype=np.float64)[None, :, None]
    j = np.arange(CHUNK, dtype=np.float64)[None, None, :]
    ones = np.ones((1, 1, HEAD_DIM))
    intra = scale * np.where(i >= j, np.exp(lg_f * np.maximum(i - j, 0.0)), np.exp(lg_b * np.maximum(j - i, 0.0)))
    xi_f = np.exp(lg_f * (i + 1.0)) * ones
    xi_b = np.exp(lg_b * (CHUNK - i)) * ones
    zeta_f = scale * np.exp(lg_f * (CHUNK - 1.0 - i)) * ones
    zeta_b = scale * np.exp(lg_b * i) * ones
    dec_f = tuple(float(v) for v in np.exp(lg_f[:, 0, 0] * CHUNK))
    dec_b = tuple(float(v) for v in np.exp(lg_b[:, 0, 0] * CHUNK))
    f = lambda a: jnp.asarray(a.astype(np.float32))
    return f(intra), f(xi_f), f(xi_b), f(zeta_f), f(zeta_b), dec_f, dec_b


def _f32(a):
    return jnp.asarray(np.asarray(a).astype(np.float32))


def _rotary_tables(seq, tm):
    half = HEAD_DIM // 2
    inv = ROPE_BASE ** (-np.arange(half, dtype=np.float64) / half)
    inv = np.concatenate([inv, inv])
    sign = np.concatenate([-np.ones(half), np.ones(half)])
    base = np.arange(0, seq, tm, dtype=np.float64)[:, None] * inv[None, :]
    local = np.arange(tm, dtype=np.float64)[:, None] * inv[None, :]
    base_rows = np.stack([np.cos(base), np.sin(base), sign * np.cos(base), sign * np.sin(base)], axis=1)
    return _f32(base_rows), _f32(np.stack([np.cos(local), np.sin(local)], axis=0))


def _dft_stage_a(n1):
    k = np.arange(n1, dtype=np.int64)
    ang = 2.0 * np.pi * ((k[:, None] * k[None, :]) % n1) / n1
    return _f32(np.concatenate([np.cos(ang), -np.sin(ang)], axis=0) * (n1 ** -0.5)).astype(BF16)


def _dft_stage_b(n1, n2):
    reps = MXU_DIM // (2 * n2)
    seq = n1 * n2
    idx = np.arange(MXU_DIM, dtype=np.int64)
    row_c, row_r, row_k = idx // (reps * n2), (idx // n2) % reps, idx % n2
    col_r, col_s, col_c = idx // (2 * n2), (idx // 2) % n2, idx % 2
    alpha = 2.0 * np.pi * ((row_k[:, None] * col_s[None, :]) % n2) / n2
    ca, sa = np.cos(alpha), np.sin(alpha)
    c_out, c_in = row_c[:, None], col_c[None, :]
    p_mat = np.where(c_out == c_in, ca, np.where(c_out < c_in, sa, -sa))
    q_mat = np.where(c_out == c_in, -sa, np.where(c_out < c_in, ca, -ca))
    keep = (row_r[:, None] == col_r[None, :]) * ((n2 * GROUP_DIM) ** -0.5)
    blk = np.arange(n1 // reps, dtype=np.int64)[:, None]
    beta = 2.0 * np.pi * ((col_s[None, :] * (blk * reps + col_r[None, :])) % seq) / seq
    return _f32(p_mat * keep), _f32(q_mat * keep), _f32(np.cos(beta)[:, None, :]), _f32(np.sin(beta)[:, None, :])


def _channel_dft():
    c = np.arange(GROUP_DIM, dtype=np.int64)
    ang = 2.0 * np.pi * ((c[:, None] * c[None, :]) % GROUP_DIM) / GROUP_DIM
    return jnp.asarray(np.concatenate([np.cos(ang), np.sin(ang)], axis=0).astype(np.float32))


def _mod_kernel(c_ref, w_ref, b_ref, o_ref):
    c = c_ref[...]
    s = c * jax.nn.sigmoid(c)
    o_ref[...] = jnp.dot(s, w_ref[...], preferred_element_type=F32,
                         precision=lax.Precision.HIGHEST) + b_ref[...]


def _modulation(c, ada_w, ada_b):
    rows = c.shape[0]
    tn = 1536
    return pl.pallas_call(
        _mod_kernel,
        grid=(N_MOD * D_MODEL // tn,),
        in_specs=[pl.BlockSpec((rows, D_MODEL), lambda n: (0, 0)),
                  pl.BlockSpec((D_MODEL, tn), lambda n: (0, n)),
                  pl.BlockSpec((1, tn), lambda n: (0, n))],
        out_specs=pl.BlockSpec((rows, tn), lambda n: (0, n)),
        out_shape=jax.ShapeDtypeStruct((rows, N_MOD * D_MODEL), F32),
        compiler_params=_params(("arbitrary",)),
        name="adaln_mod",
    )(c, ada_w, ada_b.reshape(1, -1))


def _fnet_weight_kernel(cs_ref, w_ref, o_ref):
    o_ref[0] = jnp.dot(cs_ref[...], w_ref[0], preferred_element_type=F32,
                       precision=lax.Precision.HIGHEST).astype(BF16)


def _fnet_channel_weights(w_fnet):
    return pl.pallas_call(
        _fnet_weight_kernel,
        grid=(FNET_GROUPS,),
        in_specs=[pl.BlockSpec((2 * GROUP_DIM, GROUP_DIM), lambda g: (0, 0)),
                  pl.BlockSpec((1, GROUP_DIM, GROUP_DIM), lambda g: (g, 0, 0))],
        out_specs=pl.BlockSpec((1, 2 * GROUP_DIM, GROUP_DIM), lambda g: (g, 0, 0)),
        out_shape=jax.ShapeDtypeStruct((FNET_GROUPS, 2 * GROUP_DIM, GROUP_DIM), BF16),
        compiler_params=_params(("arbitrary",)),
        name="fnet_channel_weights",
    )(_channel_dft(), w_fnet)


def _rms(x):
    return x * lax.rsqrt(jnp.mean(x * x, axis=-1, keepdims=True) + EPS)


def _kv_outer(k, vz):
    return lax.dot_general(k, vz, (((0,), (0,)), ((), ())), preferred_element_type=F32)


def _inproj_kernel(x_ref, mod_ref, gain_ref, w_ref, rbase_ref, rlocal_ref, zeta_ref,
                   u_ref, q_ref, k_ref, v_ref, g_ref, sb_ref, state_ref, *, n_chunks, dec_b):
    @pl.when(pl.program_id(1) == 0)
    def _():
        state_ref[...] = jnp.zeros_like(state_ref)

    x = x_ref[0]
    a = gain_ref[...] * (1.0 + mod_ref[0, 1:2, :])
    h = (_rms(x) * a + mod_ref[0, 0:1, :]).astype(BF16)
    proj = jnp.dot(h, w_ref[...], preferred_element_type=F32)
    u_ref[0] = proj[:, :FNET_WIDTH].astype(BF16)
    cos_l, sin_l = rlocal_ref[0], rlocal_ref[1]
    cos = rbase_ref[0, 0:1, :] * cos_l - rbase_ref[0, 1:2, :] * sin_l
    sin = rbase_ref[0, 3:4, :] * cos_l + rbase_ref[0, 2:3, :] * sin_l
    for hd in range(RET_HEADS):
        lo = hd * HEAD_DIM
        for off, ref in ((FNET_WIDTH, q_ref), (FNET_WIDTH + RET_WIDTH, k_ref)):
            t = proj[:, off + lo:off + lo + HEAD_DIM]
            ref[0, :, lo:lo + HEAD_DIM] = (t * cos + pltpu.roll(t, HEAD_DIM // 2, 1) * sin).astype(BF16)
    v_ref[0] = proj[:, FNET_WIDTH + 2 * RET_WIDTH:FNET_WIDTH + 3 * RET_WIDTH].astype(BF16)
    g_ref[0] = proj[:, FNET_WIDTH + 3 * RET_WIDTH:].astype(BF16)

    for c in reversed(range(n_chunks)):
        rows = pl.ds(c * CHUNK, CHUNK)
        for hd in range(RET_HEADS):
            lanes = slice(hd * HEAD_DIM, (hd + 1) * HEAD_DIM)
            sb_ref[0, c, hd] = state_ref[hd].astype(BF16)
            vz = (v_ref[0, rows, lanes].astype(F32) * zeta_ref[hd]).astype(BF16)
            state_ref[hd] = state_ref[hd] * dec_b[hd] + _kv_outer(k_ref[0, rows, lanes], vz)


def _inproj(x, mod, gain, w_in, zeta_b, dec_b):
    b, s, _ = x.shape
    tm = min(TM_INPROJ, s)
    nt = s // tm
    nc = tm // CHUNK
    rot_base, rot_local = _rotary_tables(s, tm)
    rev = lambda bi, ti: (bi, nt - 1 - ti, 0)
    tok = lambda width: pl.BlockSpec((1, tm, width), rev)
    const = lambda shape: pl.BlockSpec(shape, lambda bi, ti: (0,) * len(shape))
    act = jax.ShapeDtypeStruct((b, s, RET_WIDTH), BF16)
    return pl.pallas_call(
        functools.partial(_inproj_kernel, n_chunks=nc, dec_b=dec_b),
        grid=(b, nt),
        in_specs=[tok(D_MODEL),
                  pl.BlockSpec((1, N_MOD, D_MODEL), lambda bi, ti: (bi, 0, 0)),
                  const((1, D_MODEL)),
                  const((D_MODEL, IN_WIDTH)),
                  pl.BlockSpec((1, 4, HEAD_DIM), lambda bi, ti: (nt - 1 - ti, 0, 0)),
                  const((2, tm, HEAD_DIM)),
                  const((RET_HEADS, CHUNK, HEAD_DIM))],
        out_specs=[tok(FNET_WIDTH), tok(RET_WIDTH), tok(RET_WIDTH), tok(RET_WIDTH), tok(RET_WIDTH),
                   pl.BlockSpec((1, nc, RET_HEADS, HEAD_DIM, HEAD_DIM),
                                lambda bi, ti: (bi, nt - 1 - ti, 0, 0, 0))],
        out_shape=[jax.ShapeDtypeStruct((b, s, FNET_WIDTH), BF16), act, act, act, act,
                   jax.ShapeDtypeStruct((b, s // CHUNK, RET_HEADS, HEAD_DIM, HEAD_DIM), BF16)],
        scratch_shapes=[pltpu.VMEM((RET_HEADS, HEAD_DIM, HEAD_DIM), F32)],
        compiler_params=_params(("parallel", "arbitrary")),
        name="inproj",
    )(x, mod, gain, w_in, rot_base, rot_local, zeta_b)


def _group_lanes():
    return [(g, slice(g * GROUP_DIM, (g + 1) * GROUP_DIM)) for g in range(FNET_GROUPS)]


def _unpack_rows(load, dst_ref, slab=0):
    units = dst_ref.shape[2] // BF16_ROWS
    half = units * F32_ROWS
    for g, lanes in _group_lanes():
        rows = load(lanes).reshape(units, BF16_ROWS, GROUP_DIM).astype(F32)
        dst_ref[slab, g, 0:half, :] = rows[:, :F32_ROWS, :].reshape(half, GROUP_DIM)
        dst_ref[slab, g, half:, :] = rows[:, F32_ROWS:, :].reshape(half, GROUP_DIM)


def _strided_rows(src_ref, row, slab=0):
    units = src_ref.shape[2] // BF16_ROWS
    start = (row // F32_ROWS) * (units * F32_ROWS) + row % F32_ROWS
    rows = pl.ds(start, units, stride=F32_ROWS)
    return jnp.concatenate([src_ref[slab, g, rows, :] for g in range(FNET_GROUPS)], axis=1).astype(BF16)


def _fnet_a_kernel(x_ref, fa_ref, y_ref, xs_ref, *, n1, tb):
    _unpack_rows(lambda lanes: x_ref[0, :, :, lanes], xs_ref)
    for j in range(tb):
        yj = jnp.dot(fa_ref[...], _strided_rows(xs_ref, j), preferred_element_type=F32)
        y_ref[0, j] = yj.astype(BF16).reshape(2, n1, FNET_WIDTH)


def _fnet_b_kernel(y_ref, p_ref, q_ref, cb_ref, sb_ref, m_ref, o_ref, ys_ref, *, slabs, reps, n2):
    half = MXU_DIM // 2
    blocks = BF16_ROWS // reps
    for sl in range(slabs):
        _unpack_rows(lambda lanes: y_ref[0, :, :, sl, :, lanes], ys_ref, sl)
    for sl in range(slabs):
        for j in range(blocks):
            blk = sl * blocks + j
            t = (p_ref[...] * cb_ref[blk] + q_ref[...] * sb_ref[blk]).astype(BF16)
            pieces = [_strided_rows(ys_ref, j * reps + r, sl) for r in range(reps)]
            yb = pieces[0] if reps == 1 else jnp.concatenate(pieces, axis=0)
            for pair in range(FNET_WIDTH // MXU_DIM):
                z = jnp.dot(t, yb[:, pair * MXU_DIM:(pair + 1) * MXU_DIM], preferred_element_type=F32)
                for sub in range(MXU_DIM // GROUP_DIM):
                    g = pair * (MXU_DIM // GROUP_DIM) + sub
                    zl = slice(sub * GROUP_DIM, (sub + 1) * GROUP_DIM)
                    zz = jnp.concatenate([z[:half, zl], z[half:, zl]], axis=1).astype(BF16)
                    res = jnp.dot(zz, m_ref[g], preferred_element_type=F32)
                    for r in range(reps):
                        o_ref[0, blk * reps + r, :, g * GROUP_DIM:(g + 1) * GROUP_DIM] = (
                            res[r * n2:(r + 1) * n2].astype(BF16))


def _fnet(u, m_chan):
    b, s, _ = u.shape
    n1 = DFT_N1
    n2 = s // n1
    assert s == n1 * n2 and MXU_DIM % (2 * n2) == 0 and n2 % BF16_ROWS == 0, s
    reps = MXU_DIM // (2 * n2)
    tb = BF16_ROWS
    y = pl.pallas_call(
        functools.partial(_fnet_a_kernel, n1=n1, tb=tb),
        grid=(b, n2 // tb),
        in_specs=[pl.BlockSpec((1, n1, tb, FNET_WIDTH), lambda bi, ni: (bi, 0, ni, 0)),
                  pl.BlockSpec((2 * n1, n1), lambda bi, ni: (0, 0))],
        out_specs=pl.BlockSpec((1, tb, 2, n1, FNET_WIDTH), lambda bi, ni: (bi, ni, 0, 0, 0)),
        out_shape=jax.ShapeDtypeStruct((b, n2, 2, n1, FNET_WIDTH), BF16),
        scratch_shapes=[pltpu.VMEM((1, FNET_GROUPS, n1 * BF16_ROWS, GROUP_DIM), F32)],
        compiler_params=_params(("parallel", "parallel")),
        name="fnet_stage_a",
    )(u.reshape(b, n1, n2, FNET_WIDTH), _dft_stage_a(n1))

    slabs = STAGE_B_MXU_BLOCKS * reps // BF16_ROWS
    k1_step = slabs * BF16_ROWS
    p_mat, q_mat, cos_b, sin_b = _dft_stage_b(n1, n2)
    square = pl.BlockSpec((MXU_DIM, MXU_DIM), lambda ki, bi: (0, 0))
    phase = pl.BlockSpec((STAGE_B_MXU_BLOCKS, 1, MXU_DIM), lambda ki, bi: (ki, 0, 0))
    return pl.pallas_call(
        functools.partial(_fnet_b_kernel, slabs=slabs, reps=reps, n2=n2),
        grid=(n1 // k1_step, b),
        in_specs=[pl.BlockSpec((1, n2, 2, slabs, BF16_ROWS, FNET_WIDTH), lambda ki, bi: (bi, 0, 0, ki, 0, 0)),
                  square, square, phase, phase,
                  pl.BlockSpec((FNET_GROUPS, 2 * GROUP_DIM, GROUP_DIM), lambda ki, bi: (0, 0, 0))],
        out_specs=pl.BlockSpec((1, k1_step, n2, FNET_WIDTH), lambda ki, bi: (bi, ki, 0, 0)),
        out_shape=jax.ShapeDtypeStruct((b, n1, n2, FNET_WIDTH), BF16),
        scratch_shapes=[pltpu.VMEM((slabs, FNET_GROUPS, n2 * 2 * BF16_ROWS, GROUP_DIM), F32)],
        compiler_params=_params(("parallel", "parallel")),
        name="fnet_stage_b",
    )(y.reshape(b, n2, 2, n1 // BF16_ROWS, BF16_ROWS, FNET_WIDTH), p_mat, q_mat, cos_b, sin_b, m_chan)


def _mixer_kernel(x_ref, mod_ref, f_ref, q_ref, k_ref, v_ref, g_ref, sb_ref, w_ref,
                  intra_ref, xif_ref, xib_ref, zeta_ref, o_ref, state_ref, r_ref, fs_ref, *, n_chunks, dec_f):
    @pl.when(pl.program_id(1) == 0)
    def _():
        state_ref[...] = jnp.zeros_like(state_ref)

    tiles = BF16_ROWS // n_chunks
    sub = pl.program_id(1) % tiles

    @pl.when(sub == 0)
    def _():
        _unpack_rows(lambda lanes: f_ref[0, :, :, lanes], fs_ref)

    f_rows = [_strided_rows(fs_ref, sub * n_chunks + c) for c in range(n_chunks)]
    mix = jnp.dot(jnp.concatenate(f_rows, axis=0), w_ref[:FNET_WIDTH, :], preferred_element_type=F32)

    for c in range(n_chunks):
        rows = pl.ds(c * CHUNK, CHUNK)
        for hd in range(RET_HEADS):
            lanes = slice(hd * HEAD_DIM, (hd + 1) * HEAD_DIM)
            q = q_ref[0, rows, lanes]
            k = k_ref[0, rows, lanes]
            v = v_ref[0, rows, lanes]
            qf = q.astype(F32)
            scores = lax.dot_general(q, k, (((1,), (1,)), ((), ())), preferred_element_type=F32)
            r = jnp.dot((scores * intra_ref[hd]).astype(BF16), v, preferred_element_type=F32)
            r += jnp.dot((qf * xif_ref[hd]).astype(BF16), state_ref[hd].astype(BF16),
                         preferred_element_type=F32)
            r += jnp.dot((qf * xib_ref[hd]).astype(BF16), sb_ref[0, c, hd], preferred_element_type=F32)
            vz = (v.astype(F32) * zeta_ref[hd]).astype(BF16)
            state_ref[hd] = state_ref[hd] * dec_f[hd] + _kv_outer(k, vz)
            gate = g_ref[0, rows, lanes].astype(F32)
            r_ref[rows, lanes] = (_rms(r) * (gate * jax.nn.sigmoid(gate))).astype(BF16)

    mix += jnp.dot(r_ref[...], w_ref[FNET_WIDTH:, :], preferred_element_type=F32)
    o_ref[0] = x_ref[0] + mod_ref[0, 2:3, :] * mix


def _mixer(x, mod, f, q, k, v, g, sb, w_out, tables):
    intra, xi_f, xi_b, zeta_f, dec_f = tables
    b, s, _ = x.shape
    tm = min(TM_MIXER, s)
    nc = tm // CHUNK
    assert f.shape[1] == CHUNK and BF16_ROWS % nc == 0 and (s // tm) % (BF16_ROWS // nc) == 0
    tiles = BF16_ROWS // nc
    tok = lambda width: pl.BlockSpec((1, tm, width), lambda bi, ti: (bi, ti, 0))
    const = lambda shape: pl.BlockSpec(shape, lambda bi, ti: (0,) * len(shape))
    tab = const((RET_HEADS, CHUNK, HEAD_DIM))
    return pl.pallas_call(
        functools.partial(_mixer_kernel, n_chunks=nc, dec_f=dec_f),
        grid=(b, s // tm),
        in_specs=[tok(D_MODEL),
                  pl.BlockSpec((1, N_MOD, D_MODEL), lambda bi, ti: (bi, 0, 0)),
                  pl.BlockSpec((1, CHUNK, BF16_ROWS, FNET_WIDTH), lambda bi, ti: (bi, 0, ti // tiles, 0)),
                  tok(RET_WIDTH), tok(RET_WIDTH), tok(RET_WIDTH), tok(RET_WIDTH),
                  pl.BlockSpec((1, nc, RET_HEADS, HEAD_DIM, HEAD_DIM), lambda bi, ti: (bi, ti, 0, 0, 0)),
                  const((D_MODEL, D_MODEL)), tab, tab, tab, tab],
        out_specs=tok(D_MODEL),
        out_shape=jax.ShapeDtypeStruct((b, s, D_MODEL), F32),
        scratch_shapes=[pltpu.VMEM((RET_HEADS, HEAD_DIM, HEAD_DIM), F32),
                        pltpu.VMEM((tm, RET_WIDTH), BF16),
                        pltpu.VMEM((1, FNET_GROUPS, CHUNK * BF16_ROWS, GROUP_DIM), F32)],
        compiler_params=_params(("parallel", "arbitrary")),
        name="mixer",
    )(x, mod, f, q, k, v, g, sb, w_out, intra, xi_f, xi_b, zeta_f)


def _mlp_kernel(x_ref, mod_ref, gain_ref, w1_ref, w2_ref, gfin_ref, o_ref):
    x = x_ref[0]
    a = gain_ref[...] * (1.0 + mod_ref[0, 4:5, :])
    h = (_rms(x) * a + mod_ref[0, 3:4, :]).astype(BF16)
    mid = jnp.dot(h, w1_ref[...], preferred_element_type=F32)
    mid = jnp.square(jnp.maximum(mid, 0.0)).astype(BF16)
    y = x + mod_ref[0, 5:6, :] * jnp.dot(mid, w2_ref[...], preferred_element_type=F32)
    o_ref[0] = _rms(y) * gfin_ref[...]


def _mlp(x, mod, gain, w1, w2, gain_final):
    b, s, _ = x.shape
    tm = min(TM_MLP, s)
    tok = pl.BlockSpec((1, tm, D_MODEL), lambda bi, ti: (bi, ti, 0))
    const = lambda shape: pl.BlockSpec(shape, lambda bi, ti: (0,) * len(shape))
    return pl.pallas_call(
        _mlp_kernel,
        grid=(b, s // tm),
        in_specs=[tok, pl.BlockSpec((1, N_MOD, D_MODEL), lambda bi, ti: (bi, 0, 0)),
                  const((1, D_MODEL)), const((D_MODEL, D_FF)), const((D_FF, D_MODEL)), const((1, D_MODEL))],
        out_specs=tok,
        out_shape=jax.ShapeDtypeStruct((b, s, D_MODEL), F32),
        compiler_params=_params(("parallel", "parallel")),
        name="mlp",
    )(x, mod, gain, w1, w2, gain_final)


def _trunk(x, mod, weights, tables):
    gain_mix, w_in, m_chan, w_out, gain_mlp, w1, w2, gain_final = weights
    intra, xi_f, xi_b, zeta_f, zeta_b, dec_f, dec_b = tables
    u, q, k, v, g, sb = _inproj(x, mod, gain_mix, w_in, zeta_b, dec_b)
    f = _fnet(u, m_chan)
    x1 = _mixer(x, mod, f, q, k, v, g, sb, w_out, (intra, xi_f, xi_b, zeta_f, dec_f))
    return _mlp(x1, mod, gain_mlp, w1, w2, gain_final)


def kernel(x_prompt, x_sample, c_prompt, c_sample, ada_w, ada_b, norm_mix, w_in, w_fnet, w_out,
           norm_mlp, w_mlp_in, w_mlp_out, norm_final):
    assert ada_w.shape[0] == 1, "one encoder layer"
    bp, bs = x_prompt.shape[0], x_sample.shape[0]
    c_all = jnp.concatenate([c_prompt, c_sample], axis=0)
    mod = _modulation(c_all, ada_w[0], ada_b[0]).reshape(bp + bs, N_MOD, D_MODEL)
    weights = (norm_mix[0].reshape(1, -1), w_in[0].astype(BF16), _fnet_channel_weights(w_fnet[0]),
               w_out[0].astype(BF16), norm_mlp[0].reshape(1, -1), w_mlp_in[0].astype(BF16),
               w_mlp_out[0].astype(BF16), norm_final.reshape(1, -1))
    tables = _retention_tables()
    y_prompt = _trunk(x_prompt, mod[:bp], weights, tables)
    y_sample = _trunk(x_sample, mod[bp:], weights, tables)
    return (y_prompt, y_sample)
```

```python
import functools

import numpy as np
import jax
import jax.numpy as jnp
from jax import lax
from jax.experimental import pallas as pl
from jax.experimental.pallas import tpu as pltpu

F32 = jnp.float32
BF16 = jnp.bfloat16

D_MODEL = 1024
FNET_WIDTH = 512
FNET_GROUPS = 4
GROUP_DIM = 128
RET_WIDTH = 512
RET_HEADS = 4
HEAD_DIM = 128
D_FF = 4096
CHUNK = 128
ROPE_BASE = 10000.0
EPS = 1e-6
N_MOD = 6
IN_WIDTH = FNET_WIDTH + 4 * RET_WIDTH
DECAY_OFFSET_FWD = 0.0
DECAY_OFFSET_BWD = 0.5

DFT_N1 = 128
MXU_DIM = 256
BF16_ROWS = 16
F32_ROWS = 8
STAGE_B_MXU_BLOCKS = 16
VMEM_LIMIT = 56 * 1024 * 1024

TM_INPROJ = 1024
TM_MIXER = 1024
TM_MLP = 1024
MLP_FF_PARTS = 4


def _params(semantics):
    return pltpu.CompilerParams(dimension_semantics=semantics, vmem_limit_bytes=VMEM_LIMIT)


def _retention_tables():
    scale = HEAD_DIM ** -0.5
    h = np.arange(RET_HEADS, dtype=np.float64)
    lg_f = np.log1p(-np.exp2(-5.0 - DECAY_OFFSET_FWD - h))[:, None, None]
    lg_b = np.log1p(-np.exp2(-5.0 - DECAY_OFFSET_BWD - h))[:, None, None]
    i = np.arange(CHUNK, dtype=np.float64)[None, :, None]
    j = np.arange(CHUNK, dtype=np.float64)[None, None, :]
    ones = np.ones((1, 1, HEAD_DIM))
    intra = scale * np.where(i >= j, np.exp(lg_f * np.maximum(i - j, 0.0)), np.exp(lg_b * np.maximum(j - i, 0.0)))
    xi_f = np.exp(lg_f * (i + 1.0)) * ones
    xi_b = np.exp(lg_b * (CHUNK - i)) * ones
    zeta_f = scale * np.exp(lg_f * (CHUNK - 1.0 - i)) * ones
    zeta_b = scale * np.exp(lg_b * i) * ones
    dec_f = tuple(float(v) for v in np.exp(lg_f[:, 0, 0] * CHUNK))
    dec_b = tuple(float(v) for v in np.exp(lg_b[:, 0, 0] * CHUNK))
    f = lambda a: jnp.asarray(a.astype(np.float32))
    return f(intra), f(xi_f), f(xi_b), f(zeta_f), f(zeta_b), dec_f, dec_b


def _f32(a):
    return jnp.asarray(np.asarray(a).astype(np.float32))


def _rotary_tables(seq, tm):
    half = HEAD_DIM // 2
    inv = ROPE_BASE ** (-np.arange(half, dtype=np.float64) / half)
    inv = np.concatenate([inv, inv])
    sign = np.concatenate([-np.ones(half), np.ones(half)])
    base = np.arange(0, seq, tm, dtype=np.float64)[:, None] * inv[None, :]
    local = np.arange(tm, dtype=np.float64)[:, None] * inv[None, :]
    base_rows = np.stack([np.cos(base), np.sin(base), sign * np.cos(base), sign * np.sin(base)], axis=1)
    return _f32(base_rows), _f32(np.stack([np.cos(local), np.sin(local)], axis=0))


def _dft_stage_a(n1):
    k = np.arange(n1, dtype=np.int64)
    ang = 2.0 * np.pi * ((k[:, None] * k[None, :]) % n1) / n1
    return _f32(np.concatenate([np.cos(ang), -np.sin(ang)], axis=0) * (n1 ** -0.5)).astype(BF16)


def _dft_stage_b(n1, n2):
    reps = MXU_DIM // (2 * n2)
    seq = n1 * n2
    idx = np.arange(MXU_DIM, dtype=np.int64)
    row_c, row_r, row_k = idx // (reps * n2), (idx // n2) % reps, idx % n2
    col_r, col_s, col_c = idx // (2 * n2), (idx // 2) % n2, idx % 2
    alpha = 2.0 * np.pi * ((row_k[:, None] * col_s[None, :]) % n2) / n2
    ca, sa = np.cos(alpha), np.sin(alpha)
    c_out, c_in = row_c[:, None], col_c[None, :]
    p_mat = np.where(c_out == c_in, ca, np.where(c_out < c_in, sa, -sa))
    q_mat = np.where(c_out == c_in, -sa, np.where(c_out < c_in, ca, -ca))
    keep = (row_r[:, None] == col_r[None, :]) * ((n2 * GROUP_DIM) ** -0.5)
    blk = np.arange(n1 // reps, dtype=np.int64)[:, None]
    beta = 2.0 * np.pi * ((col_s[None, :] * (blk * reps + col_r[None, :])) % seq) / seq
    return _f32(p_mat * keep), _f32(q_mat * keep), _f32(np.cos(beta)[:, None, :]), _f32(np.sin(beta)[:, None, :])


def _channel_dft():
    c = np.arange(GROUP_DIM, dtype=np.int64)
    ang = 2.0 * np.pi * ((c[:, None] * c[None, :]) % GROUP_DIM) / GROUP_DIM
    return jnp.asarray(np.concatenate([np.cos(ang), np.sin(ang)], axis=0).astype(np.float32))


def _mod_kernel(c_ref, w_ref, b_ref, o_ref):
    c = c_ref[...]
    s = c * jax.nn.sigmoid(c)
    o_ref[...] = jnp.dot(s, w_ref[...], preferred_element_type=F32,
                         precision=lax.Precision.HIGHEST) + b_ref[...]


def _modulation(c, ada_w, ada_b):
    rows = c.shape[0]
    tn = 1536
    return pl.pallas_call(
        _mod_kernel,
        grid=(N_MOD * D_MODEL // tn,),
        in_specs=[pl.BlockSpec((rows, D_MODEL), lambda n: (0, 0)),
                  pl.BlockSpec((D_MODEL, tn), lambda n: (0, n)),
                  pl.BlockSpec((1, tn), lambda n: (0, n))],
        out_specs=pl.BlockSpec((rows, tn), lambda n: (0, n)),
        out_shape=jax.ShapeDtypeStruct((rows, N_MOD * D_MODEL), F32),
        compiler_params=_params(("arbitrary",)),
        name="adaln_mod",
    )(c, ada_w, ada_b.reshape(1, -1))


def _fnet_weight_kernel(cs_ref, w_ref, o_ref):
    o_ref[0] = jnp.dot(cs_ref[...], w_ref[0], preferred_element_type=F32,
                       precision=lax.Precision.HIGHEST).astype(BF16)


def _fnet_channel_weights(w_fnet):
    return pl.pallas_call(
        _fnet_weight_kernel,
        grid=(FNET_GROUPS,),
        in_specs=[pl.BlockSpec((2 * GROUP_DIM, GROUP_DIM), lambda g: (0, 0)),
                  pl.BlockSpec((1, GROUP_DIM, GROUP_DIM), lambda g: (g, 0, 0))],
        out_specs=pl.BlockSpec((1, 2 * GROUP_DIM, GROUP_DIM), lambda g: (g, 0, 0)),
        out_shape=jax.ShapeDtypeStruct((FNET_GROUPS, 2 * GROUP_DIM, GROUP_DIM), BF16),
        compiler_params=_params(("arbitrary",)),
        name="fnet_channel_weights",
    )(_channel_dft(), w_fnet)


def _rms(x):
    return x * lax.rsqrt(jnp.mean(x * x, axis=-1, keepdims=True) + EPS)


def _kv_outer(k, vz):
    return lax.dot_general(k, vz, (((0,), (0,)), ((), ())), preferred_element_type=F32)


def _inproj_kernel(x_ref, mod_ref, gain_ref, w_ref, rbase_ref, rlocal_ref, zeta_ref,
                   u_ref, q_ref, k_ref, v_ref, g_ref, sb_ref, state_ref, *, n_chunks, dec_b):
    @pl.when(pl.program_id(1) == 0)
    def _():
        state_ref[...] = jnp.zeros_like(state_ref)

    x = x_ref[0]
    a = gain_ref[...] * (1.0 + mod_ref[0, 1:2, :])
    h = (_rms(x) * a + mod_ref[0, 0:1, :]).astype(BF16)
    proj = jnp.dot(h, w_ref[...], preferred_element_type=F32)
    u_ref[0] = proj[:, :FNET_WIDTH].astype(BF16)
    cos_l, sin_l = rlocal_ref[0], rlocal_ref[1]
    cos = rbase_ref[0, 0:1, :] * cos_l - rbase_ref[0, 1:2, :] * sin_l
    sin = rbase_ref[0, 3:4, :] * cos_l + rbase_ref[0, 2:3, :] * sin_l
    for hd in range(RET_HEADS):
        lo = hd * HEAD_DIM
        for off, ref in ((FNET_WIDTH, q_ref), (FNET_WIDTH + RET_WIDTH, k_ref)):
            t = proj[:, off + lo:off + lo + HEAD_DIM]
            ref[0, :, lo:lo + HEAD_DIM] = (t * cos + pltpu.roll(t, HEAD_DIM // 2, 1) * sin).astype(BF16)
    v_ref[0] = proj[:, FNET_WIDTH + 2 * RET_WIDTH:FNET_WIDTH + 3 * RET_WIDTH].astype(BF16)
    g_ref[0] = proj[:, FNET_WIDTH + 3 * RET_WIDTH:].astype(BF16)

    for c in reversed(range(n_chunks)):
        rows = pl.ds(c * CHUNK, CHUNK)
        for hd in range(RET_HEADS):
            lanes = slice(hd * HEAD_DIM, (hd + 1) * HEAD_DIM)
            sb_ref[0, c, hd] = state_ref[hd].astype(BF16)
            vz = (v_ref[0, rows, lanes].astype(F32) * zeta_ref[hd]).astype(BF16)
            state_ref[hd] = state_ref[hd] * dec_b[hd] + _kv_outer(k_ref[0, rows, lanes], vz)


def _inproj(x, mod, gain, w_in, zeta_b, dec_b):
    b, s, _ = x.shape
    tm = min(TM_INPROJ, s)
    nt = s // tm
    nc = tm // CHUNK
    rot_base, rot_local = _rotary_tables(s, tm)
    rev = lambda bi, ti: (bi, nt - 1 - ti, 0)
    tok = lambda width: pl.BlockSpec((1, tm, width), rev)
    const = lambda shape: pl.BlockSpec(shape, lambda bi, ti: (0,) * len(shape))
    act = jax.ShapeDtypeStruct((b, s, RET_WIDTH), BF16)
    states = pl.BlockSpec((1, nc, RET_HEADS, HEAD_DIM, HEAD_DIM), lambda bi, ti: (bi, nt - 1 - ti, 0, 0, 0))
    state_shape = jax.ShapeDtypeStruct((b, s // CHUNK, RET_HEADS, HEAD_DIM, HEAD_DIM), BF16)
    return pl.pallas_call(
        functools.partial(_inproj_kernel, n_chunks=nc, dec_b=dec_b),
        grid=(b, nt),
        in_specs=[tok(D_MODEL),
                  pl.BlockSpec((1, N_MOD, D_MODEL), lambda bi, ti: (bi, 0, 0)),
                  const((1, D_MODEL)),
                  const((D_MODEL, IN_WIDTH)),
                  pl.BlockSpec((1, 4, HEAD_DIM), lambda bi, ti: (nt - 1 - ti, 0, 0)),
                  const((2, tm, HEAD_DIM)),
                  const((RET_HEADS, CHUNK, HEAD_DIM))],
        out_specs=[tok(FNET_WIDTH), tok(RET_WIDTH), tok(RET_WIDTH), tok(RET_WIDTH), tok(RET_WIDTH),
                   states],
        out_shape=[jax.ShapeDtypeStruct((b, s, FNET_WIDTH), BF16), act, act, act, act,
                   state_shape],
        scratch_shapes=[pltpu.VMEM((RET_HEADS, HEAD_DIM, HEAD_DIM), F32)],
        compiler_params=_params(("parallel", "arbitrary")),
        name="inproj",
    )(x, mod, gain, w_in, rot_base, rot_local, zeta_b)


def _group_lanes():
    return [(g, slice(g * GROUP_DIM, (g + 1) * GROUP_DIM)) for g in range(FNET_GROUPS)]


def _unpack_rows(load, dst_ref, slab=0):
    units = dst_ref.shape[2] // BF16_ROWS
    half = units * F32_ROWS
    for g, lanes in _group_lanes():
        rows = load(lanes).reshape(units, BF16_ROWS, GROUP_DIM).astype(F32)
        dst_ref[slab, g, 0:half, :] = rows[:, :F32_ROWS, :].reshape(half, GROUP_DIM)
        dst_ref[slab, g, half:, :] = rows[:, F32_ROWS:, :].reshape(half, GROUP_DIM)


def _strided_rows(src_ref, row, slab=0):
    units = src_ref.shape[2] // BF16_ROWS
    start = (row // F32_ROWS) * (units * F32_ROWS) + row % F32_ROWS
    rows = pl.ds(start, units, stride=F32_ROWS)
    return jnp.concatenate([src_ref[slab, g, rows, :] for g in range(FNET_GROUPS)], axis=1).astype(BF16)


def _fnet_a_kernel(x_ref, fa_ref, y_ref, xs_ref, *, n1, tb):
    _unpack_rows(lambda lanes: x_ref[0, :, :, lanes], xs_ref)
    for j in range(tb):
        yj = jnp.dot(fa_ref[...], _strided_rows(xs_ref, j), preferred_element_type=F32)
        y_ref[0, j] = yj.astype(BF16).reshape(2, n1, FNET_WIDTH)


def _fnet_b_kernel(y_ref, p_ref, q_ref, cb_ref, sb_ref, m_ref, o_ref, ys_ref, *, slabs, reps, n2):
    half = MXU_DIM // 2
    blocks = BF16_ROWS // reps
    for sl in range(slabs):
        _unpack_rows(lambda lanes: y_ref[0, :, :, sl, :, lanes], ys_ref, sl)
    for sl in range(slabs):
        for j in range(blocks):
            blk = sl * blocks + j
            t = (p_ref[...] * cb_ref[blk] + q_ref[...] * sb_ref[blk]).astype(BF16)
            pieces = [_strided_rows(ys_ref, j * reps + r, sl) for r in range(reps)]
            yb = pieces[0] if reps == 1 else jnp.concatenate(pieces, axis=0)
            for pair in range(FNET_WIDTH // MXU_DIM):
                z = jnp.dot(t, yb[:, pair * MXU_DIM:(pair + 1) * MXU_DIM], preferred_element_type=F32)
                for sub in range(MXU_DIM // GROUP_DIM):
                    g = pair * (MXU_DIM // GROUP_DIM) + sub
                    zl = slice(sub * GROUP_DIM, (sub + 1) * GROUP_DIM)
                    zz = jnp.concatenate([z[:half, zl], z[half:, zl]], axis=1).astype(BF16)
                    res = jnp.dot(zz, m_ref[g], preferred_element_type=F32)
                    for r in range(reps):
                        o_ref[0, blk * reps + r, :, g * GROUP_DIM:(g + 1) * GROUP_DIM] = (
                            res[r * n2:(r + 1) * n2].astype(BF16))


def _fnet(u, m_chan):
    b, s, _ = u.shape
    n1 = DFT_N1
    n2 = s // n1
    assert s == n1 * n2 and MXU_DIM % (2 * n2) == 0 and n2 % BF16_ROWS == 0, s
    reps = MXU_DIM // (2 * n2)
    tb = BF16_ROWS
    y = pl.pallas_call(
        functools.partial(_fnet_a_kernel, n1=n1, tb=tb),
        grid=(b, n2 // tb),
        in_specs=[pl.BlockSpec((1, n1, tb, FNET_WIDTH), lambda bi, ni: (bi, 0, ni, 0)),
                  pl.BlockSpec((2 * n1, n1), lambda bi, ni: (0, 0))],
        out_specs=pl.BlockSpec((1, tb, 2, n1, FNET_WIDTH), lambda bi, ni: (bi, ni, 0, 0, 0)),
        out_shape=jax.ShapeDtypeStruct((b, n2, 2, n1, FNET_WIDTH), BF16),
        scratch_shapes=[pltpu.VMEM((1, FNET_GROUPS, n1 * BF16_ROWS, GROUP_DIM), F32)],
        compiler_params=_params(("parallel", "parallel")),
        name="fnet_stage_a",
    )(u.reshape(b, n1, n2, FNET_WIDTH), _dft_stage_a(n1))

    slabs = STAGE_B_MXU_BLOCKS * reps // BF16_ROWS
    k1_step = slabs * BF16_ROWS
    p_mat, q_mat, cos_b, sin_b = _dft_stage_b(n1, n2)
    square = pl.BlockSpec((MXU_DIM, MXU_DIM), lambda ki, bi: (0, 0))
    phase = pl.BlockSpec((STAGE_B_MXU_BLOCKS, 1, MXU_DIM), lambda ki, bi: (ki, 0, 0))
    return pl.pallas_call(
        functools.partial(_fnet_b_kernel, slabs=slabs, reps=reps, n2=n2),
        grid=(n1 // k1_step, b),
        in_specs=[pl.BlockSpec((1, n2, 2, slabs, BF16_ROWS, FNET_WIDTH), lambda ki, bi: (bi, 0, 0, ki, 0, 0)),
                  square, square, phase, phase,
                  pl.BlockSpec((FNET_GROUPS, 2 * GROUP_DIM, GROUP_DIM), lambda ki, bi: (0, 0, 0))],
        out_specs=pl.BlockSpec((1, k1_step, n2, FNET_WIDTH), lambda ki, bi: (bi, ki, 0, 0)),
        out_shape=jax.ShapeDtypeStruct((b, n1, n2, FNET_WIDTH), BF16),
        scratch_shapes=[pltpu.VMEM((slabs, FNET_GROUPS, n2 * 2 * BF16_ROWS, GROUP_DIM), F32)],
        compiler_params=_params(("parallel", "parallel")),
        name="fnet_stage_b",
    )(y.reshape(b, n2, 2, n1 // BF16_ROWS, BF16_ROWS, FNET_WIDTH), p_mat, q_mat, cos_b, sin_b, m_chan)


def _mixer_kernel(x_ref, mod_ref, f_ref, q_ref, k_ref, v_ref, g_ref, sb_ref, w_ref,
                  intra_ref, xif_ref, xib_ref, zeta_ref, o_ref, state_ref, r_ref, fs_ref, *, n_chunks, dec_f):
    @pl.when(pl.program_id(1) == 0)
    def _():
        state_ref[...] = jnp.zeros_like(state_ref)

    tiles = BF16_ROWS // n_chunks
    sub = pl.program_id(1) % tiles

    @pl.when(sub == 0)
    def _():
        _unpack_rows(lambda lanes: f_ref[0, :, :, lanes], fs_ref)

    f_rows = [_strided_rows(fs_ref, sub * n_chunks + c) for c in range(n_chunks)]
    mix = jnp.dot(jnp.concatenate(f_rows, axis=0), w_ref[:FNET_WIDTH, :], preferred_element_type=F32)

    for c in range(n_chunks):
        rows = pl.ds(c * CHUNK, CHUNK)
        for hd in range(RET_HEADS):
            lanes = slice(hd * HEAD_DIM, (hd + 1) * HEAD_DIM)
            q = q_ref[0, rows, lanes]
            k = k_ref[0, rows, lanes]
            v = v_ref[0, rows, lanes]
            qf = q.astype(F32)
            scores = lax.dot_general(q, k, (((1,), (1,)), ((), ())), preferred_element_type=F32)
            r = jnp.dot((scores * intra_ref[hd]).astype(BF16), v, preferred_element_type=F32)
            r += jnp.dot((qf * xif_ref[hd]).astype(BF16), state_ref[hd].astype(BF16),
                         preferred_element_type=F32)
            r += jnp.dot((qf * xib_ref[hd]).astype(BF16), sb_ref[0, c, hd], preferred_element_type=F32)
            vz = (v.astype(F32) * zeta_ref[hd]).astype(BF16)
            state_ref[hd] = state_ref[hd] * dec_f[hd] + _kv_outer(k, vz)
            gate = g_ref[0, rows, lanes].astype(F32)
            r_ref[rows, lanes] = (_rms(r) * (gate * jax.nn.sigmoid(gate))).astype(BF16)

    mix += jnp.dot(r_ref[...], w_ref[FNET_WIDTH:, :], preferred_element_type=F32)
    o_ref[0] = x_ref[0] + mod_ref[0, 2:3, :] * mix


def _mixer(x, mod, f, q, k, v, g, sb, w_out, tables):
    intra, xi_f, xi_b, zeta_f, dec_f = tables
    b, s, _ = x.shape
    tm = min(TM_MIXER, s)
    nc = tm // CHUNK
    assert f.shape[1] == CHUNK and BF16_ROWS % nc == 0 and (s // tm) % (BF16_ROWS // nc) == 0
    tiles = BF16_ROWS // nc
    tok = lambda width: pl.BlockSpec((1, tm, width), lambda bi, ti: (bi, ti, 0))
    const = lambda shape: pl.BlockSpec(shape, lambda bi, ti: (0,) * len(shape))
    tab = const((RET_HEADS, CHUNK, HEAD_DIM))
    states = pl.BlockSpec((1, nc, RET_HEADS, HEAD_DIM, HEAD_DIM), lambda bi, ti: (bi, ti, 0, 0, 0))
    return pl.pallas_call(
        functools.partial(_mixer_kernel, n_chunks=nc, dec_f=dec_f),
        grid=(b, s // tm),
        in_specs=[tok(D_MODEL),
                  pl.BlockSpec((1, N_MOD, D_MODEL), lambda bi, ti: (bi, 0, 0)),
                  pl.BlockSpec((1, CHUNK, BF16_ROWS, FNET_WIDTH), lambda bi, ti: (bi, 0, ti // tiles, 0)),
                  tok(RET_WIDTH), tok(RET_WIDTH), tok(RET_WIDTH), tok(RET_WIDTH),
                  states,
                  const((D_MODEL, D_MODEL)), tab, tab, tab, tab],
        out_specs=tok(D_MODEL),
        out_shape=jax.ShapeDtypeStruct((b, s, D_MODEL), F32),
        scratch_shapes=[pltpu.VMEM((RET_HEADS, HEAD_DIM, HEAD_DIM), F32),
                        pltpu.VMEM((tm, RET_WIDTH), BF16),
                        pltpu.VMEM((1, FNET_GROUPS, CHUNK * BF16_ROWS, GROUP_DIM), F32)],
        compiler_params=_params(("parallel", "arbitrary")),
        name="mixer",
    )(x, mod, f, q, k, v, g, sb, w_out, intra, xi_f, xi_b, zeta_f)


def _mlp_kernel(x_ref, mod_ref, gain_ref, w1_ref, w2_ref, gfin_ref, o_ref):
    x = x_ref[0]
    a = gain_ref[...] * (1.0 + mod_ref[0, 4:5, :])
    h = (_rms(x) * a + mod_ref[0, 3:4, :]).astype(BF16)
    out = None
    for part in range(MLP_FF_PARTS):
        cols = slice(part * (D_FF // MLP_FF_PARTS), (part + 1) * (D_FF // MLP_FF_PARTS))
        mid = jnp.dot(h, w1_ref[:, cols], preferred_element_type=F32)
        mid = jnp.square(jnp.maximum(mid, 0.0)).astype(BF16)
        slab = jnp.dot(mid, w2_ref[cols, :], preferred_element_type=F32)
        out = slab if out is None else out + slab
    y = x + mod_ref[0, 5:6, :] * out
    o_ref[0] = _rms(y) * gfin_ref[...]


def _mlp(x, mod, gain, w1, w2, gain_final):
    b, s, _ = x.shape
    tm = min(TM_MLP, s)
    tok = pl.BlockSpec((1, tm, D_MODEL), lambda bi, ti: (bi, ti, 0))
    const = lambda shape: pl.BlockSpec(shape, lambda bi, ti: (0,) * len(shape))
    weight = lambda shape: pl.BlockSpec(shape, lambda bi, ti: (0,) * len(shape), pipeline_mode=pl.Buffered(1))
    return pl.pallas_call(
        _mlp_kernel,
        grid=(b, s // tm),
        in_specs=[tok, pl.BlockSpec((1, N_MOD, D_MODEL), lambda bi, ti: (bi, 0, 0)),
                  const((1, D_MODEL)), weight((D_MODEL, D_FF)), weight((D_FF, D_MODEL)), const((1, D_MODEL))],
        out_specs=tok,
        out_shape=jax.ShapeDtypeStruct((b, s, D_MODEL), F32),
        compiler_params=_params(("parallel", "parallel")),
        name="mlp",
    )(x, mod, gain, w1, w2, gain_final)


def _trunk(x, mod, weights, tables):
    gain_mix, w_in, m_chan, w_out, gain_mlp, w1, w2, gain_final = weights
    intra, xi_f, xi_b, zeta_f, zeta_b, dec_f, dec_b = tables
    u, q, k, v, g, sb = _inproj(x, mod, gain_mix, w_in, zeta_b, dec_b)
    f = _fnet(u, m_chan)
    x1 = _mixer(x, mod, f, q, k, v, g, sb, w_out, (intra, xi_f, xi_b, zeta_f, dec_f))
    return _mlp(x1, mod, gain_mlp, w1, w2, gain_final)


def kernel(x_prompt, x_sample, c_prompt, c_sample, ada_w, ada_b, norm_mix, w_in, w_fnet, w_out,
           norm_mlp, w_mlp_in, w_mlp_out, norm_final):
    assert ada_w.shape[0] == 1, "one encoder layer"
    bp, bs = x_prompt.shape[0], x_sample.shape[0]
    c_all = jnp.concatenate([c_prompt, c_sample], axis=0)
    mod = _modulation(c_all, ada_w[0], ada_b[0]).reshape(bp + bs, N_MOD, D_MODEL)
    weights = (norm_mix[0].reshape(1, -1), w_in[0].astype(BF16), _fnet_channel_weights(w_fnet[0]),
               w_out[0].astype(BF16), norm_mlp[0].reshape(1, -1), w_mlp_in[0].astype(BF16),
               w_mlp_out[0].astype(BF16), norm_final.reshape(1, -1))
    tables = _retention_tables()
    y_prompt = _trunk(x_prompt, mod[:bp], weights, tables)
    y_sample = _trunk(x_sample, mod[bp:], weights, tables)
    return (y_prompt, y_sample)
```

```python
import functools

import numpy as np
import jax
import jax.numpy as jnp
from jax import lax
from jax.experimental import pallas as pl
from jax.experimental.pallas import tpu as pltpu

F32 = jnp.float32
BF16 = jnp.bfloat16

D_MODEL = 1024
FNET_WIDTH = 512
FNET_GROUPS = 4
GROUP_DIM = 128
RET_WIDTH = 512
RET_HEADS = 4
HEAD_DIM = 128
D_FF = 4096
CHUNK = 128
ROPE_BASE = 10000.0
EPS = 1e-6
N_MOD = 6
IN_WIDTH = FNET_WIDTH + 4 * RET_WIDTH
DECAY_OFFSET_FWD = 0.0
DECAY_OFFSET_BWD = 0.5

DFT_N1 = 128
MXU_DIM = 256
BF16_ROWS = 16
F32_ROWS = 8
STAGE_A_SLABS = 2
STAGE_B_MXU_BLOCKS = 16
VMEM_LIMIT = 56 * 1024 * 1024

TM_INPROJ = 1024
TM_MIXER = 1024
TM_MLP = 1024
MLP_FF_PARTS = 4


def _params(semantics):
    return pltpu.CompilerParams(dimension_semantics=semantics, vmem_limit_bytes=VMEM_LIMIT)


def _retention_tables():
    scale = HEAD_DIM ** -0.5
    h = np.arange(RET_HEADS, dtype=np.float64)
    lg_f = np.log1p(-np.exp2(-5.0 - DECAY_OFFSET_FWD - h))[:, None, None]
    lg_b = np.log1p(-np.exp2(-5.0 - DECAY_OFFSET_BWD - h))[:, None, None]
    i = np.arange(CHUNK, dtype=np.float64)[None, :, None]
    j = np.arange(CHUNK, dtype=np.float64)[None, None, :]
    ones = np.ones((1, 1, HEAD_DIM))
    intra = scale * np.where(i >= j, np.exp(lg_f * np.maximum(i - j, 0.0)), np.exp(lg_b * np.maximum(j - i, 0.0)))
    xi_f = np.exp(lg_f * (i + 1.0)) * ones
    xi_b = np.exp(lg_b * (CHUNK - i)) * ones
    zeta_f = scale * np.exp(lg_f * (CHUNK - 1.0 - i)) * ones
    zeta_b = scale * np.exp(lg_b * i) * ones
    dec_f = tuple(float(v) for v in np.exp(lg_f[:, 0, 0] * CHUNK))
    dec_b = tuple(float(v) for v in np.exp(lg_b[:, 0, 0] * CHUNK))
    f = lambda a: jnp.asarray(a.astype(np.float32))
    return f(intra), f(xi_f), f(xi_b), f(zeta_f), f(zeta_b), dec_f, dec_b


def _f32(a):
    return jnp.asarray(np.asarray(a).astype(np.float32))


def _rotary_tables(seq, tm):
    half = HEAD_DIM // 2
    inv = ROPE_BASE ** (-np.arange(half, dtype=np.float64) / half)
    inv = np.concatenate([inv, inv])
    sign = np.concatenate([-np.ones(half), np.ones(half)])
    base = np.arange(0, seq, tm, dtype=np.float64)[:, None] * inv[None, :]
    local = np.arange(tm, dtype=np.float64)[:, None] * inv[None, :]
    base_rows = np.stack([np.cos(base), np.sin(base), sign * np.cos(base), sign * np.sin(base)], axis=1)
    return _f32(base_rows), _f32(np.stack([np.cos(local), np.sin(local)], axis=0))


def _dft_stage_a(n1):
    k = np.arange(n1, dtype=np.int64)
    ang = 2.0 * np.pi * ((k[:, None] * k[None, :]) % n1) / n1
    return _f32(np.concatenate([np.cos(ang), -np.sin(ang)], axis=0) * (n1 ** -0.5)).astype(BF16)


def _dft_stage_b(n1, n2):
    reps = MXU_DIM // (2 * n2)
    seq = n1 * n2
    idx = np.arange(MXU_DIM, dtype=np.int64)
    row_c, row_r, row_k = idx // (reps * n2), (idx // n2) % reps, idx % n2
    col_r, col_s, col_c = idx // (2 * n2), (idx // 2) % n2, idx % 2
    alpha = 2.0 * np.pi * ((row_k[:, None] * col_s[None, :]) % n2) / n2
    ca, sa = np.cos(alpha), np.sin(alpha)
    c_out, c_in = row_c[:, None], col_c[None, :]
    p_mat = np.where(c_out == c_in, ca, np.where(c_out < c_in, sa, -sa))
    q_mat = np.where(c_out == c_in, -sa, np.where(c_out < c_in, ca, -ca))
    keep = (row_r[:, None] == col_r[None, :]) * ((n2 * GROUP_DIM) ** -0.5)
    blk = np.arange(n1 // reps, dtype=np.int64)[:, None]
    beta = 2.0 * np.pi * ((col_s[None, :] * (blk * reps + col_r[None, :])) % seq) / seq
    return _f32(p_mat * keep), _f32(q_mat * keep), _f32(np.cos(beta)[:, None, :]), _f32(np.sin(beta)[:, None, :])


def _channel_dft():
    c = np.arange(GROUP_DIM, dtype=np.int64)
    ang = 2.0 * np.pi * ((c[:, None] * c[None, :]) % GROUP_DIM) / GROUP_DIM
    return jnp.asarray(np.concatenate([np.cos(ang), np.sin(ang)], axis=0).astype(np.float32))


def _mod_kernel(c_ref, w_ref, b_ref, o_ref):
    c = c_ref[...]
    s = c * jax.nn.sigmoid(c)
    w = w_ref[...]
    s_hi, w_hi = s.astype(BF16), w.astype(BF16)
    s_lo, w_lo = (s - s_hi.astype(F32)).astype(BF16), (w - w_hi.astype(F32)).astype(BF16)
    dot = functools.partial(jnp.dot, preferred_element_type=F32)
    o_ref[...] = dot(s_hi, w_hi) + (dot(s_hi, w_lo) + dot(s_lo, w_hi)) + b_ref[...]


def _modulation(c, ada_w, ada_b):
    rows = c.shape[0]
    tn = 1536
    return pl.pallas_call(
        _mod_kernel,
        grid=(N_MOD * D_MODEL // tn,),
        in_specs=[pl.BlockSpec((rows, D_MODEL), lambda n: (0, 0)),
                  pl.BlockSpec((D_MODEL, tn), lambda n: (0, n)),
                  pl.BlockSpec((1, tn), lambda n: (0, n))],
        out_specs=pl.BlockSpec((rows, tn), lambda n: (0, n)),
        out_shape=jax.ShapeDtypeStruct((rows, N_MOD * D_MODEL), F32),
        compiler_params=_params(("arbitrary",)),
        name="adaln_mod",
    )(c, ada_w, ada_b.reshape(1, -1))


def _fnet_weight_kernel(cs_ref, w_ref, o_ref):
    o_ref[0] = jnp.dot(cs_ref[...], w_ref[0], preferred_element_type=F32,
                       precision=lax.Precision.HIGHEST).astype(BF16)


def _fnet_channel_weights(w_fnet):
    return pl.pallas_call(
        _fnet_weight_kernel,
        grid=(FNET_GROUPS,),
        in_specs=[pl.BlockSpec((2 * GROUP_DIM, GROUP_DIM), lambda g: (0, 0)),
                  pl.BlockSpec((1, GROUP_DIM, GROUP_DIM), lambda g: (g, 0, 0))],
        out_specs=pl.BlockSpec((1, 2 * GROUP_DIM, GROUP_DIM), lambda g: (g, 0, 0)),
        out_shape=jax.ShapeDtypeStruct((FNET_GROUPS, 2 * GROUP_DIM, GROUP_DIM), BF16),
        compiler_params=_params(("arbitrary",)),
        name="fnet_channel_weights",
    )(_channel_dft(), w_fnet)


def _rms(x):
    return x * lax.rsqrt(jnp.mean(x * x, axis=-1, keepdims=True) + EPS)


def _kv_outer(k, vz):
    return lax.dot_general(k, vz, (((0,), (0,)), ((), ())), preferred_element_type=F32)


def _inproj_kernel(x_ref, mod_ref, gain_ref, w_ref, rbase_ref, rlocal_ref, zeta_ref,
                   u_ref, q_ref, k_ref, v_ref, g_ref, sb_ref, state_ref, *, n_chunks, dec_b):
    @pl.when(pl.program_id(1) == 0)
    def _():
        state_ref[...] = jnp.zeros_like(state_ref)

    x = x_ref[0]
    a = gain_ref[...] * (1.0 + mod_ref[0, 1:2, :])
    h = (_rms(x) * a + mod_ref[0, 0:1, :]).astype(BF16)
    proj = jnp.dot(h, w_ref[...], preferred_element_type=F32)
    u_ref[0] = proj[:, :FNET_WIDTH].astype(BF16)
    cos_l, sin_l = rlocal_ref[0], rlocal_ref[1]
    cos = rbase_ref[0, 0:1, :] * cos_l - rbase_ref[0, 1:2, :] * sin_l
    sin = rbase_ref[0, 3:4, :] * cos_l + rbase_ref[0, 2:3, :] * sin_l
    for hd in range(RET_HEADS):
        lo = hd * HEAD_DIM
        for off, ref in ((FNET_WIDTH, q_ref), (FNET_WIDTH + RET_WIDTH, k_ref)):
            t = proj[:, off + lo:off + lo + HEAD_DIM]
            ref[0, :, lo:lo + HEAD_DIM] = (t * cos + pltpu.roll(t, HEAD_DIM // 2, 1) * sin).astype(BF16)
    v_ref[0] = proj[:, FNET_WIDTH + 2 * RET_WIDTH:FNET_WIDTH + 3 * RET_WIDTH].astype(BF16)
    g_ref[0] = proj[:, FNET_WIDTH + 3 * RET_WIDTH:].astype(BF16)

    for c in reversed(range(n_chunks)):
        rows = pl.ds(c * CHUNK, CHUNK)
        for hd in range(RET_HEADS):
            lanes = slice(hd * HEAD_DIM, (hd + 1) * HEAD_DIM)
            sb_ref[0, c, hd] = state_ref[hd].astype(BF16)
            vz = (v_ref[0, rows, lanes].astype(F32) * zeta_ref[hd]).astype(BF16)
            state_ref[hd] = state_ref[hd] * dec_b[hd] + _kv_outer(k_ref[0, rows, lanes], vz)


def _inproj(x, mod, gain, w_in, zeta_b, dec_b):
    b, s, _ = x.shape
    tm = min(TM_INPROJ, s)
    nt = s // tm
    nc = tm // CHUNK
    rot_base, rot_local = _rotary_tables(s, tm)
    rev = lambda bi, ti: (bi, nt - 1 - ti, 0)
    tok = lambda width: pl.BlockSpec((1, tm, width), rev)
    const = lambda shape: pl.BlockSpec(shape, lambda bi, ti: (0,) * len(shape))
    act = jax.ShapeDtypeStruct((b, s, RET_WIDTH), BF16)
    states = pl.BlockSpec((1, nc, RET_HEADS, HEAD_DIM, HEAD_DIM), lambda bi, ti: (bi, nt - 1 - ti, 0, 0, 0))
    state_shape = jax.ShapeDtypeStruct((b, s // CHUNK, RET_HEADS, HEAD_DIM, HEAD_DIM), BF16)
    return pl.pallas_call(
        functools.partial(_inproj_kernel, n_chunks=nc, dec_b=dec_b),
        grid=(b, nt),
        in_specs=[tok(D_MODEL),
                  pl.BlockSpec((1, N_MOD, D_MODEL), lambda bi, ti: (bi, 0, 0)),
                  const((1, D_MODEL)),
                  const((D_MODEL, IN_WIDTH)),
                  pl.BlockSpec((1, 4, HEAD_DIM), lambda bi, ti: (nt - 1 - ti, 0, 0)),
                  const((2, tm, HEAD_DIM)),
                  const((RET_HEADS, CHUNK, HEAD_DIM))],
        out_specs=[tok(FNET_WIDTH), tok(RET_WIDTH), tok(RET_WIDTH), tok(RET_WIDTH), tok(RET_WIDTH),
                   states],
        out_shape=[jax.ShapeDtypeStruct((b, s, FNET_WIDTH), BF16), act, act, act, act,
                   state_shape],
        scratch_shapes=[pltpu.VMEM((RET_HEADS, HEAD_DIM, HEAD_DIM), F32)],
        compiler_params=_params(("parallel", "arbitrary")),
        name="inproj",
    )(x, mod, gain, w_in, rot_base, rot_local, zeta_b)


def _group_lanes():
    return [(g, slice(g * GROUP_DIM, (g + 1) * GROUP_DIM)) for g in range(FNET_GROUPS)]


def _unpack_rows(load, dst_ref, slab=0):
    units = dst_ref.shape[2] // BF16_ROWS
    half = units * F32_ROWS
    for g, lanes in _group_lanes():
        rows = load(lanes).reshape(units, BF16_ROWS, GROUP_DIM).astype(F32)
        dst_ref[slab, g, 0:half, :] = rows[:, :F32_ROWS, :].reshape(half, GROUP_DIM)
        dst_ref[slab, g, half:, :] = rows[:, F32_ROWS:, :].reshape(half, GROUP_DIM)


def _strided_rows(src_ref, row, slab=0):
    units = src_ref.shape[2] // BF16_ROWS
    start = (row // F32_ROWS) * (units * F32_ROWS) + row % F32_ROWS
    rows = pl.ds(start, units, stride=F32_ROWS)
    return jnp.concatenate([src_ref[slab, g, rows, :] for g in range(FNET_GROUPS)], axis=1).astype(BF16)


def _fnet_a_kernel(x_ref, fa_ref, y_ref, xs_ref, *, n1, slabs):
    for sl in range(slabs):
        _unpack_rows(lambda lanes: x_ref[0, :, sl * BF16_ROWS:(sl + 1) * BF16_ROWS, lanes], xs_ref, sl)
    for sl in range(slabs):
        for j in range(BF16_ROWS):
            yj = jnp.dot(fa_ref[...], _strided_rows(xs_ref, j, sl), preferred_element_type=F32)
            y_ref[0, sl * BF16_ROWS + j] = yj.astype(BF16).reshape(2, n1, FNET_WIDTH)


def _fnet_b_kernel(y_ref, p_ref, q_ref, cb_ref, sb_ref, m_ref, o_ref, ys_ref, *, slabs, reps, n2):
    half = MXU_DIM // 2
    blocks = BF16_ROWS // reps
    for sl in range(slabs):
        _unpack_rows(lambda lanes: y_ref[0, :, :, sl, :, lanes], ys_ref, sl)
    for sl in range(slabs):
        for j in range(blocks):
            blk = sl * blocks + j
            t = (p_ref[...] * cb_ref[blk] + q_ref[...] * sb_ref[blk]).astype(BF16)
            pieces = [_strided_rows(ys_ref, j * reps + r, sl) for r in range(reps)]
            yb = pieces[0] if reps == 1 else jnp.concatenate(pieces, axis=0)
            for pair in range(FNET_WIDTH // MXU_DIM):
                z = jnp.dot(t, yb[:, pair * MXU_DIM:(pair + 1) * MXU_DIM], preferred_element_type=F32)
                for sub in range(MXU_DIM // GROUP_DIM):
                    g = pair * (MXU_DIM // GROUP_DIM) + sub
                    zl = slice(sub * GROUP_DIM, (sub + 1) * GROUP_DIM)
                    zz = jnp.concatenate([z[:half, zl], z[half:, zl]], axis=1).astype(BF16)
                    res = jnp.dot(zz, m_ref[g], preferred_element_type=F32)
                    for r in range(reps):
                        o_ref[0, blk * reps + r, :, g * GROUP_DIM:(g + 1) * GROUP_DIM] = (
                            res[r * n2:(r + 1) * n2].astype(BF16))


def _fnet(u, m_chan):
    b, s, _ = u.shape
    n1 = DFT_N1
    n2 = s // n1
    assert s == n1 * n2 and MXU_DIM % (2 * n2) == 0 and n2 % BF16_ROWS == 0, s
    reps = MXU_DIM // (2 * n2)
    a_slabs = min(STAGE_A_SLABS, n2 // BF16_ROWS)
    tb = a_slabs * BF16_ROWS
    y = pl.pallas_call(
        functools.partial(_fnet_a_kernel, n1=n1, slabs=a_slabs),
        grid=(b, n2 // tb),
        in_specs=[pl.BlockSpec((1, n1, tb, FNET_WIDTH), lambda bi, ni: (bi, 0, ni, 0)),
                  pl.BlockSpec((2 * n1, n1), lambda bi, ni: (0, 0))],
        out_specs=pl.BlockSpec((1, tb, 2, n1, FNET_WIDTH), lambda bi, ni: (bi, ni, 0, 0, 0)),
        out_shape=jax.ShapeDtypeStruct((b, n2, 2, n1, FNET_WIDTH), BF16),
        scratch_shapes=[pltpu.VMEM((a_slabs, FNET_GROUPS, n1 * BF16_ROWS, GROUP_DIM), F32)],
        compiler_params=_params(("parallel", "parallel")),
        name="fnet_stage_a",
    )(u.reshape(b, n1, n2, FNET_WIDTH), _dft_stage_a(n1))

    slabs = STAGE_B_MXU_BLOCKS * reps // BF16_ROWS
    k1_step = slabs * BF16_ROWS
    p_mat, q_mat, cos_b, sin_b = _dft_stage_b(n1, n2)
    square = pl.BlockSpec((MXU_DIM, MXU_DIM), lambda ki, bi: (0, 0))
    phase = pl.BlockSpec((STAGE_B_MXU_BLOCKS, 1, MXU_DIM), lambda ki, bi: (ki, 0, 0))
    return pl.pallas_call(
        functools.partial(_fnet_b_kernel, slabs=slabs, reps=reps, n2=n2),
        grid=(n1 // k1_step, b),
        in_specs=[pl.BlockSpec((1, n2, 2, slabs, BF16_ROWS, FNET_WIDTH), lambda ki, bi: (bi, 0, 0, ki, 0, 0)),
                  square, square, phase, phase,
                  pl.BlockSpec((FNET_GROUPS, 2 * GROUP_DIM, GROUP_DIM), lambda ki, bi: (0, 0, 0))],
        out_specs=pl.BlockSpec((1, k1_step, n2, FNET_WIDTH), lambda ki, bi: (bi, ki, 0, 0)),
        out_shape=jax.ShapeDtypeStruct((b, n1, n2, FNET_WIDTH), BF16),
        scratch_shapes=[pltpu.VMEM((slabs, FNET_GROUPS, n2 * 2 * BF16_ROWS, GROUP_DIM), F32)],
        compiler_params=_params(("parallel", "parallel")),
        name="fnet_stage_b",
    )(y.reshape(b, n2, 2, n1 // BF16_ROWS, BF16_ROWS, FNET_WIDTH), p_mat, q_mat, cos_b, sin_b, m_chan)


def _mixer_kernel(mod_ref, f_ref, q_ref, k_ref, v_ref, g_ref, sb_ref, w_ref,
                  intra_ref, xif_ref, xib_ref, zeta_ref, o_ref, state_ref, r_ref, fs_ref, *, n_chunks, dec_f):
    @pl.when(pl.program_id(1) == 0)
    def _():
        state_ref[...] = jnp.zeros_like(state_ref)

    tiles = BF16_ROWS // n_chunks
    sub = pl.program_id(1) % tiles

    @pl.when(sub == 0)
    def _():
        _unpack_rows(lambda lanes: f_ref[0, :, :, lanes], fs_ref)

    f_rows = [_strided_rows(fs_ref, sub * n_chunks + c) for c in range(n_chunks)]
    mix = jnp.dot(jnp.concatenate(f_rows, axis=0), w_ref[:FNET_WIDTH, :], preferred_element_type=F32)

    for c in range(n_chunks):
        rows = pl.ds(c * CHUNK, CHUNK)
        for hd in range(RET_HEADS):
            lanes = slice(hd * HEAD_DIM, (hd + 1) * HEAD_DIM)
            q = q_ref[0, rows, lanes]
            k = k_ref[0, rows, lanes]
            v = v_ref[0, rows, lanes]
            qf = q.astype(F32)
            scores = lax.dot_general(q, k, (((1,), (1,)), ((), ())), preferred_element_type=F32)
            r = jnp.dot((scores * intra_ref[hd]).astype(BF16), v, preferred_element_type=F32)
            r += jnp.dot((qf * xif_ref[hd]).astype(BF16), state_ref[hd].astype(BF16),
                         preferred_element_type=F32)
            r += jnp.dot((qf * xib_ref[hd]).astype(BF16), sb_ref[0, c, hd], preferred_element_type=F32)
            vz = (v.astype(F32) * zeta_ref[hd]).astype(BF16)
            state_ref[hd] = state_ref[hd] * dec_f[hd] + _kv_outer(k, vz)
            gate = g_ref[0, rows, lanes].astype(F32)
            r_ref[rows, lanes] = (_rms(r) * (gate * jax.nn.sigmoid(gate))).astype(BF16)

    mix += jnp.dot(r_ref[...], w_ref[FNET_WIDTH:, :], preferred_element_type=F32)
    o_ref[0] = (mod_ref[0, 2:3, :] * mix).astype(BF16)


def _mixer(mod, f, q, k, v, g, sb, w_out, tables):
    intra, xi_f, xi_b, zeta_f, dec_f = tables
    b, s, _ = q.shape
    tm = min(TM_MIXER, s)
    nc = tm // CHUNK
    assert f.shape[1] == CHUNK and BF16_ROWS % nc == 0 and (s // tm) % (BF16_ROWS // nc) == 0
    tiles = BF16_ROWS // nc
    tok = lambda width: pl.BlockSpec((1, tm, width), lambda bi, ti: (bi, ti, 0))
    const = lambda shape: pl.BlockSpec(shape, lambda bi, ti: (0,) * len(shape))
    tab = const((RET_HEADS, CHUNK, HEAD_DIM))
    states = pl.BlockSpec((1, nc, RET_HEADS, HEAD_DIM, HEAD_DIM), lambda bi, ti: (bi, ti, 0, 0, 0))
    return pl.pallas_call(
        functools.partial(_mixer_kernel, n_chunks=nc, dec_f=dec_f),
        grid=(b, s // tm),
        in_specs=[pl.BlockSpec((1, N_MOD, D_MODEL), lambda bi, ti: (bi, 0, 0)),
                  pl.BlockSpec((1, CHUNK, BF16_ROWS, FNET_WIDTH), lambda bi, ti: (bi, 0, ti // tiles, 0)),
                  tok(RET_WIDTH), tok(RET_WIDTH), tok(RET_WIDTH), tok(RET_WIDTH),
                  states,
                  const((D_MODEL, D_MODEL)), tab, tab, tab, tab],
        out_specs=tok(D_MODEL),
        out_shape=jax.ShapeDtypeStruct((b, s, D_MODEL), BF16),
        scratch_shapes=[pltpu.VMEM((RET_HEADS, HEAD_DIM, HEAD_DIM), F32),
                        pltpu.VMEM((tm, RET_WIDTH), BF16),
                        pltpu.VMEM((1, FNET_GROUPS, CHUNK * BF16_ROWS, GROUP_DIM), F32)],
        compiler_params=_params(("parallel", "arbitrary")),
        name="mixer",
    )(mod, f, q, k, v, g, sb, w_out, intra, xi_f, xi_b, zeta_f)


def _mlp_kernel(x_ref, d_ref, mod_ref, gain_ref, w1_ref, w2_ref, gfin_ref, o_ref):
    x = x_ref[0] + d_ref[0].astype(F32)
    a = gain_ref[...] * (1.0 + mod_ref[0, 4:5, :])
    h = (_rms(x) * a + mod_ref[0, 3:4, :]).astype(BF16)
    out = None
    for part in range(MLP_FF_PARTS):
        cols = slice(part * (D_FF // MLP_FF_PARTS), (part + 1) * (D_FF // MLP_FF_PARTS))
        mid = jnp.dot(h, w1_ref[:, cols], preferred_element_type=F32)
        mid = jnp.square(jnp.maximum(mid, 0.0)).astype(BF16)
        slab = jnp.dot(mid, w2_ref[cols, :], preferred_element_type=F32)
        out = slab if out is None else out + slab
    y = x + mod_ref[0, 5:6, :] * out
    o_ref[0] = _rms(y) * gfin_ref[...]


def _mlp(x, delta, mod, gain, w1, w2, gain_final):
    b, s, _ = x.shape
    tm = min(TM_MLP, s)
    tok = pl.BlockSpec((1, tm, D_MODEL), lambda bi, ti: (bi, ti, 0))
    const = lambda shape: pl.BlockSpec(shape, lambda bi, ti: (0,) * len(shape))
    weight = lambda shape: pl.BlockSpec(shape, lambda bi, ti: (0,) * len(shape), pipeline_mode=pl.Buffered(1))
    return pl.pallas_call(
        _mlp_kernel,
        grid=(b, s // tm),
        in_specs=[tok, tok, pl.BlockSpec((1, N_MOD, D_MODEL), lambda bi, ti: (bi, 0, 0)),
                  const((1, D_MODEL)), weight((D_MODEL, D_FF)), weight((D_FF, D_MODEL)), const((1, D_MODEL))],
        out_specs=tok,
        out_shape=jax.ShapeDtypeStruct((b, s, D_MODEL), F32),
        compiler_params=_params(("parallel", "parallel")),
        name="mlp",
    )(x, delta, mod, gain, w1, w2, gain_final)


def _trunk(x, mod, weights, tables):
    gain_mix, w_in, m_chan, w_out, gain_mlp, w1, w2, gain_final = weights
    intra, xi_f, xi_b, zeta_f, zeta_b, dec_f, dec_b = tables
    u, q, k, v, g, sb = _inproj(x, mod, gain_mix, w_in, zeta_b, dec_b)
    f = _fnet(u, m_chan)
    delta = _mixer(mod, f, q, k, v, g, sb, w_out, (intra, xi_f, xi_b, zeta_f, dec_f))
    return _mlp(x, delta, mod, gain_mlp, w1, w2, gain_final)


def kernel(x_prompt, x_sample, c_prompt, c_sample, ada_w, ada_b, norm_mix, w_in, w_fnet, w_out,
           norm_mlp, w_mlp_in, w_mlp_out, norm_final):
    assert ada_w.shape[0] == 1, "one encoder layer"
    bp, bs = x_prompt.shape[0], x_sample.shape[0]
    c_all = jnp.concatenate([c_prompt, c_sample], axis=0)
    mod = _modulation(c_all, ada_w[0], ada_b[0]).reshape(bp + bs, N_MOD, D_MODEL)
    weights = (norm_mix[0].reshape(1, -1), w_in[0].astype(BF16), _fnet_channel_weights(w_fnet[0]),
               w_out[0].astype(BF16), norm_mlp[0].reshape(1, -1), w_mlp_in[0].astype(BF16),
               w_mlp_out[0].astype(BF16), norm_final.reshape(1, -1))
    tables = _retention_tables()
    y_prompt = _trunk(x_prompt, mod[:bp], weights, tables)
    y_sample = _trunk(x_sample, mod[bp:], weights, tables)
    return (y_prompt, y_sample)
```

```python
import functools

import numpy as np
import jax
import jax.numpy as jnp
from jax import lax
from jax.experimental import pallas as pl
from jax.experimental.pallas import tpu as pltpu

F32 = jnp.float32
BF16 = jnp.bfloat16

D_MODEL = 1024
FNET_WIDTH = 512
FNET_GROUPS = 4
GROUP_DIM = 128
RET_WIDTH = 512
RET_HEADS = 4
HEAD_DIM = 128
D_FF = 4096
CHUNK = 128
ROPE_BASE = 10000.0
EPS = 1e-6
N_MOD = 6
IN_WIDTH = FNET_WIDTH + 4 * RET_WIDTH
DECAY_OFFSET_FWD = 0.0
DECAY_OFFSET_BWD = 0.5

DFT_N1 = 128
MXU_DIM = 256
BF16_ROWS = 16
F32_ROWS = 8
STAGE_A_SLABS = 2
STAGE_B_MXU_BLOCKS = 32
VMEM_LIMIT = 56 * 1024 * 1024

TM_INPROJ = 1024
TM_MIXER = 1024
TM_MLP = 1024
MLP_FF_PARTS = 4


def _params(semantics):
    return pltpu.CompilerParams(dimension_semantics=semantics, vmem_limit_bytes=VMEM_LIMIT)


def _retention_tables():
    scale = HEAD_DIM ** -0.5
    h = np.arange(RET_HEADS, dtype=np.float64)
    lg_f = np.log1p(-np.exp2(-5.0 - DECAY_OFFSET_FWD - h))[:, None, None]
    lg_b = np.log1p(-np.exp2(-5.0 - DECAY_OFFSET_BWD - h))[:, None, None]
    i = np.arange(CHUNK, dtype=np.float64)[None, :, None]
    j = np.arange(CHUNK, dtype=np.float64)[None, None, :]
    ones = np.ones((1, 1, HEAD_DIM))
    intra = scale * np.where(i >= j, np.exp(lg_f * np.maximum(i - j, 0.0)), np.exp(lg_b * np.maximum(j - i, 0.0)))
    xi_f = np.exp(lg_f * (i + 1.0)) * ones
    xi_b = np.exp(lg_b * (CHUNK - i)) * ones
    zeta_f = scale * np.exp(lg_f * (CHUNK - 1.0 - i)) * ones
    zeta_b = scale * np.exp(lg_b * i) * ones
    dec_f = tuple(float(v) for v in np.exp(lg_f[:, 0, 0] * CHUNK))
    dec_b = tuple(float(v) for v in np.exp(lg_b[:, 0, 0] * CHUNK))
    f = lambda a: jnp.asarray(a.astype(np.float32))
    return f(intra), f(xi_f), f(xi_b), f(zeta_f), f(zeta_b), dec_f, dec_b


def _f32(a):
    return jnp.asarray(np.asarray(a).astype(np.float32))


def _rotary_tables(seq, tm):
    half = HEAD_DIM // 2
    inv = ROPE_BASE ** (-np.arange(half, dtype=np.float64) / half)
    inv = np.concatenate([inv, inv])
    sign = np.concatenate([-np.ones(half), np.ones(half)])
    base = np.arange(0, seq, tm, dtype=np.float64)[:, None] * inv[None, :]
    local = np.arange(tm, dtype=np.float64)[:, None] * inv[None, :]
    base_rows = np.stack([np.cos(base), np.sin(base), sign * np.cos(base), sign * np.sin(base)], axis=1)
    return _f32(base_rows), _f32(np.stack([np.cos(local), np.sin(local)], axis=0))


def _dft_stage_a(n1):
    k = np.arange(n1, dtype=np.int64)
    ang = 2.0 * np.pi * ((k[:, None] * k[None, :]) % n1) / n1
    return _f32(np.concatenate([np.cos(ang), -np.sin(ang)], axis=0) * (n1 ** -0.5)).astype(BF16)


def _dft_stage_b(n1, n2):
    reps = MXU_DIM // (2 * n2)
    seq = n1 * n2
    idx = np.arange(MXU_DIM, dtype=np.int64)
    row_c, row_r, row_k = idx // (reps * n2), (idx // n2) % reps, idx % n2
    col_r, col_s, col_c = idx // (2 * n2), (idx // 2) % n2, idx % 2
    alpha = 2.0 * np.pi * ((row_k[:, None] * col_s[None, :]) % n2) / n2
    ca, sa = np.cos(alpha), np.sin(alpha)
    c_out, c_in = row_c[:, None], col_c[None, :]
    p_mat = np.where(c_out == c_in, ca, np.where(c_out < c_in, sa, -sa))
    q_mat = np.where(c_out == c_in, -sa, np.where(c_out < c_in, ca, -ca))
    keep = (row_r[:, None] == col_r[None, :]) * ((n2 * GROUP_DIM) ** -0.5)
    blk = np.arange(n1 // reps, dtype=np.int64)[:, None]
    beta = 2.0 * np.pi * ((col_s[None, :] * (blk * reps + col_r[None, :])) % seq) / seq
    return _f32(p_mat * keep), _f32(q_mat * keep), _f32(np.cos(beta)[:, None, :]), _f32(np.sin(beta)[:, None, :])


def _channel_dft():
    c = np.arange(GROUP_DIM, dtype=np.int64)
    ang = 2.0 * np.pi * ((c[:, None] * c[None, :]) % GROUP_DIM) / GROUP_DIM
    return jnp.asarray(np.concatenate([np.cos(ang), np.sin(ang)], axis=0).astype(np.float32))


def _mod_kernel(c_ref, w_ref, b_ref, o_ref):
    c = c_ref[...]
    s = c * jax.nn.sigmoid(c)
    w = w_ref[...]
    s_hi, w_hi = s.astype(BF16), w.astype(BF16)
    s_lo, w_lo = (s - s_hi.astype(F32)).astype(BF16), (w - w_hi.astype(F32)).astype(BF16)
    dot = functools.partial(jnp.dot, preferred_element_type=F32)
    o_ref[...] = dot(s_hi, w_hi) + (dot(s_hi, w_lo) + dot(s_lo, w_hi)) + b_ref[...]


def _modulation(c, ada_w, ada_b):
    rows = c.shape[0]
    tn = 1536
    return pl.pallas_call(
        _mod_kernel,
        grid=(N_MOD * D_MODEL // tn,),
        in_specs=[pl.BlockSpec((rows, D_MODEL), lambda n: (0, 0)),
                  pl.BlockSpec((D_MODEL, tn), lambda n: (0, n)),
                  pl.BlockSpec((1, tn), lambda n: (0, n))],
        out_specs=pl.BlockSpec((rows, tn), lambda n: (0, n)),
        out_shape=jax.ShapeDtypeStruct((rows, N_MOD * D_MODEL), F32),
        compiler_params=_params(("arbitrary",)),
        name="adaln_mod",
    )(c, ada_w, ada_b.reshape(1, -1))


def _fnet_weight_kernel(cs_ref, w_ref, o_ref):
    o_ref[0] = jnp.dot(cs_ref[...], w_ref[0], preferred_element_type=F32,
                       precision=lax.Precision.HIGHEST).astype(BF16)


def _fnet_channel_weights(w_fnet):
    return pl.pallas_call(
        _fnet_weight_kernel,
        grid=(FNET_GROUPS,),
        in_specs=[pl.BlockSpec((2 * GROUP_DIM, GROUP_DIM), lambda g: (0, 0)),
                  pl.BlockSpec((1, GROUP_DIM, GROUP_DIM), lambda g: (g, 0, 0))],
        out_specs=pl.BlockSpec((1, 2 * GROUP_DIM, GROUP_DIM), lambda g: (g, 0, 0)),
        out_shape=jax.ShapeDtypeStruct((FNET_GROUPS, 2 * GROUP_DIM, GROUP_DIM), BF16),
        compiler_params=_params(("arbitrary",)),
        name="fnet_channel_weights",
    )(_channel_dft(), w_fnet)


def _rms(x):
    return x * lax.rsqrt(jnp.mean(x * x, axis=-1, keepdims=True) + EPS)


def _kv_outer(k, vz):
    return lax.dot_general(k, vz, (((0,), (0,)), ((), ())), preferred_element_type=F32)


def _inproj_kernel(x_ref, mod_ref, gain_ref, w_ref, rbase_ref, rlocal_ref, zeta_ref,
                   u_ref, q_ref, k_ref, v_ref, g_ref, sb_ref, state_ref, *, n_chunks, dec_b):
    @pl.when(pl.program_id(1) == 0)
    def _():
        state_ref[...] = jnp.zeros_like(state_ref)

    x = x_ref[0]
    a = gain_ref[...] * (1.0 + mod_ref[0, 1:2, :])
    h = (_rms(x) * a + mod_ref[0, 0:1, :]).astype(BF16)
    proj = jnp.dot(h, w_ref[...], preferred_element_type=F32)
    u_ref[0] = proj[:, :FNET_WIDTH].astype(BF16)
    cos_l, sin_l = rlocal_ref[0], rlocal_ref[1]
    cos = rbase_ref[0, 0:1, :] * cos_l - rbase_ref[0, 1:2, :] * sin_l
    sin = rbase_ref[0, 3:4, :] * cos_l + rbase_ref[0, 2:3, :] * sin_l
    for hd in range(RET_HEADS):
        lo = hd * HEAD_DIM
        for off, ref in ((FNET_WIDTH, q_ref), (FNET_WIDTH + RET_WIDTH, k_ref)):
            t = proj[:, off + lo:off + lo + HEAD_DIM]
            ref[0, :, lo:lo + HEAD_DIM] = (t * cos + pltpu.roll(t, HEAD_DIM // 2, 1) * sin).astype(BF16)
    v_ref[0] = proj[:, FNET_WIDTH + 2 * RET_WIDTH:FNET_WIDTH + 3 * RET_WIDTH].astype(BF16)
    g_ref[0] = proj[:, FNET_WIDTH + 3 * RET_WIDTH:].astype(BF16)

    for c in reversed(range(n_chunks)):
        rows = pl.ds(c * CHUNK, CHUNK)
        for hd in range(RET_HEADS):
            lanes = slice(hd * HEAD_DIM, (hd + 1) * HEAD_DIM)
            sb_ref[0, c, hd] = state_ref[hd].astype(BF16)
            vz = (v_ref[0, rows, lanes].astype(F32) * zeta_ref[hd]).astype(BF16)
            state_ref[hd] = state_ref[hd] * dec_b[hd] + _kv_outer(k_ref[0, rows, lanes], vz)


def _inproj(x, mod, gain, w_in, zeta_b, dec_b):
    b, s, _ = x.shape
    tm = min(TM_INPROJ, s)
    nt = s // tm
    nc = tm // CHUNK
    rot_base, rot_local = _rotary_tables(s, tm)
    rev = lambda bi, ti: (bi, nt - 1 - ti, 0)
    tok = lambda width: pl.BlockSpec((1, tm, width), rev)
    const = lambda shape: pl.BlockSpec(shape, lambda bi, ti: (0,) * len(shape))
    act = jax.ShapeDtypeStruct((b, s, RET_WIDTH), BF16)
    states = pl.BlockSpec((1, nc, RET_HEADS, HEAD_DIM, HEAD_DIM), lambda bi, ti: (bi, nt - 1 - ti, 0, 0, 0))
    state_shape = jax.ShapeDtypeStruct((b, s // CHUNK, RET_HEADS, HEAD_DIM, HEAD_DIM), BF16)
    return pl.pallas_call(
        functools.partial(_inproj_kernel, n_chunks=nc, dec_b=dec_b),
        grid=(b, nt),
        in_specs=[tok(D_MODEL),
                  pl.BlockSpec((1, N_MOD, D_MODEL), lambda bi, ti: (bi, 0, 0)),
                  const((1, D_MODEL)),
                  const((D_MODEL, IN_WIDTH)),
                  pl.BlockSpec((1, 4, HEAD_DIM), lambda bi, ti: (nt - 1 - ti, 0, 0)),
                  const((2, tm, HEAD_DIM)),
                  const((RET_HEADS, CHUNK, HEAD_DIM))],
        out_specs=[tok(FNET_WIDTH), tok(RET_WIDTH), tok(RET_WIDTH), tok(RET_WIDTH), tok(RET_WIDTH),
                   states],
        out_shape=[jax.ShapeDtypeStruct((b, s, FNET_WIDTH), BF16), act, act, act, act,
                   state_shape],
        scratch_shapes=[pltpu.VMEM((RET_HEADS, HEAD_DIM, HEAD_DIM), F32)],
        compiler_params=_params(("parallel", "arbitrary")),
        name="inproj",
    )(x, mod, gain, w_in, rot_base, rot_local, zeta_b)


def _group_lanes():
    return [(g, slice(g * GROUP_DIM, (g + 1) * GROUP_DIM)) for g in range(FNET_GROUPS)]


def _unpack_rows(load, dst_ref, slab=0):
    units = dst_ref.shape[2] // BF16_ROWS
    half = units * F32_ROWS
    for g, lanes in _group_lanes():
        rows = load(lanes).reshape(units, BF16_ROWS, GROUP_DIM).astype(F32)
        dst_ref[slab, g, 0:half, :] = rows[:, :F32_ROWS, :].reshape(half, GROUP_DIM)
        dst_ref[slab, g, half:, :] = rows[:, F32_ROWS:, :].reshape(half, GROUP_DIM)


def _strided_rows(src_ref, row, slab=0):
    units = src_ref.shape[2] // BF16_ROWS
    start = (row // F32_ROWS) * (units * F32_ROWS) + row % F32_ROWS
    rows = pl.ds(start, units, stride=F32_ROWS)
    return jnp.concatenate([src_ref[slab, g, rows, :] for g in range(FNET_GROUPS)], axis=1).astype(BF16)


def _fnet_a_kernel(x_ref, fa_ref, y_ref, xs_ref, *, n1, slabs):
    for sl in range(slabs):
        _unpack_rows(lambda lanes: x_ref[0, :, sl * BF16_ROWS:(sl + 1) * BF16_ROWS, lanes], xs_ref, sl)
    for sl in range(slabs):
        for j in range(BF16_ROWS):
            yj = jnp.dot(fa_ref[...], _strided_rows(xs_ref, j, sl), preferred_element_type=F32)
            y_ref[0, sl * BF16_ROWS + j] = yj.astype(BF16).reshape(2, n1, FNET_WIDTH)


def _fnet_b_kernel(y_ref, p_ref, q_ref, cb_ref, sb_ref, m_ref, o_ref, ys_ref, *, slabs, reps, n2):
    half = MXU_DIM // 2
    blocks = BF16_ROWS // reps
    for sl in range(slabs):
        _unpack_rows(lambda lanes: y_ref[0, :, :, sl, :, lanes], ys_ref, sl)
    for sl in range(slabs):
        for j in range(blocks):
            blk = sl * blocks + j
            t = (p_ref[...] * cb_ref[blk] + q_ref[...] * sb_ref[blk]).astype(BF16)
            pieces = [_strided_rows(ys_ref, j * reps + r, sl) for r in range(reps)]
            yb = pieces[0] if reps == 1 else jnp.concatenate(pieces, axis=0)
            for pair in range(FNET_WIDTH // MXU_DIM):
                z = jnp.dot(t, yb[:, pair * MXU_DIM:(pair + 1) * MXU_DIM], preferred_element_type=F32)
                for sub in range(MXU_DIM // GROUP_DIM):
                    g = pair * (MXU_DIM // GROUP_DIM) + sub
                    zl = slice(sub * GROUP_DIM, (sub + 1) * GROUP_DIM)
                    zz = jnp.concatenate([z[:half, zl], z[half:, zl]], axis=1).astype(BF16)
                    res = jnp.dot(zz, m_ref[g], preferred_element_type=F32)
                    for r in range(reps):
                        o_ref[0, blk * reps + r, :, g * GROUP_DIM:(g + 1) * GROUP_DIM] = (
                            res[r * n2:(r + 1) * n2].astype(BF16))


def _fnet(u, m_chan):
    b, s, _ = u.shape
    n1 = DFT_N1
    n2 = s // n1
    assert s == n1 * n2 and MXU_DIM % (2 * n2) == 0 and n2 % BF16_ROWS == 0, s
    reps = MXU_DIM // (2 * n2)
    a_slabs = min(STAGE_A_SLABS, n2 // BF16_ROWS)
    tb = a_slabs * BF16_ROWS
    y = pl.pallas_call(
        functools.partial(_fnet_a_kernel, n1=n1, slabs=a_slabs),
        grid=(b, n2 // tb),
        in_specs=[pl.BlockSpec((1, n1, tb, FNET_WIDTH), lambda bi, ni: (bi, 0, ni, 0)),
                  pl.BlockSpec((2 * n1, n1), lambda bi, ni: (0, 0))],
        out_specs=pl.BlockSpec((1, tb, 2, n1, FNET_WIDTH), lambda bi, ni: (bi, ni, 0, 0, 0)),
        out_shape=jax.ShapeDtypeStruct((b, n2, 2, n1, FNET_WIDTH), BF16),
        scratch_shapes=[pltpu.VMEM((a_slabs, FNET_GROUPS, n1 * BF16_ROWS, GROUP_DIM), F32)],
        compiler_params=_params(("parallel", "parallel")),
        name="fnet_stage_a",
    )(u.reshape(b, n1, n2, FNET_WIDTH), _dft_stage_a(n1))

    slabs = min(STAGE_B_MXU_BLOCKS * reps, n1) // BF16_ROWS
    k1_step = slabs * BF16_ROWS
    p_mat, q_mat, cos_b, sin_b = _dft_stage_b(n1, n2)
    square = pl.BlockSpec((MXU_DIM, MXU_DIM), lambda ki, bi: (0, 0))
    phase = pl.BlockSpec((k1_step // reps, 1, MXU_DIM), lambda ki, bi: (ki, 0, 0))
    return pl.pallas_call(
        functools.partial(_fnet_b_kernel, slabs=slabs, reps=reps, n2=n2),
        grid=(n1 // k1_step, b),
        in_specs=[pl.BlockSpec((1, n2, 2, slabs, BF16_ROWS, FNET_WIDTH), lambda ki, bi: (bi, 0, 0, ki, 0, 0)),
                  square, square, phase, phase,
                  pl.BlockSpec((FNET_GROUPS, 2 * GROUP_DIM, GROUP_DIM), lambda ki, bi: (0, 0, 0))],
        out_specs=pl.BlockSpec((1, k1_step, n2, FNET_WIDTH), lambda ki, bi: (bi, ki, 0, 0)),
        out_shape=jax.ShapeDtypeStruct((b, n1, n2, FNET_WIDTH), BF16),
        scratch_shapes=[pltpu.VMEM((slabs, FNET_GROUPS, n2 * 2 * BF16_ROWS, GROUP_DIM), F32)],
        compiler_params=_params(("parallel", "parallel")),
        name="fnet_stage_b",
    )(y.reshape(b, n2, 2, n1 // BF16_ROWS, BF16_ROWS, FNET_WIDTH), p_mat, q_mat, cos_b, sin_b, m_chan)


def _mixer_kernel(mod_ref, f_ref, q_ref, k_ref, v_ref, g_ref, sb_ref, w_ref,
                  intra_ref, xif_ref, xib_ref, zeta_ref, o_ref, state_ref, r_ref, fs_ref, *, n_chunks, dec_f):
    @pl.when(pl.program_id(1) == 0)
    def _():
        state_ref[...] = jnp.zeros_like(state_ref)

    tiles = BF16_ROWS // n_chunks
    sub = pl.program_id(1) % tiles

    @pl.when(sub == 0)
    def _():
        _unpack_rows(lambda lanes: f_ref[0, :, :, lanes], fs_ref)

    f_rows = [_strided_rows(fs_ref, sub * n_chunks + c) for c in range(n_chunks)]
    mix = jnp.dot(jnp.concatenate(f_rows, axis=0), w_ref[:FNET_WIDTH, :], preferred_element_type=F32)

    for c in range(n_chunks):
        rows = pl.ds(c * CHUNK, CHUNK)
        for hd in range(RET_HEADS):
            lanes = slice(hd * HEAD_DIM, (hd + 1) * HEAD_DIM)
            q = q_ref[0, rows, lanes]
            k = k_ref[0, rows, lanes]
            v = v_ref[0, rows, lanes]
            qf = q.astype(F32)
            scores = lax.dot_general(q, k, (((1,), (1,)), ((), ())), preferred_element_type=F32)
            r = jnp.dot((scores * intra_ref[hd]).astype(BF16), v, preferred_element_type=F32)
            r += jnp.dot((qf * xif_ref[hd]).astype(BF16), state_ref[hd].astype(BF16),
                         preferred_element_type=F32)
            r += jnp.dot((qf * xib_ref[hd]).astype(BF16), sb_ref[0, c, hd], preferred_element_type=F32)
            vz = (v.astype(F32) * zeta_ref[hd]).astype(BF16)
            state_ref[hd] = state_ref[hd] * dec_f[hd] + _kv_outer(k, vz)
            gate = g_ref[0, rows, lanes].astype(F32)
            r_ref[rows, lanes] = (_rms(r) * (gate * jax.nn.sigmoid(gate))).astype(BF16)

    mix += jnp.dot(r_ref[...], w_ref[FNET_WIDTH:, :], preferred_element_type=F32)
    o_ref[0] = (mod_ref[0, 2:3, :] * mix).astype(BF16)


def _mixer(mod, f, q, k, v, g, sb, w_out, tables):
    intra, xi_f, xi_b, zeta_f, dec_f = tables
    b, s, _ = q.shape
    tm = min(TM_MIXER, s)
    nc = tm // CHUNK
    assert f.shape[1] == CHUNK and BF16_ROWS % nc == 0 and (s // tm) % (BF16_ROWS // nc) == 0
    tiles = BF16_ROWS // nc
    tok = lambda width: pl.BlockSpec((1, tm, width), lambda bi, ti: (bi, ti, 0))
    const = lambda shape: pl.BlockSpec(shape, lambda bi, ti: (0,) * len(shape))
    tab = const((RET_HEADS, CHUNK, HEAD_DIM))
    states = pl.BlockSpec((1, nc, RET_HEADS, HEAD_DIM, HEAD_DIM), lambda bi, ti: (bi, ti, 0, 0, 0))
    return pl.pallas_call(
        functools.partial(_mixer_kernel, n_chunks=nc, dec_f=dec_f),
        grid=(b, s // tm),
        in_specs=[pl.BlockSpec((1, N_MOD, D_MODEL), lambda bi, ti: (bi, 0, 0)),
                  pl.BlockSpec((1, CHUNK, BF16_ROWS, FNET_WIDTH), lambda bi, ti: (bi, 0, ti // tiles, 0)),
                  tok(RET_WIDTH), tok(RET_WIDTH), tok(RET_WIDTH), tok(RET_WIDTH),
                  states,
                  const((D_MODEL, D_MODEL)), tab, tab, tab, tab],
        out_specs=tok(D_MODEL),
        out_shape=jax.ShapeDtypeStruct((b, s, D_MODEL), BF16),
        scratch_shapes=[pltpu.VMEM((RET_HEADS, HEAD_DIM, HEAD_DIM), F32),
                        pltpu.VMEM((tm, RET_WIDTH), BF16),
                        pltpu.VMEM((1, FNET_GROUPS, CHUNK * BF16_ROWS, GROUP_DIM), F32)],
        compiler_params=_params(("parallel", "arbitrary")),
        name="mixer",
    )(mod, f, q, k, v, g, sb, w_out, intra, xi_f, xi_b, zeta_f)


def _mlp_kernel(x_ref, d_ref, mod_ref, gain_ref, w1_ref, w2_ref, gfin_ref, o_ref):
    x = x_ref[0] + d_ref[0].astype(F32)
    a = gain_ref[...] * (1.0 + mod_ref[0, 4:5, :])
    h = (_rms(x) * a + mod_ref[0, 3:4, :]).astype(BF16)
    out = None
    for part in range(MLP_FF_PARTS):
        cols = slice(part * (D_FF // MLP_FF_PARTS), (part + 1) * (D_FF // MLP_FF_PARTS))
        mid = jnp.dot(h, w1_ref[:, cols], preferred_element_type=F32)
        mid = jnp.square(jnp.maximum(mid, 0.0)).astype(BF16)
        slab = jnp.dot(mid, w2_ref[cols, :], preferred_element_type=F32)
        out = slab if out is None else out + slab
    y = x + mod_ref[0, 5:6, :] * out
    o_ref[0] = _rms(y) * gfin_ref[...]


def _mlp(x, delta, mod, gain, w1, w2, gain_final):
    b, s, _ = x.shape
    tm = min(TM_MLP, s)
    tok = pl.BlockSpec((1, tm, D_MODEL), lambda bi, ti: (bi, ti, 0))
    const = lambda shape: pl.BlockSpec(shape, lambda bi, ti: (0,) * len(shape))
    weight = lambda shape: pl.BlockSpec(shape, lambda bi, ti: (0,) * len(shape), pipeline_mode=pl.Buffered(1))
    return pl.pallas_call(
        _mlp_kernel,
        grid=(b, s // tm),
        in_specs=[tok, tok, pl.BlockSpec((1, N_MOD, D_MODEL), lambda bi, ti: (bi, 0, 0)),
                  const((1, D_MODEL)), weight((D_MODEL, D_FF)), weight((D_FF, D_MODEL)), const((1, D_MODEL))],
        out_specs=tok,
        out_shape=jax.ShapeDtypeStruct((b, s, D_MODEL), F32),
        compiler_params=_params(("parallel", "parallel")),
        name="mlp",
    )(x, delta, mod, gain, w1, w2, gain_final)


def _trunk(x, mod, weights, tables):
    gain_mix, w_in, m_chan, w_out, gain_mlp, w1, w2, gain_final = weights
    intra, xi_f, xi_b, zeta_f, zeta_b, dec_f, dec_b = tables
    u, q, k, v, g, sb = _inproj(x, mod, gain_mix, w_in, zeta_b, dec_b)
    f = _fnet(u, m_chan)
    delta = _mixer(mod, f, q, k, v, g, sb, w_out, (intra, xi_f, xi_b, zeta_f, dec_f))
    return _mlp(x, delta, mod, gain_mlp, w1, w2, gain_final)


def kernel(x_prompt, x_sample, c_prompt, c_sample, ada_w, ada_b, norm_mix, w_in, w_fnet, w_out,
           norm_mlp, w_mlp_in, w_mlp_out, norm_final):
    assert ada_w.shape[0] == 1, "one encoder layer"
    bp, bs = x_prompt.shape[0], x_sample.shape[0]
    c_all = jnp.concatenate([c_prompt, c_sample], axis=0)
    mod = _modulation(c_all, ada_w[0], ada_b[0]).reshape(bp + bs, N_MOD, D_MODEL)
    weights = (norm_mix[0].reshape(1, -1), w_in[0].astype(BF16), _fnet_channel_weights(w_fnet[0]),
               w_out[0].astype(BF16), norm_mlp[0].reshape(1, -1), w_mlp_in[0].astype(BF16),
               w_mlp_out[0].astype(BF16), norm_final.reshape(1, -1))
    tables = _retention_tables()
    y_prompt = _trunk(x_prompt, mod[:bp], weights, tables)
    y_sample = _trunk(x_sample, mod[bp:], weights, tables)
    return (y_prompt, y_sample)
```

```python
import functools

import numpy as np
import jax
import jax.numpy as jnp
from jax import lax
from jax.experimental import pallas as pl
from jax.experimental.pallas import tpu as pltpu

F32 = jnp.float32
BF16 = jnp.bfloat16

D_MODEL = 1024
FNET_WIDTH = 512
FNET_GROUPS = 4
GROUP_DIM = 128
RET_WIDTH = 512
RET_HEADS = 4
HEAD_DIM = 128
D_FF = 4096
CHUNK = 128
ROPE_BASE = 10000.0
EPS = 1e-6
N_MOD = 6
IN_WIDTH = FNET_WIDTH + 4 * RET_WIDTH
DECAY_OFFSET_FWD = 0.0
DECAY_OFFSET_BWD = 0.5

DFT_N1 = 128
MXU_DIM = 256
BF16_ROWS = 16
F32_ROWS = 8
WORD_ROWS = BF16_ROWS // 2
STAGE_A_SLABS = 2
STAGE_B_MXU_BLOCKS = 16
VMEM_LIMIT = 56 * 1024 * 1024

TM_INPROJ = 1024
TM_MIXER = 1024
TM_MLP = 1024
MLP_FF_PARTS = 4


def _params(semantics):
    return pltpu.CompilerParams(dimension_semantics=semantics, vmem_limit_bytes=VMEM_LIMIT)


def _retention_tables():
    scale = HEAD_DIM ** -0.5
    h = np.arange(RET_HEADS, dtype=np.float64)
    lg_f = np.log1p(-np.exp2(-5.0 - DECAY_OFFSET_FWD - h))[:, None, None]
    lg_b = np.log1p(-np.exp2(-5.0 - DECAY_OFFSET_BWD - h))[:, None, None]
    i = np.arange(CHUNK, dtype=np.float64)[None, :, None]
    j = np.arange(CHUNK, dtype=np.float64)[None, None, :]
    ones = np.ones((1, 1, HEAD_DIM))
    intra = scale * np.where(i >= j, np.exp(lg_f * np.maximum(i - j, 0.0)), np.exp(lg_b * np.maximum(j - i, 0.0)))
    xi_f = np.exp(lg_f * (i + 1.0)) * ones
    xi_b = np.exp(lg_b * (CHUNK - i)) * ones
    zeta_f = scale * np.exp(lg_f * (CHUNK - 1.0 - i)) * ones
    zeta_b = scale * np.exp(lg_b * i) * ones
    dec_f = tuple(float(v) for v in np.exp(lg_f[:, 0, 0] * CHUNK))
    dec_b = tuple(float(v) for v in np.exp(lg_b[:, 0, 0] * CHUNK))
    f = lambda a: jnp.asarray(a.astype(np.float32))
    return f(intra), f(xi_f), f(xi_b), f(zeta_f), f(zeta_b), dec_f, dec_b


def _f32(a):
    return jnp.asarray(np.asarray(a).astype(np.float32))


def _rotary_tables(seq, tm):
    half = HEAD_DIM // 2
    inv = ROPE_BASE ** (-np.arange(half, dtype=np.float64) / half)
    inv = np.concatenate([inv, inv])
    sign = np.concatenate([-np.ones(half), np.ones(half)])
    base = np.arange(0, seq, tm, dtype=np.float64)[:, None] * inv[None, :]
    local = np.arange(tm, dtype=np.float64)[:, None] * inv[None, :]
    base_rows = np.stack([np.cos(base), np.sin(base), sign * np.cos(base), sign * np.sin(base)], axis=1)
    return _f32(base_rows), _f32(np.stack([np.cos(local), np.sin(local)], axis=0))


def _dft_stage_a(n1):
    k = np.arange(n1, dtype=np.int64)
    ang = 2.0 * np.pi * ((k[:, None] * k[None, :]) % n1) / n1
    base = np.stack([np.cos(ang), -np.sin(ang)], axis=1).reshape(2 * n1, n1) * (n1 ** -0.5)
    even, odd = np.zeros((2 * n1, 2 * n1)), np.zeros((2 * n1, 2 * n1))
    even[:, 0::2] = base
    odd[:, 1::2] = base
    return _f32(even).astype(BF16), _f32(odd).astype(BF16)


def _dft_stage_b(n1, n2):
    reps = MXU_DIM // (2 * n2)
    seq = n1 * n2
    idx = np.arange(MXU_DIM, dtype=np.int64)
    row_c, row_r, row_k = idx // (reps * n2), (idx // n2) % reps, idx % n2
    col_r, col_s, col_c = idx // (2 * n2), (idx // 2) % n2, idx % 2
    alpha = 2.0 * np.pi * ((row_k[:, None] * col_s[None, :]) % n2) / n2
    ca, sa = np.cos(alpha), np.sin(alpha)
    c_out, c_in = row_c[:, None], col_c[None, :]
    p_mat = np.where(c_out == c_in, ca, np.where(c_out < c_in, sa, -sa))
    q_mat = np.where(c_out == c_in, -sa, np.where(c_out < c_in, ca, -ca))
    keep = (row_r[:, None] == col_r[None, :]) * ((n2 * GROUP_DIM) ** -0.5)
    blk = np.arange(n1 // reps, dtype=np.int64)[:, None]
    beta = 2.0 * np.pi * ((col_s[None, :] * (blk * reps + col_r[None, :])) % seq) / seq
    return _f32(p_mat * keep), _f32(q_mat * keep), _f32(np.cos(beta)[:, None, :]), _f32(np.sin(beta)[:, None, :])


def _channel_dft():
    c = np.arange(GROUP_DIM, dtype=np.int64)
    ang = 2.0 * np.pi * ((c[:, None] * c[None, :]) % GROUP_DIM) / GROUP_DIM
    return jnp.asarray(np.concatenate([np.cos(ang), np.sin(ang)], axis=0).astype(np.float32))


def _mod_kernel(c_ref, w_ref, b_ref, o_ref):
    c = c_ref[...]
    s = c * jax.nn.sigmoid(c)
    w = w_ref[...]
    s_hi, w_hi = s.astype(BF16), w.astype(BF16)
    s_lo, w_lo = (s - s_hi.astype(F32)).astype(BF16), (w - w_hi.astype(F32)).astype(BF16)
    dot = functools.partial(jnp.dot, preferred_element_type=F32)
    o_ref[...] = dot(s_hi, w_hi) + (dot(s_hi, w_lo) + dot(s_lo, w_hi)) + b_ref[...]


def _modulation(c, ada_w, ada_b):
    rows = c.shape[0]
    tn = 1536
    return pl.pallas_call(
        _mod_kernel,
        grid=(N_MOD * D_MODEL // tn,),
        in_specs=[pl.BlockSpec((rows, D_MODEL), lambda n: (0, 0)),
                  pl.BlockSpec((D_MODEL, tn), lambda n: (0, n)),
                  pl.BlockSpec((1, tn), lambda n: (0, n))],
        out_specs=pl.BlockSpec((rows, tn), lambda n: (0, n)),
        out_shape=jax.ShapeDtypeStruct((rows, N_MOD * D_MODEL), F32),
        compiler_params=_params(("arbitrary",)),
        name="adaln_mod",
    )(c, ada_w, ada_b.reshape(1, -1))


def _fnet_weight_kernel(cs_ref, w_ref, o_ref):
    o_ref[0] = jnp.dot(cs_ref[...], w_ref[0], preferred_element_type=F32,
                       precision=lax.Precision.HIGHEST).astype(BF16)


def _fnet_channel_weights(w_fnet):
    return pl.pallas_call(
        _fnet_weight_kernel,
        grid=(FNET_GROUPS,),
        in_specs=[pl.BlockSpec((2 * GROUP_DIM, GROUP_DIM), lambda g: (0, 0)),
                  pl.BlockSpec((1, GROUP_DIM, GROUP_DIM), lambda g: (g, 0, 0))],
        out_specs=pl.BlockSpec((1, 2 * GROUP_DIM, GROUP_DIM), lambda g: (g, 0, 0)),
        out_shape=jax.ShapeDtypeStruct((FNET_GROUPS, 2 * GROUP_DIM, GROUP_DIM), BF16),
        compiler_params=_params(("arbitrary",)),
        name="fnet_channel_weights",
    )(_channel_dft(), w_fnet)


def _rms(x):
    return x * lax.rsqrt(jnp.mean(x * x, axis=-1, keepdims=True) + EPS)


def _kv_outer(k, vz):
    return lax.dot_general(k, vz, (((0,), (0,)), ((), ())), preferred_element_type=F32)


def _inproj_kernel(x_ref, mod_ref, gain_ref, w_ref, rbase_ref, rlocal_ref, zeta_ref,
                   u_ref, q_ref, k_ref, v_ref, g_ref, sb_ref, state_ref, *, n_chunks, dec_b):
    @pl.when(pl.program_id(1) == 0)
    def _():
        state_ref[...] = jnp.zeros_like(state_ref)

    x = x_ref[0]
    a = gain_ref[...] * (1.0 + mod_ref[0, 1:2, :])
    h = (_rms(x) * a + mod_ref[0, 0:1, :]).astype(BF16)
    proj = jnp.dot(h, w_ref[...], preferred_element_type=F32)
    u_ref[0] = proj[:, :FNET_WIDTH].astype(BF16)
    cos_l, sin_l = rlocal_ref[0], rlocal_ref[1]
    cos = rbase_ref[0, 0:1, :] * cos_l - rbase_ref[0, 1:2, :] * sin_l
    sin = rbase_ref[0, 3:4, :] * cos_l + rbase_ref[0, 2:3, :] * sin_l
    for hd in range(RET_HEADS):
        lo = hd * HEAD_DIM
        for off, ref in ((FNET_WIDTH, q_ref), (FNET_WIDTH + RET_WIDTH, k_ref)):
            t = proj[:, off + lo:off + lo + HEAD_DIM]
            ref[0, :, lo:lo + HEAD_DIM] = (t * cos + pltpu.roll(t, HEAD_DIM // 2, 1) * sin).astype(BF16)
    v_ref[0] = proj[:, FNET_WIDTH + 2 * RET_WIDTH:FNET_WIDTH + 3 * RET_WIDTH].astype(BF16)
    g_ref[0] = proj[:, FNET_WIDTH + 3 * RET_WIDTH:].astype(BF16)

    for c in reversed(range(n_chunks)):
        rows = pl.ds(c * CHUNK, CHUNK)
        for hd in range(RET_HEADS):
            lanes = slice(hd * HEAD_DIM, (hd + 1) * HEAD_DIM)
            sb_ref[0, c, hd] = state_ref[hd].astype(BF16)
            vz = (v_ref[0, rows, lanes].astype(F32) * zeta_ref[hd]).astype(BF16)
            state_ref[hd] = state_ref[hd] * dec_b[hd] + _kv_outer(k_ref[0, rows, lanes], vz)


def _inproj(x, mod, gain, w_in, zeta_b, dec_b):
    b, s, _ = x.shape
    tm = min(TM_INPROJ, s)
    nt = s // tm
    nc = tm // CHUNK
    rot_base, rot_local = _rotary_tables(s, tm)
    rev = lambda bi, ti: (bi, nt - 1 - ti, 0)
    tok = lambda width: pl.BlockSpec((1, tm, width), rev)
    const = lambda shape: pl.BlockSpec(shape, lambda bi, ti: (0,) * len(shape))
    act = jax.ShapeDtypeStruct((b, s, RET_WIDTH), BF16)
    states = pl.BlockSpec((1, nc, RET_HEADS, HEAD_DIM, HEAD_DIM), lambda bi, ti: (bi, nt - 1 - ti, 0, 0, 0))
    state_shape = jax.ShapeDtypeStruct((b, s // CHUNK, RET_HEADS, HEAD_DIM, HEAD_DIM), BF16)
    return pl.pallas_call(
        functools.partial(_inproj_kernel, n_chunks=nc, dec_b=dec_b),
        grid=(b, nt),
        in_specs=[tok(D_MODEL),
                  pl.BlockSpec((1, N_MOD, D_MODEL), lambda bi, ti: (bi, 0, 0)),
                  const((1, D_MODEL)),
                  const((D_MODEL, IN_WIDTH)),
                  pl.BlockSpec((1, 4, HEAD_DIM), lambda bi, ti: (nt - 1 - ti, 0, 0)),
                  const((2, tm, HEAD_DIM)),
                  const((RET_HEADS, CHUNK, HEAD_DIM))],
        out_specs=[tok(FNET_WIDTH), tok(RET_WIDTH), tok(RET_WIDTH), tok(RET_WIDTH), tok(RET_WIDTH),
                   states],
        out_shape=[jax.ShapeDtypeStruct((b, s, FNET_WIDTH), BF16), act, act, act, act,
                   state_shape],
        scratch_shapes=[pltpu.VMEM((RET_HEADS, HEAD_DIM, HEAD_DIM), F32)],
        compiler_params=_params(("parallel", "arbitrary")),
        name="inproj",
    )(x, mod, gain, w_in, rot_base, rot_local, zeta_b)


def _group_lanes():
    return [(g, slice(g * GROUP_DIM, (g + 1) * GROUP_DIM)) for g in range(FNET_GROUPS)]


def _unpack_rows(load, dst_ref, slab=0):
    units = dst_ref.shape[2] // BF16_ROWS
    half = units * F32_ROWS
    for g, lanes in _group_lanes():
        rows = load(lanes).reshape(units, BF16_ROWS, GROUP_DIM).astype(F32)
        dst_ref[slab, g, 0:half, :] = rows[:, :F32_ROWS, :].reshape(half, GROUP_DIM)
        dst_ref[slab, g, half:, :] = rows[:, F32_ROWS:, :].reshape(half, GROUP_DIM)


def _strided_rows(src_ref, row, slab=0):
    units = src_ref.shape[2] // BF16_ROWS
    start = (row // F32_ROWS) * (units * F32_ROWS) + row % F32_ROWS
    rows = pl.ds(start, units, stride=F32_ROWS)
    return jnp.concatenate([src_ref[slab, g, rows, :] for g in range(FNET_GROUPS)], axis=1).astype(BF16)


def _pack_words(load, dst_ref, slab=0):
    units = dst_ref.shape[2] // WORD_ROWS
    for g, lanes in _group_lanes():
        dst_ref[slab, g] = pltpu.bitcast(load(lanes).reshape(units * BF16_ROWS, GROUP_DIM), jnp.uint32)


def _strided_words(src_ref, word, slab=0):
    units = src_ref.shape[2] // WORD_ROWS
    rows = pl.ds(word, units, stride=WORD_ROWS)
    return jnp.concatenate([pltpu.bitcast(src_ref[slab, g, rows, :], BF16) for g in range(FNET_GROUPS)], axis=1)


def _fnet_a_kernel(x_ref, even_ref, odd_ref, y_ref, xw_ref, *, slabs):
    for sl in range(slabs):
        _pack_words(lambda lanes: x_ref[0, :, sl * BF16_ROWS:(sl + 1) * BF16_ROWS, lanes], xw_ref, sl)
    for sl in range(slabs):
        for i in range(WORD_ROWS):
            pair = _strided_words(xw_ref, i, sl)
            s2 = sl * BF16_ROWS + 2 * i
            y_ref[0, s2] = jnp.dot(even_ref[...], pair, preferred_element_type=F32).astype(BF16)
            y_ref[0, s2 + 1] = jnp.dot(odd_ref[...], pair, preferred_element_type=F32).astype(BF16)


def _fnet_b_kernel(y_ref, p_ref, q_ref, cb_ref, sb_ref, m_ref, o_ref, yw_ref, *, slabs, reps, n2):
    half = MXU_DIM // 2
    blocks = WORD_ROWS // reps
    for sl in range(slabs):
        _pack_words(lambda lanes: y_ref[0, :, sl, :, lanes], yw_ref, sl)
    for sl in range(slabs):
        for j in range(blocks):
            blk = sl * blocks + j
            t = (p_ref[...] * cb_ref[blk] + q_ref[...] * sb_ref[blk]).astype(BF16)
            pieces = [_strided_words(yw_ref, j * reps + r, sl) for r in range(reps)]
            yb = pieces[0] if reps == 1 else jnp.concatenate(pieces, axis=0)
            for pair in range(FNET_WIDTH // MXU_DIM):
                z = jnp.dot(t, yb[:, pair * MXU_DIM:(pair + 1) * MXU_DIM], preferred_element_type=F32)
                for sub in range(MXU_DIM // GROUP_DIM):
                    g = pair * (MXU_DIM // GROUP_DIM) + sub
                    zl = slice(sub * GROUP_DIM, (sub + 1) * GROUP_DIM)
                    zz = jnp.concatenate([z[:half, zl], z[half:, zl]], axis=1).astype(BF16)
                    res = jnp.dot(zz, m_ref[g], preferred_element_type=F32)
                    for r in range(reps):
                        o_ref[0, blk * reps + r, :, g * GROUP_DIM:(g + 1) * GROUP_DIM] = (
                            res[r * n2:(r + 1) * n2].astype(BF16))


def _fnet(u, m_chan):
    b, s, _ = u.shape
    n1 = DFT_N1
    n2 = s // n1
    assert s == n1 * n2 and MXU_DIM % (2 * n2) == 0 and n2 % BF16_ROWS == 0, s
    reps = MXU_DIM // (2 * n2)
    a_slabs = min(STAGE_A_SLABS, n2 // BF16_ROWS)
    tb = a_slabs * BF16_ROWS
    dft = pl.BlockSpec((2 * n1, 2 * n1), lambda bi, ni: (0, 0))
    y = pl.pallas_call(
        functools.partial(_fnet_a_kernel, slabs=a_slabs),
        grid=(b, n2 // tb),
        in_specs=[pl.BlockSpec((1, n1, tb, FNET_WIDTH), lambda bi, ni: (bi, 0, ni, 0)), dft, dft],
        out_specs=pl.BlockSpec((1, tb, 2 * n1, FNET_WIDTH), lambda bi, ni: (bi, ni, 0, 0)),
        out_shape=jax.ShapeDtypeStruct((b, n2, 2 * n1, FNET_WIDTH), BF16),
        scratch_shapes=[pltpu.VMEM((a_slabs, FNET_GROUPS, n1 * WORD_ROWS, GROUP_DIM), jnp.uint32)],
        compiler_params=_params(("parallel", "parallel")),
        name="fnet_stage_a",
    )(u.reshape(b, n1, n2, FNET_WIDTH), *_dft_stage_a(n1))

    slabs = STAGE_B_MXU_BLOCKS * reps // WORD_ROWS
    k1_step = slabs * WORD_ROWS
    p_mat, q_mat, cos_b, sin_b = _dft_stage_b(n1, n2)
    square = pl.BlockSpec((MXU_DIM, MXU_DIM), lambda ki, bi: (0, 0))
    phase = pl.BlockSpec((STAGE_B_MXU_BLOCKS, 1, MXU_DIM), lambda ki, bi: (ki, 0, 0))
    return pl.pallas_call(
        functools.partial(_fnet_b_kernel, slabs=slabs, reps=reps, n2=n2),
        grid=(n1 // k1_step, b),
        in_specs=[pl.BlockSpec((1, n2, slabs, BF16_ROWS, FNET_WIDTH), lambda ki, bi: (bi, 0, ki, 0, 0)),
                  square, square, phase, phase,
                  pl.BlockSpec((FNET_GROUPS, 2 * GROUP_DIM, GROUP_DIM), lambda ki, bi: (0, 0, 0))],
        out_specs=pl.BlockSpec((1, k1_step, n2, FNET_WIDTH), lambda ki, bi: (bi, ki, 0, 0)),
        out_shape=jax.ShapeDtypeStruct((b, n1, n2, FNET_WIDTH), BF16),
        scratch_shapes=[pltpu.VMEM((slabs, FNET_GROUPS, n2 * WORD_ROWS, GROUP_DIM), jnp.uint32)],
        compiler_params=_params(("parallel", "parallel")),
        name="fnet_stage_b",
    )(y.reshape(b, n2, 2 * n1 // BF16_ROWS, BF16_ROWS, FNET_WIDTH), p_mat, q_mat, cos_b, sin_b, m_chan)


def _mixer_kernel(mod_ref, f_ref, q_ref, k_ref, v_ref, g_ref, sb_ref, w_ref,
                  intra_ref, xif_ref, xib_ref, zeta_ref, o_ref, state_ref, r_ref, fs_ref, *, n_chunks, dec_f):
    @pl.when(pl.program_id(1) == 0)
    def _():
        state_ref[...] = jnp.zeros_like(state_ref)

    tiles = BF16_ROWS // n_chunks
    sub = pl.program_id(1) % tiles

    @pl.when(sub == 0)
    def _():
        _unpack_rows(lambda lanes: f_ref[0, :, :, lanes], fs_ref)

    f_rows = [_strided_rows(fs_ref, sub * n_chunks + c) for c in range(n_chunks)]
    mix = jnp.dot(jnp.concatenate(f_rows, axis=0), w_ref[:FNET_WIDTH, :], preferred_element_type=F32)

    for c in range(n_chunks):
        rows = pl.ds(c * CHUNK, CHUNK)
        for hd in range(RET_HEADS):
            lanes = slice(hd * HEAD_DIM, (hd + 1) * HEAD_DIM)
            q = q_ref[0, rows, lanes]
            k = k_ref[0, rows, lanes]
            v = v_ref[0, rows, lanes]
            qf = q.astype(F32)
            scores = lax.dot_general(q, k, (((1,), (1,)), ((), ())), preferred_element_type=F32)
            r = jnp.dot((scores * intra_ref[hd]).astype(BF16), v, preferred_element_type=F32)
            r += jnp.dot((qf * xif_ref[hd]).astype(BF16), state_ref[hd].astype(BF16),
                         preferred_element_type=F32)
            r += jnp.dot((qf * xib_ref[hd]).astype(BF16), sb_ref[0, c, hd], preferred_element_type=F32)
            vz = (v.astype(F32) * zeta_ref[hd]).astype(BF16)
            state_ref[hd] = state_ref[hd] * dec_f[hd] + _kv_outer(k, vz)
            gate = g_ref[0, rows, lanes].astype(F32)
            r_ref[rows, lanes] = (_rms(r) * (gate * jax.nn.sigmoid(gate))).astype(BF16)

    mix += jnp.dot(r_ref[...], w_ref[FNET_WIDTH:, :], preferred_element_type=F32)
    o_ref[0] = (mod_ref[0, 2:3, :] * mix).astype(BF16)


def _mixer(mod, f, q, k, v, g, sb, w_out, tables):
    intra, xi_f, xi_b, zeta_f, dec_f = tables
    b, s, _ = q.shape
    tm = min(TM_MIXER, s)
    nc = tm // CHUNK
    assert f.shape[1] == CHUNK and BF16_ROWS % nc == 0 and (s // tm) % (BF16_ROWS // nc) == 0
    tiles = BF16_ROWS // nc
    tok = lambda width: pl.BlockSpec((1, tm, width), lambda bi, ti: (bi, ti, 0))
    const = lambda shape: pl.BlockSpec(shape, lambda bi, ti: (0,) * len(shape))
    tab = const((RET_HEADS, CHUNK, HEAD_DIM))
    states = pl.BlockSpec((1, nc, RET_HEADS, HEAD_DIM, HEAD_DIM), lambda bi, ti: (bi, ti, 0, 0, 0))
    return pl.pallas_call(
        functools.partial(_mixer_kernel, n_chunks=nc, dec_f=dec_f),
        grid=(b, s // tm),
        in_specs=[pl.BlockSpec((1, N_MOD, D_MODEL), lambda bi, ti: (bi, 0, 0)),
                  pl.BlockSpec((1, CHUNK, BF16_ROWS, FNET_WIDTH), lambda bi, ti: (bi, 0, ti // tiles, 0)),
                  tok(RET_WIDTH), tok(RET_WIDTH), tok(RET_WIDTH), tok(RET_WIDTH),
                  states,
                  const((D_MODEL, D_MODEL)), tab, tab, tab, tab],
        out_specs=tok(D_MODEL),
        out_shape=jax.ShapeDtypeStruct((b, s, D_MODEL), BF16),
        scratch_shapes=[pltpu.VMEM((RET_HEADS, HEAD_DIM, HEAD_DIM), F32),
                        pltpu.VMEM((tm, RET_WIDTH), BF16),
                        pltpu.VMEM((1, FNET_GROUPS, CHUNK * BF16_ROWS, GROUP_DIM), F32)],
        compiler_params=_params(("parallel", "arbitrary")),
        name="mixer",
    )(mod, f, q, k, v, g, sb, w_out, intra, xi_f, xi_b, zeta_f)


def _mlp_kernel(x_ref, d_ref, mod_ref, gain_ref, w1_ref, w2_ref, gfin_ref, o_ref):
    x = x_ref[0] + d_ref[0].astype(F32)
    a = gain_ref[...] * (1.0 + mod_ref[0, 4:5, :])
    h = (_rms(x) * a + mod_ref[0, 3:4, :]).astype(BF16)
    out = None
    for part in range(MLP_FF_PARTS):
        cols = slice(part * (D_FF // MLP_FF_PARTS), (part + 1) * (D_FF // MLP_FF_PARTS))
        mid = jnp.dot(h, w1_ref[:, cols], preferred_element_type=F32)
        mid = jnp.square(jnp.maximum(mid, 0.0)).astype(BF16)
        slab = jnp.dot(mid, w2_ref[cols, :], preferred_element_type=F32)
        out = slab if out is None else out + slab
    y = x + mod_ref[0, 5:6, :] * out
    o_ref[0] = _rms(y) * gfin_ref[...]


def _mlp(x, delta, mod, gain, w1, w2, gain_final):
    b, s, _ = x.shape
    tm = min(TM_MLP, s)
    tok = pl.BlockSpec((1, tm, D_MODEL), lambda bi, ti: (bi, ti, 0))
    const = lambda shape: pl.BlockSpec(shape, lambda bi, ti: (0,) * len(shape))
    weight = lambda shape: pl.BlockSpec(shape, lambda bi, ti: (0,) * len(shape), pipeline_mode=pl.Buffered(1))
    return pl.pallas_call(
        _mlp_kernel,
        grid=(b, s // tm),
        in_specs=[tok, tok, pl.BlockSpec((1, N_MOD, D_MODEL), lambda bi, ti: (bi, 0, 0)),
                  const((1, D_MODEL)), weight((D_MODEL, D_FF)), weight((D_FF, D_MODEL)), const((1, D_MODEL))],
        out_specs=tok,
        out_shape=jax.ShapeDtypeStruct((b, s, D_MODEL), F32),
        compiler_params=_params(("parallel", "parallel")),
        name="mlp",
    )(x, delta, mod, gain, w1, w2, gain_final)


def _trunk(x, mod, weights, tables):
    gain_mix, w_in, m_chan, w_out, gain_mlp, w1, w2, gain_final = weights
    intra, xi_f, xi_b, zeta_f, zeta_b, dec_f, dec_b = tables
    u, q, k, v, g, sb = _inproj(x, mod, gain_mix, w_in, zeta_b, dec_b)
    f = _fnet(u, m_chan)
    delta = _mixer(mod, f, q, k, v, g, sb, w_out, (intra, xi_f, xi_b, zeta_f, dec_f))
    return _mlp(x, delta, mod, gain_mlp, w1, w2, gain_final)


def kernel(x_prompt, x_sample, c_prompt, c_sample, ada_w, ada_b, norm_mix, w_in, w_fnet, w_out,
           norm_mlp, w_mlp_in, w_mlp_out, norm_final):
    assert ada_w.shape[0] == 1, "one encoder layer"
    bp, bs = x_prompt.shape[0], x_sample.shape[0]
    c_all = jnp.concatenate([c_prompt, c_sample], axis=0)
    mod = _modulation(c_all, ada_w[0], ada_b[0]).reshape(bp + bs, N_MOD, D_MODEL)
    weights = (norm_mix[0].reshape(1, -1), w_in[0].astype(BF16), _fnet_channel_weights(w_fnet[0]),
               w_out[0].astype(BF16), norm_mlp[0].reshape(1, -1), w_mlp_in[0].astype(BF16),
               w_mlp_out[0].astype(BF16), norm_final.reshape(1, -1))
    tables = _retention_tables()
    y_prompt = _trunk(x_prompt, mod[:bp], weights, tables)
    y_sample = _trunk(x_sample, mod[bp:], weights, tables)
    return (y_prompt, y_sample)
```

```python
import functools

import numpy as np
import jax
import jax.numpy as jnp
from jax import lax
from jax.experimental import pallas as pl
from jax.experimental.pallas import tpu as pltpu

F32 = jnp.float32
BF16 = jnp.bfloat16

D_MODEL = 1024
FNET_WIDTH = 512
FNET_GROUPS = 4
GROUP_DIM = 128
RET_WIDTH = 512
RET_HEADS = 4
HEAD_DIM = 128
D_FF = 4096
CHUNK = 128
ROPE_BASE = 10000.0
EPS = 1e-6
N_MOD = 6
IN_WIDTH = FNET_WIDTH + 4 * RET_WIDTH
DECAY_OFFSET_FWD = 0.0
DECAY_OFFSET_BWD = 0.5

DFT_N1 = 128
MXU_DIM = 256
BF16_ROWS = 16
F32_ROWS = 8
WORD_ROWS = BF16_ROWS // 2
HALF_GROUPS = MXU_DIM // GROUP_DIM
Y_K1 = 72
FNET_UNROLL = 16
VMEM_LIMIT = 56 * 1024 * 1024

TM_INPROJ = 1024
TM_MIXER = 1024
TM_MLP = 1024
MLP_FF_PARTS = 4


def _params(semantics):
    return pltpu.CompilerParams(dimension_semantics=semantics, vmem_limit_bytes=VMEM_LIMIT)


def _retention_tables():
    scale = HEAD_DIM ** -0.5
    h = np.arange(RET_HEADS, dtype=np.float64)
    lg_f = np.log1p(-np.exp2(-5.0 - DECAY_OFFSET_FWD - h))[:, None, None]
    lg_b = np.log1p(-np.exp2(-5.0 - DECAY_OFFSET_BWD - h))[:, None, None]
    i = np.arange(CHUNK, dtype=np.float64)[None, :, None]
    j = np.arange(CHUNK, dtype=np.float64)[None, None, :]
    ones = np.ones((1, 1, HEAD_DIM))
    intra = scale * np.where(i >= j, np.exp(lg_f * np.maximum(i - j, 0.0)), np.exp(lg_b * np.maximum(j - i, 0.0)))
    xi_f = np.exp(lg_f * (i + 1.0)) * ones
    xi_b = np.exp(lg_b * (CHUNK - i)) * ones
    zeta_f = scale * np.exp(lg_f * (CHUNK - 1.0 - i)) * ones
    zeta_b = scale * np.exp(lg_b * i) * ones
    dec_f = tuple(float(v) for v in np.exp(lg_f[:, 0, 0] * CHUNK))
    dec_b = tuple(float(v) for v in np.exp(lg_b[:, 0, 0] * CHUNK))
    f = lambda a: jnp.asarray(a.astype(np.float32))
    return f(intra), f(xi_f), f(xi_b), f(zeta_f), f(zeta_b), dec_f, dec_b


def _f32(a):
    return jnp.asarray(np.asarray(a).astype(np.float32))


def _rotary_tables(seq, tm):
    half = HEAD_DIM // 2
    inv = ROPE_BASE ** (-np.arange(half, dtype=np.float64) / half)
    inv = np.concatenate([inv, inv])
    sign = np.concatenate([-np.ones(half), np.ones(half)])
    base = np.arange(0, seq, tm, dtype=np.float64)[:, None] * inv[None, :]
    local = np.arange(tm, dtype=np.float64)[:, None] * inv[None, :]
    base_rows = np.stack([np.cos(base), np.sin(base), sign * np.cos(base), sign * np.sin(base)], axis=1)
    return _f32(base_rows), _f32(np.stack([np.cos(local), np.sin(local)], axis=0))


def _dft_stage_a(n1):
    k = np.arange(n1, dtype=np.int64)
    ang = 2.0 * np.pi * ((k[:Y_K1, None] * k[None, :]) % n1) / n1
    base = np.stack([np.cos(ang), -np.sin(ang)], axis=1).reshape(2 * Y_K1, n1) * (n1 ** -0.5)
    even, odd = np.zeros((2 * Y_K1, 2 * n1)), np.zeros((2 * Y_K1, 2 * n1))
    even[:, 0::2] = base
    odd[:, 1::2] = base
    return _f32(even).astype(BF16), _f32(odd).astype(BF16)


def _dft_stage_b(n1, n2):
    reps = MXU_DIM // (2 * n2)
    seq = n1 * n2
    idx = np.arange(MXU_DIM, dtype=np.int64)
    row_c, row_r, row_k = idx // (reps * n2), (idx // n2) % reps, idx % n2
    col_r, col_s, col_c = idx // (2 * n2), (idx // 2) % n2, idx % 2
    alpha = 2.0 * np.pi * ((row_k[:, None] * col_s[None, :]) % n2) / n2
    ca, sa = np.cos(alpha), np.sin(alpha)
    c_out, c_in = row_c[:, None], col_c[None, :]
    p_mat = np.where(c_out == c_in, ca, np.where(c_out < c_in, sa, -sa))
    q_mat = np.where(c_out == c_in, -sa, np.where(c_out < c_in, ca, -ca))
    keep = (row_r[:, None] == col_r[None, :]) * ((n2 * GROUP_DIM) ** -0.5)
    blk = np.arange(n1 // reps, dtype=np.int64)[:, None]
    k1 = blk * reps + col_r[None, :]
    beta = 2.0 * np.pi * ((col_s[None, :] * k1) % seq) / seq
    conj = np.where((k1 >= Y_K1) & (col_c[None, :] == 1), -1.0, 1.0)
    return (_f32(p_mat * keep), _f32(q_mat * keep),
            _f32((np.cos(beta) * conj)[:, None, :]), _f32((np.sin(beta) * conj)[:, None, :]))


def _channel_dft():
    c = np.arange(GROUP_DIM, dtype=np.int64)
    ang = 2.0 * np.pi * ((c[:, None] * c[None, :]) % GROUP_DIM) / GROUP_DIM
    return jnp.asarray(np.concatenate([np.cos(ang), np.sin(ang)], axis=0).astype(np.float32))


def _mod_kernel(c_ref, w_ref, b_ref, o_ref):
    c = c_ref[...]
    s = c * jax.nn.sigmoid(c)
    w = w_ref[...]
    s_hi, w_hi = s.astype(BF16), w.astype(BF16)
    s_lo, w_lo = (s - s_hi.astype(F32)).astype(BF16), (w - w_hi.astype(F32)).astype(BF16)
    dot = functools.partial(jnp.dot, preferred_element_type=F32)
    o_ref[...] = dot(s_hi, w_hi) + (dot(s_hi, w_lo) + dot(s_lo, w_hi)) + b_ref[...]


def _modulation(c, ada_w, ada_b):
    rows = c.shape[0]
    tn = 1536
    return pl.pallas_call(
        _mod_kernel,
        grid=(N_MOD * D_MODEL // tn,),
        in_specs=[pl.BlockSpec((rows, D_MODEL), lambda n: (0, 0)),
                  pl.BlockSpec((D_MODEL, tn), lambda n: (0, n)),
                  pl.BlockSpec((1, tn), lambda n: (0, n))],
        out_specs=pl.BlockSpec((rows, tn), lambda n: (0, n)),
        out_shape=jax.ShapeDtypeStruct((rows, N_MOD * D_MODEL), F32),
        compiler_params=_params(("arbitrary",)),
        name="adaln_mod",
    )(c, ada_w, ada_b.reshape(1, -1))


def _fnet_weight_kernel(cs_ref, w_ref, o_ref):
    o_ref[0] = jnp.dot(cs_ref[...], w_ref[0], preferred_element_type=F32,
                       precision=lax.Precision.HIGHEST).astype(BF16)


def _fnet_channel_weights(w_fnet):
    return pl.pallas_call(
        _fnet_weight_kernel,
        grid=(FNET_GROUPS,),
        in_specs=[pl.BlockSpec((2 * GROUP_DIM, GROUP_DIM), lambda g: (0, 0)),
                  pl.BlockSpec((1, GROUP_DIM, GROUP_DIM), lambda g: (g, 0, 0))],
        out_specs=pl.BlockSpec((1, 2 * GROUP_DIM, GROUP_DIM), lambda g: (g, 0, 0)),
        out_shape=jax.ShapeDtypeStruct((FNET_GROUPS, 2 * GROUP_DIM, GROUP_DIM), BF16),
        compiler_params=_params(("arbitrary",)),
        name="fnet_channel_weights",
    )(_channel_dft(), w_fnet)


def _rms(x):
    return x * lax.rsqrt(jnp.mean(x * x, axis=-1, keepdims=True) + EPS)


def _kv_outer(k, vz):
    return lax.dot_general(k, vz, (((0,), (0,)), ((), ())), preferred_element_type=F32)


def _inproj_kernel(x_ref, mod_ref, gain_ref, w_ref, rbase_ref, rlocal_ref, zeta_ref,
                   u_ref, q_ref, k_ref, v_ref, g_ref, sb_ref, state_ref, *, n_chunks, dec_b):
    @pl.when(pl.program_id(1) == 0)
    def _():
        state_ref[...] = jnp.zeros_like(state_ref)

    x = x_ref[0]
    a = gain_ref[...] * (1.0 + mod_ref[0, 1:2, :])
    h = (_rms(x) * a + mod_ref[0, 0:1, :]).astype(BF16)
    proj = jnp.dot(h, w_ref[...], preferred_element_type=F32)
    u_ref[0] = proj[:, :FNET_WIDTH].astype(BF16)
    cos_l, sin_l = rlocal_ref[0], rlocal_ref[1]
    cos = rbase_ref[0, 0:1, :] * cos_l - rbase_ref[0, 1:2, :] * sin_l
    sin = rbase_ref[0, 3:4, :] * cos_l + rbase_ref[0, 2:3, :] * sin_l
    for hd in range(RET_HEADS):
        lo = hd * HEAD_DIM
        for off, ref in ((FNET_WIDTH, q_ref), (FNET_WIDTH + RET_WIDTH, k_ref)):
            t = proj[:, off + lo:off + lo + HEAD_DIM]
            ref[0, :, lo:lo + HEAD_DIM] = (t * cos + pltpu.roll(t, HEAD_DIM // 2, 1) * sin).astype(BF16)
    v_ref[0] = proj[:, FNET_WIDTH + 2 * RET_WIDTH:FNET_WIDTH + 3 * RET_WIDTH].astype(BF16)
    g_ref[0] = proj[:, FNET_WIDTH + 3 * RET_WIDTH:].astype(BF16)

    for c in reversed(range(n_chunks)):
        rows = pl.ds(c * CHUNK, CHUNK)
        for hd in range(RET_HEADS):
            lanes = slice(hd * HEAD_DIM, (hd + 1) * HEAD_DIM)
            sb_ref[0, c, hd] = state_ref[hd].astype(BF16)
            vz = (v_ref[0, rows, lanes].astype(F32) * zeta_ref[hd]).astype(BF16)
            state_ref[hd] = state_ref[hd] * dec_b[hd] + _kv_outer(k_ref[0, rows, lanes], vz)


def _inproj(x, mod, gain, w_in, zeta_b, dec_b):
    b, s, _ = x.shape
    tm = min(TM_INPROJ, s)
    nt = s // tm
    nc = tm // CHUNK
    rot_base, rot_local = _rotary_tables(s, tm)
    rev = lambda bi, ti: (bi, nt - 1 - ti, 0)
    tok = lambda width: pl.BlockSpec((1, tm, width), rev)
    const = lambda shape: pl.BlockSpec(shape, lambda bi, ti: (0,) * len(shape))
    act = jax.ShapeDtypeStruct((b, s, RET_WIDTH), BF16)
    states = pl.BlockSpec((1, nc, RET_HEADS, HEAD_DIM, HEAD_DIM), lambda bi, ti: (bi, nt - 1 - ti, 0, 0, 0))
    state_shape = jax.ShapeDtypeStruct((b, s // CHUNK, RET_HEADS, HEAD_DIM, HEAD_DIM), BF16)
    return pl.pallas_call(
        functools.partial(_inproj_kernel, n_chunks=nc, dec_b=dec_b),
        grid=(b, nt),
        in_specs=[tok(D_MODEL),
                  pl.BlockSpec((1, N_MOD, D_MODEL), lambda bi, ti: (bi, 0, 0)),
                  const((1, D_MODEL)),
                  const((D_MODEL, IN_WIDTH)),
                  pl.BlockSpec((1, 4, HEAD_DIM), lambda bi, ti: (nt - 1 - ti, 0, 0)),
                  const((2, tm, HEAD_DIM)),
                  const((RET_HEADS, CHUNK, HEAD_DIM))],
        out_specs=[tok(FNET_WIDTH), tok(RET_WIDTH), tok(RET_WIDTH), tok(RET_WIDTH), tok(RET_WIDTH),
                   states],
        out_shape=[jax.ShapeDtypeStruct((b, s, FNET_WIDTH), BF16), act, act, act, act,
                   state_shape],
        scratch_shapes=[pltpu.VMEM((RET_HEADS, HEAD_DIM, HEAD_DIM), F32)],
        compiler_params=_params(("parallel", "arbitrary")),
        name="inproj",
    )(x, mod, gain, w_in, rot_base, rot_local, zeta_b)


def _group_lanes():
    return [(g, slice(g * GROUP_DIM, (g + 1) * GROUP_DIM)) for g in range(FNET_GROUPS)]


def _unpack_rows(load, dst_ref, slab=0):
    units = dst_ref.shape[2] // BF16_ROWS
    half = units * F32_ROWS
    for g, lanes in _group_lanes():
        rows = load(lanes).reshape(units, BF16_ROWS, GROUP_DIM).astype(F32)
        dst_ref[slab, g, 0:half, :] = rows[:, :F32_ROWS, :].reshape(half, GROUP_DIM)
        dst_ref[slab, g, half:, :] = rows[:, F32_ROWS:, :].reshape(half, GROUP_DIM)


def _strided_rows(src_ref, row, slab=0):
    units = src_ref.shape[2] // BF16_ROWS
    start = (row // F32_ROWS) * (units * F32_ROWS) + row % F32_ROWS
    rows = pl.ds(start, units, stride=F32_ROWS)
    return jnp.concatenate([src_ref[slab, g, rows, :] for g in range(FNET_GROUPS)], axis=1).astype(BF16)


def _word_rows(src_ref, lead, word, units):
    rows = pl.ds(word, units, stride=WORD_ROWS)
    return jnp.concatenate([pltpu.bitcast(src_ref[lead, g, rows, :], BF16) for g in range(HALF_GROUPS)], axis=1)


def _fnet_kernel(x_ref, even_ref, odd_ref, p_ref, q_ref, cb_ref, sb_ref, m_ref, o_ref, xw_ref, yw_ref, *, n2, reps):
    n1 = DFT_N1
    half = MXU_DIM // 2
    groups = [slice(g * GROUP_DIM, (g + 1) * GROUP_DIM) for g in range(HALF_GROUPS)]

    def aligned(start, step):
        return start * step if isinstance(start, int) else pl.multiple_of(start * step, step)

    def stage_a(sl):
        s2_base = aligned(sl, BF16_ROWS)
        for g, lanes in enumerate(groups):
            block = x_ref[0, :, pl.ds(s2_base, BF16_ROWS), lanes]
            xw_ref[0, g] = pltpu.bitcast(block.reshape(n1 * BF16_ROWS, GROUP_DIM), jnp.uint32)
        for i in range(WORD_ROWS):
            pair = _word_rows(xw_ref, 0, i, n1)
            for parity, mat_ref in enumerate((even_ref, odd_ref)):
                y = jnp.dot(mat_ref[...], pair, preferred_element_type=F32).astype(BF16)
                words = pltpu.bitcast(y, jnp.uint32)
                row0 = aligned(s2_base + 2 * i + parity, WORD_ROWS)
                for ks in range(Y_K1 // WORD_ROWS):
                    for g, lanes in enumerate(groups):
                        yw_ref[ks, g, pl.ds(row0, WORD_ROWS), :] = words[ks * WORD_ROWS:(ks + 1) * WORD_ROWS, lanes]

    def stage_b(blk):
        t = (p_ref[...] * cb_ref[blk] + q_ref[...] * sb_ref[blk]).astype(BF16)
        pieces = []
        for r in range(reps):
            k1 = blk * reps + r
            src = (n1 - k1 if k1 >= Y_K1 else k1) if isinstance(k1, int) else jnp.where(k1 >= Y_K1, n1 - k1, k1)
            pieces.append(_word_rows(yw_ref, src // WORD_ROWS, src % WORD_ROWS, n2))
        yb = pieces[0] if reps == 1 else jnp.concatenate(pieces, axis=0)
        z = jnp.dot(t, yb, preferred_element_type=F32)
        for g, lanes in enumerate(groups):
            zz = jnp.concatenate([z[:half, lanes], z[half:, lanes]], axis=1).astype(BF16)
            res = jnp.dot(zz, m_ref[g], preferred_element_type=F32)
            for r in range(reps):
                o_ref[0, blk * reps + r, :, lanes] = res[r * n2:(r + 1) * n2].astype(BF16)

    _for_each(n2 // BF16_ROWS, stage_a, 1)
    _for_each(n1 // reps, stage_b, FNET_UNROLL)


def _for_each(count, body, group):
    if count <= group:
        for i in range(count):
            body(i)
        return

    def step(j, carry):
        for u in range(group):
            body(j * group + u)
        return carry

    lax.fori_loop(0, count // group, step, 0)


def _fnet(u, m_chan):
    b, s, _ = u.shape
    n1 = DFT_N1
    n2 = s // n1
    assert s == n1 * n2 and MXU_DIM % (2 * n2) == 0 and n2 % BF16_ROWS == 0, s
    reps = MXU_DIM // (2 * n2)
    p_mat, q_mat, cos_b, sin_b = _dft_stage_b(n1, n2)
    const = lambda shape: pl.BlockSpec(shape, lambda bi, hi: (0,) * len(shape))
    tokens = pl.BlockSpec((1, n1, n2, MXU_DIM), lambda bi, hi: (bi, 0, 0, hi))
    return pl.pallas_call(
        functools.partial(_fnet_kernel, n2=n2, reps=reps),
        grid=(b, FNET_WIDTH // MXU_DIM),
        in_specs=[tokens, const((2 * Y_K1, 2 * n1)), const((2 * Y_K1, 2 * n1)),
                  const((MXU_DIM, MXU_DIM)), const((MXU_DIM, MXU_DIM)),
                  const((n1 // reps, 1, MXU_DIM)), const((n1 // reps, 1, MXU_DIM)),
                  pl.BlockSpec((HALF_GROUPS, 2 * GROUP_DIM, GROUP_DIM), lambda bi, hi: (hi, 0, 0))],
        out_specs=tokens,
        out_shape=jax.ShapeDtypeStruct((b, n1, n2, FNET_WIDTH), BF16),
        scratch_shapes=[pltpu.VMEM((1, HALF_GROUPS, n1 * WORD_ROWS, GROUP_DIM), jnp.uint32),
                        pltpu.VMEM((Y_K1 // WORD_ROWS, HALF_GROUPS, n2 * WORD_ROWS, GROUP_DIM), jnp.uint32)],
        compiler_params=_params(("parallel", "parallel")),
        name="fnet",
    )(u.reshape(b, n1, n2, FNET_WIDTH), *_dft_stage_a(n1), p_mat, q_mat, cos_b, sin_b, m_chan)


def _mixer_kernel(mod_ref, f_ref, q_ref, k_ref, v_ref, g_ref, sb_ref, w_ref,
                  intra_ref, xif_ref, xib_ref, zeta_ref, o_ref, state_ref, r_ref, fs_ref, *, n_chunks, dec_f):
    @pl.when(pl.program_id(1) == 0)
    def _():
        state_ref[...] = jnp.zeros_like(state_ref)

    tiles = BF16_ROWS // n_chunks
    sub = pl.program_id(1) % tiles

    @pl.when(sub == 0)
    def _():
        _unpack_rows(lambda lanes: f_ref[0, :, :, lanes], fs_ref)

    f_rows = [_strided_rows(fs_ref, sub * n_chunks + c) for c in range(n_chunks)]
    mix = jnp.dot(jnp.concatenate(f_rows, axis=0), w_ref[:FNET_WIDTH, :], preferred_element_type=F32)

    for c in range(n_chunks):
        rows = pl.ds(c * CHUNK, CHUNK)
        for hd in range(RET_HEADS):
            lanes = slice(hd * HEAD_DIM, (hd + 1) * HEAD_DIM)
            q = q_ref[0, rows, lanes]
            k = k_ref[0, rows, lanes]
            v = v_ref[0, rows, lanes]
            qf = q.astype(F32)
            scores = lax.dot_general(q, k, (((1,), (1,)), ((), ())), preferred_element_type=F32)
            r = jnp.dot((scores * intra_ref[hd]).astype(BF16), v, preferred_element_type=F32)
            r += jnp.dot((qf * xif_ref[hd]).astype(BF16), state_ref[hd].astype(BF16),
                         preferred_element_type=F32)
            r += jnp.dot((qf * xib_ref[hd]).astype(BF16), sb_ref[0, c, hd], preferred_element_type=F32)
            vz = (v.astype(F32) * zeta_ref[hd]).astype(BF16)
            state_ref[hd] = state_ref[hd] * dec_f[hd] + _kv_outer(k, vz)
            gate = g_ref[0, rows, lanes].astype(F32)
            r_ref[rows, lanes] = (_rms(r) * (gate * jax.nn.sigmoid(gate))).astype(BF16)

    mix += jnp.dot(r_ref[...], w_ref[FNET_WIDTH:, :], preferred_element_type=F32)
    o_ref[0] = (mod_ref[0, 2:3, :] * mix).astype(BF16)


def _mixer(mod, f, q, k, v, g, sb, w_out, tables):
    intra, xi_f, xi_b, zeta_f, dec_f = tables
    b, s, _ = q.shape
    tm = min(TM_MIXER, s)
    nc = tm // CHUNK
    assert f.shape[1] == CHUNK and BF16_ROWS % nc == 0 and (s // tm) % (BF16_ROWS // nc) == 0
    tiles = BF16_ROWS // nc
    tok = lambda width: pl.BlockSpec((1, tm, width), lambda bi, ti: (bi, ti, 0))
    const = lambda shape: pl.BlockSpec(shape, lambda bi, ti: (0,) * len(shape))
    tab = const((RET_HEADS, CHUNK, HEAD_DIM))
    states = pl.BlockSpec((1, nc, RET_HEADS, HEAD_DIM, HEAD_DIM), lambda bi, ti: (bi, ti, 0, 0, 0))
    return pl.pallas_call(
        functools.partial(_mixer_kernel, n_chunks=nc, dec_f=dec_f),
        grid=(b, s // tm),
        in_specs=[pl.BlockSpec((1, N_MOD, D_MODEL), lambda bi, ti: (bi, 0, 0)),
                  pl.BlockSpec((1, CHUNK, BF16_ROWS, FNET_WIDTH), lambda bi, ti: (bi, 0, ti // tiles, 0)),
                  tok(RET_WIDTH), tok(RET_WIDTH), tok(RET_WIDTH), tok(RET_WIDTH),
                  states,
                  const((D_MODEL, D_MODEL)), tab, tab, tab, tab],
        out_specs=tok(D_MODEL),
        out_shape=jax.ShapeDtypeStruct((b, s, D_MODEL), BF16),
        scratch_shapes=[pltpu.VMEM((RET_HEADS, HEAD_DIM, HEAD_DIM), F32),
                        pltpu.VMEM((tm, RET_WIDTH), BF16),
                        pltpu.VMEM((1, FNET_GROUPS, CHUNK * BF16_ROWS, GROUP_DIM), F32)],
        compiler_params=_params(("parallel", "arbitrary")),
        name="mixer",
    )(mod, f, q, k, v, g, sb, w_out, intra, xi_f, xi_b, zeta_f)


def _mlp_kernel(x_ref, d_ref, mod_ref, gain_ref, w1_ref, w2_ref, gfin_ref, o_ref):
    x = x_ref[0] + d_ref[0].astype(F32)
    a = gain_ref[...] * (1.0 + mod_ref[0, 4:5, :])
    h = (_rms(x) * a + mod_ref[0, 3:4, :]).astype(BF16)
    out = None
    for part in range(MLP_FF_PARTS):
        cols = slice(part * (D_FF // MLP_FF_PARTS), (part + 1) * (D_FF // MLP_FF_PARTS))
        mid = jnp.dot(h, w1_ref[:, cols], preferred_element_type=F32)
        mid = jnp.square(jnp.maximum(mid, 0.0)).astype(BF16)
        slab = jnp.dot(mid, w2_ref[cols, :], preferred_element_type=F32)
        out = slab if out is None else out + slab
    y = x + mod_ref[0, 5:6, :] * out
    o_ref[0] = _rms(y) * gfin_ref[...]


def _mlp(x, delta, mod, gain, w1, w2, gain_final):
    b, s, _ = x.shape
    tm = min(TM_MLP, s)
    tok = pl.BlockSpec((1, tm, D_MODEL), lambda bi, ti: (bi, ti, 0))
    const = lambda shape: pl.BlockSpec(shape, lambda bi, ti: (0,) * len(shape))
    weight = lambda shape: pl.BlockSpec(shape, lambda bi, ti: (0,) * len(shape), pipeline_mode=pl.Buffered(1))
    return pl.pallas_call(
        _mlp_kernel,
        grid=(b, s // tm),
        in_specs=[tok, tok, pl.BlockSpec((1, N_MOD, D_MODEL), lambda bi, ti: (bi, 0, 0)),
                  const((1, D_MODEL)), weight((D_MODEL, D_FF)), weight((D_FF, D_MODEL)), const((1, D_MODEL))],
        out_specs=tok,
        out_shape=jax.ShapeDtypeStruct((b, s, D_MODEL), F32),
        compiler_params=_params(("parallel", "parallel")),
        name="mlp",
    )(x, delta, mod, gain, w1, w2, gain_final)


def _trunk(x, mod, weights, tables):
    gain_mix, w_in, m_chan, w_out, gain_mlp, w1, w2, gain_final = weights
    intra, xi_f, xi_b, zeta_f, zeta_b, dec_f, dec_b = tables
    u, q, k, v, g, sb = _inproj(x, mod, gain_mix, w_in, zeta_b, dec_b)
    f = _fnet(u, m_chan)
    delta = _mixer(mod, f, q, k, v, g, sb, w_out, (intra, xi_f, xi_b, zeta_f, dec_f))
    return _mlp(x, delta, mod, gain_mlp, w1, w2, gain_final)


def kernel(x_prompt, x_sample, c_prompt, c_sample, ada_w, ada_b, norm_mix, w_in, w_fnet, w_out,
           norm_mlp, w_mlp_in, w_mlp_out, norm_final):
    assert ada_w.shape[0] == 1, "one encoder layer"
    bp, bs = x_prompt.shape[0], x_sample.shape[0]
    c_all = jnp.concatenate([c_prompt, c_sample], axis=0)
    mod = _modulation(c_all, ada_w[0], ada_b[0]).reshape(bp + bs, N_MOD, D_MODEL)
    weights = (norm_mix[0].reshape(1, -1), w_in[0].astype(BF16), _fnet_channel_weights(w_fnet[0]),
               w_out[0].astype(BF16), norm_mlp[0].reshape(1, -1), w_mlp_in[0].astype(BF16),
               w_mlp_out[0].astype(BF16), norm_final.reshape(1, -1))
    tables = _retention_tables()
    y_prompt = _trunk(x_prompt, mod[:bp], weights, tables)
    y_sample = _trunk(x_sample, mod[bp:], weights, tables)
    return (y_prompt, y_sample)
```

```python
import functools

import numpy as np
import jax
import jax.numpy as jnp
from jax import lax
from jax.experimental import pallas as pl
from jax.experimental.pallas import tpu as pltpu

F32 = jnp.float32
BF16 = jnp.bfloat16

D_MODEL = 1024
FNET_WIDTH = 512
FNET_GROUPS = 4
GROUP_DIM = 128
RET_WIDTH = 512
RET_HEADS = 4
HEAD_DIM = 128
D_FF = 4096
CHUNK = 128
ROPE_BASE = 10000.0
EPS = 1e-6
N_MOD = 6
IN_WIDTH = FNET_WIDTH + 4 * RET_WIDTH
DECAY_OFFSET_FWD = 0.0
DECAY_OFFSET_BWD = 0.5

DFT_N1 = 128
MXU_DIM = 256
BF16_ROWS = 16
F32_ROWS = 8
WORD_ROWS = BF16_ROWS // 2
HALF_GROUPS = MXU_DIM // GROUP_DIM
Y_K1 = 72
FNET_UNROLL = 32
STAGE_A_UNROLL = 2
VMEM_LIMIT = 56 * 1024 * 1024

TM_INPROJ = 1024
TM_MIXER = 1024
TM_MLP = 1024
MLP_FF_PARTS = 4


def _params(semantics):
    return pltpu.CompilerParams(dimension_semantics=semantics, vmem_limit_bytes=VMEM_LIMIT)


def _retention_tables():
    scale = HEAD_DIM ** -0.5
    h = np.arange(RET_HEADS, dtype=np.float64)
    lg_f = np.log1p(-np.exp2(-5.0 - DECAY_OFFSET_FWD - h))[:, None, None]
    lg_b = np.log1p(-np.exp2(-5.0 - DECAY_OFFSET_BWD - h))[:, None, None]
    i = np.arange(CHUNK, dtype=np.float64)[None, :, None]
    j = np.arange(CHUNK, dtype=np.float64)[None, None, :]
    ones = np.ones((1, 1, HEAD_DIM))
    intra = scale * np.where(i >= j, np.exp(lg_f * np.maximum(i - j, 0.0)), np.exp(lg_b * np.maximum(j - i, 0.0)))
    xi_f = np.exp(lg_f * (i + 1.0)) * ones
    xi_b = np.exp(lg_b * (CHUNK - i)) * ones
    zeta_f = scale * np.exp(lg_f * (CHUNK - 1.0 - i)) * ones
    zeta_b = scale * np.exp(lg_b * i) * ones
    dec_f = tuple(float(v) for v in np.exp(lg_f[:, 0, 0] * CHUNK))
    dec_b = tuple(float(v) for v in np.exp(lg_b[:, 0, 0] * CHUNK))
    f = lambda a: jnp.asarray(a.astype(np.float32))
    return f(intra), f(xi_f), f(xi_b), f(zeta_f), f(zeta_b), dec_f, dec_b


def _f32(a):
    return jnp.asarray(np.asarray(a).astype(np.float32))


def _rotary_tables(seq, tm):
    half = HEAD_DIM // 2
    inv = ROPE_BASE ** (-np.arange(half, dtype=np.float64) / half)
    inv = np.concatenate([inv, inv])
    sign = np.concatenate([-np.ones(half), np.ones(half)])
    base = np.arange(0, seq, tm, dtype=np.float64)[:, None] * inv[None, :]
    local = np.arange(tm, dtype=np.float64)[:, None] * inv[None, :]
    base_rows = np.stack([np.cos(base), np.sin(base), sign * np.cos(base), sign * np.sin(base)], axis=1)
    return _f32(base_rows), _f32(np.stack([np.cos(local), np.sin(local)], axis=0))


def _dft_stage_a(n1):
    k = np.arange(n1, dtype=np.int64)
    ang = 2.0 * np.pi * ((k[:Y_K1, None] * k[None, :]) % n1) / n1
    base = np.stack([np.cos(ang), -np.sin(ang)], axis=1).reshape(2 * Y_K1, n1) * (n1 ** -0.5)
    even, odd = np.zeros((2 * Y_K1, 2 * n1)), np.zeros((2 * Y_K1, 2 * n1))
    even[:, 0::2] = base
    odd[:, 1::2] = base
    return _f32(even).astype(BF16), _f32(odd).astype(BF16)


def _dft_stage_b(n1, n2):
    reps = MXU_DIM // (2 * n2)
    seq = n1 * n2
    idx = np.arange(MXU_DIM, dtype=np.int64)
    row_c, row_r, row_k = idx // (reps * n2), (idx // n2) % reps, idx % n2
    col_r, col_s, col_c = idx // (2 * n2), (idx // 2) % n2, idx % 2
    alpha = 2.0 * np.pi * ((row_k[:, None] * col_s[None, :]) % n2) / n2
    ca, sa = np.cos(alpha), np.sin(alpha)
    c_out, c_in = row_c[:, None], col_c[None, :]
    p_mat = np.where(c_out == c_in, ca, np.where(c_out < c_in, sa, -sa))
    q_mat = np.where(c_out == c_in, -sa, np.where(c_out < c_in, ca, -ca))
    keep = (row_r[:, None] == col_r[None, :]) * ((n2 * GROUP_DIM) ** -0.5)
    blk = np.arange(n1 // reps, dtype=np.int64)[:, None]
    k1 = blk * reps + col_r[None, :]
    beta = 2.0 * np.pi * ((col_s[None, :] * k1) % seq) / seq
    conj = np.where((k1 >= Y_K1) & (col_c[None, :] == 1), -1.0, 1.0)
    return (_f32(p_mat * keep), _f32(q_mat * keep),
            _f32((np.cos(beta) * conj)[:, None, :]), _f32((np.sin(beta) * conj)[:, None, :]))


def _channel_dft():
    c = np.arange(GROUP_DIM, dtype=np.int64)
    ang = 2.0 * np.pi * ((c[:, None] * c[None, :]) % GROUP_DIM) / GROUP_DIM
    return jnp.asarray(np.concatenate([np.cos(ang), np.sin(ang)], axis=0).astype(np.float32))


def _mod_kernel(c_ref, w_ref, b_ref, o_ref):
    c = c_ref[...]
    s = c * jax.nn.sigmoid(c)
    w = w_ref[...]
    s_hi, w_hi = s.astype(BF16), w.astype(BF16)
    s_lo, w_lo = (s - s_hi.astype(F32)).astype(BF16), (w - w_hi.astype(F32)).astype(BF16)
    dot = functools.partial(jnp.dot, preferred_element_type=F32)
    o_ref[...] = dot(s_hi, w_hi) + (dot(s_hi, w_lo) + dot(s_lo, w_hi)) + b_ref[...]


def _modulation(c, ada_w, ada_b):
    rows = c.shape[0]
    tn = 1536
    return pl.pallas_call(
        _mod_kernel,
        grid=(N_MOD * D_MODEL // tn,),
        in_specs=[pl.BlockSpec((rows, D_MODEL), lambda n: (0, 0)),
                  pl.BlockSpec((D_MODEL, tn), lambda n: (0, n)),
                  pl.BlockSpec((1, tn), lambda n: (0, n))],
        out_specs=pl.BlockSpec((rows, tn), lambda n: (0, n)),
        out_shape=jax.ShapeDtypeStruct((rows, N_MOD * D_MODEL), F32),
        compiler_params=_params(("arbitrary",)),
        name="adaln_mod",
    )(c, ada_w, ada_b.reshape(1, -1))


def _fnet_weight_kernel(cs_ref, w_ref, o_ref):
    o_ref[0] = jnp.dot(cs_ref[...], w_ref[0], preferred_element_type=F32,
                       precision=lax.Precision.HIGHEST).astype(BF16)


def _fnet_channel_weights(w_fnet):
    return pl.pallas_call(
        _fnet_weight_kernel,
        grid=(FNET_GROUPS,),
        in_specs=[pl.BlockSpec((2 * GROUP_DIM, GROUP_DIM), lambda g: (0, 0)),
                  pl.BlockSpec((1, GROUP_DIM, GROUP_DIM), lambda g: (g, 0, 0))],
        out_specs=pl.BlockSpec((1, 2 * GROUP_DIM, GROUP_DIM), lambda g: (g, 0, 0)),
        out_shape=jax.ShapeDtypeStruct((FNET_GROUPS, 2 * GROUP_DIM, GROUP_DIM), BF16),
        compiler_params=_params(("arbitrary",)),
        name="fnet_channel_weights",
    )(_channel_dft(), w_fnet)


def _rms(x):
    return x * lax.rsqrt(jnp.mean(x * x, axis=-1, keepdims=True) + EPS)


def _kv_outer(k, vz):
    return lax.dot_general(k, vz, (((0,), (0,)), ((), ())), preferred_element_type=F32)


def _inproj_kernel(x_ref, mod_ref, gain_ref, w_ref, rbase_ref, rlocal_ref, zeta_ref,
                   u_ref, q_ref, k_ref, v_ref, g_ref, sb_ref, state_ref, *, n_chunks, dec_b):
    @pl.when(pl.program_id(1) == 0)
    def _():
        state_ref[...] = jnp.zeros_like(state_ref)

    x = x_ref[0]
    a = gain_ref[...] * (1.0 + mod_ref[0, 1:2, :])
    h = (_rms(x) * a + mod_ref[0, 0:1, :]).astype(BF16)
    proj = jnp.dot(h, w_ref[...], preferred_element_type=F32)
    u_ref[0] = proj[:, :FNET_WIDTH].astype(BF16)
    cos_l, sin_l = rlocal_ref[0], rlocal_ref[1]
    cos = rbase_ref[0, 0:1, :] * cos_l - rbase_ref[0, 1:2, :] * sin_l
    sin = rbase_ref[0, 3:4, :] * cos_l + rbase_ref[0, 2:3, :] * sin_l
    for hd in range(RET_HEADS):
        lo = hd * HEAD_DIM
        for off, ref in ((FNET_WIDTH, q_ref), (FNET_WIDTH + RET_WIDTH, k_ref)):
            t = proj[:, off + lo:off + lo + HEAD_DIM]
            ref[0, :, lo:lo + HEAD_DIM] = (t * cos + pltpu.roll(t, HEAD_DIM // 2, 1) * sin).astype(BF16)
    v_ref[0] = proj[:, FNET_WIDTH + 2 * RET_WIDTH:FNET_WIDTH + 3 * RET_WIDTH].astype(BF16)
    g_ref[0] = proj[:, FNET_WIDTH + 3 * RET_WIDTH:].astype(BF16)

    for c in reversed(range(n_chunks)):
        rows = pl.ds(c * CHUNK, CHUNK)
        for hd in range(RET_HEADS):
            lanes = slice(hd * HEAD_DIM, (hd + 1) * HEAD_DIM)
            sb_ref[0, c, hd] = state_ref[hd].astype(BF16)
            vz = (v_ref[0, rows, lanes].astype(F32) * zeta_ref[hd]).astype(BF16)
            state_ref[hd] = state_ref[hd] * dec_b[hd] + _kv_outer(k_ref[0, rows, lanes], vz)


def _inproj(x, mod, gain, w_in, zeta_b, dec_b):
    b, s, _ = x.shape
    tm = min(TM_INPROJ, s)
    nt = s // tm
    nc = tm // CHUNK
    rot_base, rot_local = _rotary_tables(s, tm)
    rev = lambda bi, ti: (bi, nt - 1 - ti, 0)
    tok = lambda width: pl.BlockSpec((1, tm, width), rev)
    const = lambda shape: pl.BlockSpec(shape, lambda bi, ti: (0,) * len(shape))
    act = jax.ShapeDtypeStruct((b, s, RET_WIDTH), BF16)
    states = pl.BlockSpec((1, nc, RET_HEADS, HEAD_DIM, HEAD_DIM), lambda bi, ti: (bi, nt - 1 - ti, 0, 0, 0))
    state_shape = jax.ShapeDtypeStruct((b, s // CHUNK, RET_HEADS, HEAD_DIM, HEAD_DIM), BF16)
    return pl.pallas_call(
        functools.partial(_inproj_kernel, n_chunks=nc, dec_b=dec_b),
        grid=(b, nt),
        in_specs=[tok(D_MODEL),
                  pl.BlockSpec((1, N_MOD, D_MODEL), lambda bi, ti: (bi, 0, 0)),
                  const((1, D_MODEL)),
                  const((D_MODEL, IN_WIDTH)),
                  pl.BlockSpec((1, 4, HEAD_DIM), lambda bi, ti: (nt - 1 - ti, 0, 0)),
                  const((2, tm, HEAD_DIM)),
                  const((RET_HEADS, CHUNK, HEAD_DIM))],
        out_specs=[tok(FNET_WIDTH), tok(RET_WIDTH), tok(RET_WIDTH), tok(RET_WIDTH), tok(RET_WIDTH),
                   states],
        out_shape=[jax.ShapeDtypeStruct((b, s, FNET_WIDTH), BF16), act, act, act, act,
                   state_shape],
        scratch_shapes=[pltpu.VMEM((RET_HEADS, HEAD_DIM, HEAD_DIM), F32)],
        compiler_params=_params(("parallel", "arbitrary")),
        name="inproj",
    )(x, mod, gain, w_in, rot_base, rot_local, zeta_b)


def _group_lanes():
    return [(g, slice(g * GROUP_DIM, (g + 1) * GROUP_DIM)) for g in range(FNET_GROUPS)]


def _unpack_rows(load, dst_ref, slab=0):
    units = dst_ref.shape[2] // BF16_ROWS
    half = units * F32_ROWS
    for g, lanes in _group_lanes():
        rows = load(lanes).reshape(units, BF16_ROWS, GROUP_DIM).astype(F32)
        dst_ref[slab, g, 0:half, :] = rows[:, :F32_ROWS, :].reshape(half, GROUP_DIM)
        dst_ref[slab, g, half:, :] = rows[:, F32_ROWS:, :].reshape(half, GROUP_DIM)


def _strided_rows(src_ref, row, slab=0):
    units = src_ref.shape[2] // BF16_ROWS
    start = (row // F32_ROWS) * (units * F32_ROWS) + row % F32_ROWS
    rows = pl.ds(start, units, stride=F32_ROWS)
    return jnp.concatenate([src_ref[slab, g, rows, :] for g in range(FNET_GROUPS)], axis=1).astype(BF16)


def _word_rows(src_ref, lead, word, units):
    rows = pl.ds(word, units, stride=WORD_ROWS)
    return jnp.concatenate([pltpu.bitcast(src_ref[lead, g, rows, :], BF16) for g in range(HALF_GROUPS)], axis=1)


def _fnet_kernel(x_ref, even_ref, odd_ref, p_ref, q_ref, cb_ref, sb_ref, m_ref, o_ref, xw_ref, yw_ref, *, n2, reps):
    n1 = DFT_N1
    half = MXU_DIM // 2
    groups = [slice(g * GROUP_DIM, (g + 1) * GROUP_DIM) for g in range(HALF_GROUPS)]

    def aligned(start, step):
        return start * step if isinstance(start, int) else pl.multiple_of(start * step, step)

    def stage_a(sl, slot):
        s2_base = aligned(sl, BF16_ROWS)
        for g, lanes in enumerate(groups):
            block = x_ref[0, :, pl.ds(s2_base, BF16_ROWS), lanes]
            xw_ref[slot, g] = pltpu.bitcast(block.reshape(n1 * BF16_ROWS, GROUP_DIM), jnp.uint32)
        for i in range(WORD_ROWS):
            pair = _word_rows(xw_ref, slot, i, n1)
            for parity, mat_ref in enumerate((even_ref, odd_ref)):
                y = jnp.dot(mat_ref[...], pair, preferred_element_type=F32).astype(BF16)
                words = pltpu.bitcast(y, jnp.uint32)
                row0 = aligned(s2_base + 2 * i + parity, WORD_ROWS)
                for ks in range(Y_K1 // WORD_ROWS):
                    for g, lanes in enumerate(groups):
                        yw_ref[ks, g, pl.ds(row0, WORD_ROWS), :] = words[ks * WORD_ROWS:(ks + 1) * WORD_ROWS, lanes]

    def stage_b(blk, slot):
        t = (p_ref[...] * cb_ref[blk] + q_ref[...] * sb_ref[blk]).astype(BF16)
        pieces = []
        for r in range(reps):
            k1 = blk * reps + r
            src = (n1 - k1 if k1 >= Y_K1 else k1) if isinstance(k1, int) else jnp.where(k1 >= Y_K1, n1 - k1, k1)
            pieces.append(_word_rows(yw_ref, src // WORD_ROWS, src % WORD_ROWS, n2))
        yb = pieces[0] if reps == 1 else jnp.concatenate(pieces, axis=0)
        z = jnp.dot(t, yb, preferred_element_type=F32)
        for g, lanes in enumerate(groups):
            zz = jnp.concatenate([z[:half, lanes], z[half:, lanes]], axis=1).astype(BF16)
            res = jnp.dot(zz, m_ref[g], preferred_element_type=F32)
            for r in range(reps):
                o_ref[0, blk * reps + r, :, lanes] = res[r * n2:(r + 1) * n2].astype(BF16)

    _for_each(n2 // BF16_ROWS, stage_a, STAGE_A_UNROLL)
    _for_each(n1 // reps, stage_b, FNET_UNROLL)


def _for_each(count, body, group):
    if count <= group:
        for i in range(count):
            body(i, i)
        return

    def step(j, carry):
        for u in range(group):
            body(j * group + u, u)
        return carry

    lax.fori_loop(0, count // group, step, 0)


def _fnet(u, m_chan):
    b, s, _ = u.shape
    n1 = DFT_N1
    n2 = s // n1
    assert s == n1 * n2 and MXU_DIM % (2 * n2) == 0 and n2 % BF16_ROWS == 0, s
    reps = MXU_DIM // (2 * n2)
    p_mat, q_mat, cos_b, sin_b = _dft_stage_b(n1, n2)
    const = lambda shape: pl.BlockSpec(shape, lambda bi, hi: (0,) * len(shape))
    tokens = pl.BlockSpec((1, n1, n2, MXU_DIM), lambda bi, hi: (bi, 0, 0, hi))
    return pl.pallas_call(
        functools.partial(_fnet_kernel, n2=n2, reps=reps),
        grid=(b, FNET_WIDTH // MXU_DIM),
        in_specs=[tokens, const((2 * Y_K1, 2 * n1)), const((2 * Y_K1, 2 * n1)),
                  const((MXU_DIM, MXU_DIM)), const((MXU_DIM, MXU_DIM)),
                  const((n1 // reps, 1, MXU_DIM)), const((n1 // reps, 1, MXU_DIM)),
                  pl.BlockSpec((HALF_GROUPS, 2 * GROUP_DIM, GROUP_DIM), lambda bi, hi: (hi, 0, 0))],
        out_specs=tokens,
        out_shape=jax.ShapeDtypeStruct((b, n1, n2, FNET_WIDTH), BF16),
        scratch_shapes=[pltpu.VMEM((STAGE_A_UNROLL, HALF_GROUPS, n1 * WORD_ROWS, GROUP_DIM), jnp.uint32),
                        pltpu.VMEM((Y_K1 // WORD_ROWS, HALF_GROUPS, n2 * WORD_ROWS, GROUP_DIM), jnp.uint32)],
        compiler_params=_params(("parallel", "parallel")),
        name="fnet",
    )(u.reshape(b, n1, n2, FNET_WIDTH), *_dft_stage_a(n1), p_mat, q_mat, cos_b, sin_b, m_chan)


def _mixer_kernel(mod_ref, f_ref, q_ref, k_ref, v_ref, g_ref, sb_ref, w_ref,
                  intra_ref, xif_ref, xib_ref, zeta_ref, o_ref, state_ref, r_ref, fs_ref, *, n_chunks, dec_f):
    @pl.when(pl.program_id(1) == 0)
    def _():
        state_ref[...] = jnp.zeros_like(state_ref)

    tiles = BF16_ROWS // n_chunks
    sub = pl.program_id(1) % tiles

    @pl.when(sub == 0)
    def _():
        _unpack_rows(lambda lanes: f_ref[0, :, :, lanes], fs_ref)

    f_rows = [_strided_rows(fs_ref, sub * n_chunks + c) for c in range(n_chunks)]
    mix = jnp.dot(jnp.concatenate(f_rows, axis=0), w_ref[:FNET_WIDTH, :], preferred_element_type=F32)

    for c in range(n_chunks):
        rows = pl.ds(c * CHUNK, CHUNK)
        for hd in range(RET_HEADS):
            lanes = slice(hd * HEAD_DIM, (hd + 1) * HEAD_DIM)
            q = q_ref[0, rows, lanes]
            k = k_ref[0, rows, lanes]
            v = v_ref[0, rows, lanes]
            qf = q.astype(F32)
            scores = lax.dot_general(q, k, (((1,), (1,)), ((), ())), preferred_element_type=F32)
            r = jnp.dot((scores * intra_ref[hd]).astype(BF16), v, preferred_element_type=F32)
            r += jnp.dot((qf * xif_ref[hd]).astype(BF16), state_ref[hd].astype(BF16),
                         preferred_element_type=F32)
            r += jnp.dot((qf * xib_ref[hd]).astype(BF16), sb_ref[0, c, hd], preferred_element_type=F32)
            vz = (v.astype(F32) * zeta_ref[hd]).astype(BF16)
            state_ref[hd] = state_ref[hd] * dec_f[hd] + _kv_outer(k, vz)
            gate = g_ref[0, rows, lanes].astype(F32)
            r_ref[rows, lanes] = (_rms(r) * (gate * jax.nn.sigmoid(gate))).astype(BF16)

    mix += jnp.dot(r_ref[...], w_ref[FNET_WIDTH:, :], preferred_element_type=F32)
    o_ref[0] = (mod_ref[0, 2:3, :] * mix).astype(BF16)


def _mixer(mod, f, q, k, v, g, sb, w_out, tables):
    intra, xi_f, xi_b, zeta_f, dec_f = tables
    b, s, _ = q.shape
    tm = min(TM_MIXER, s)
    nc = tm // CHUNK
    assert f.shape[1] == CHUNK and BF16_ROWS % nc == 0 and (s // tm) % (BF16_ROWS // nc) == 0
    tiles = BF16_ROWS // nc
    tok = lambda width: pl.BlockSpec((1, tm, width), lambda bi, ti: (bi, ti, 0))
    const = lambda shape: pl.BlockSpec(shape, lambda bi, ti: (0,) * len(shape))
    tab = const((RET_HEADS, CHUNK, HEAD_DIM))
    states = pl.BlockSpec((1, nc, RET_HEADS, HEAD_DIM, HEAD_DIM), lambda bi, ti: (bi, ti, 0, 0, 0))
    return pl.pallas_call(
        functools.partial(_mixer_kernel, n_chunks=nc, dec_f=dec_f),
        grid=(b, s // tm),
        in_specs=[pl.BlockSpec((1, N_MOD, D_MODEL), lambda bi, ti: (bi, 0, 0)),
                  pl.BlockSpec((1, CHUNK, BF16_ROWS, FNET_WIDTH), lambda bi, ti: (bi, 0, ti // tiles, 0)),
                  tok(RET_WIDTH), tok(RET_WIDTH), tok(RET_WIDTH), tok(RET_WIDTH),
                  states,
                  const((D_MODEL, D_MODEL)), tab, tab, tab, tab],
        out_specs=tok(D_MODEL),
        out_shape=jax.ShapeDtypeStruct((b, s, D_MODEL), BF16),
        scratch_shapes=[pltpu.VMEM((RET_HEADS, HEAD_DIM, HEAD_DIM), F32),
                        pltpu.VMEM((tm, RET_WIDTH), BF16),
                        pltpu.VMEM((1, FNET_GROUPS, CHUNK * BF16_ROWS, GROUP_DIM), F32)],
        compiler_params=_params(("parallel", "arbitrary")),
        name="mixer",
    )(mod, f, q, k, v, g, sb, w_out, intra, xi_f, xi_b, zeta_f)


def _mlp_kernel(x_ref, d_ref, mod_ref, gain_ref, w1_ref, w2_ref, gfin_ref, o_ref):
    x = x_ref[0] + d_ref[0].astype(F32)
    a = gain_ref[...] * (1.0 + mod_ref[0, 4:5, :])
    h = (_rms(x) * a + mod_ref[0, 3:4, :]).astype(BF16)
    out = None
    for part in range(MLP_FF_PARTS):
        cols = slice(part * (D_FF // MLP_FF_PARTS), (part + 1) * (D_FF // MLP_FF_PARTS))
        mid = jnp.dot(h, w1_ref[:, cols], preferred_element_type=F32)
        mid = jnp.square(jnp.maximum(mid, 0.0)).astype(BF16)
        slab = jnp.dot(mid, w2_ref[cols, :], preferred_element_type=F32)
        out = slab if out is None else out + slab
    y = x + mod_ref[0, 5:6, :] * out
    o_ref[0] = _rms(y) * gfin_ref[...]


def _mlp(x, delta, mod, gain, w1, w2, gain_final):
    b, s, _ = x.shape
    tm = min(TM_MLP, s)
    tok = pl.BlockSpec((1, tm, D_MODEL), lambda bi, ti: (bi, ti, 0))
    const = lambda shape: pl.BlockSpec(shape, lambda bi, ti: (0,) * len(shape))
    weight = lambda shape: pl.BlockSpec(shape, lambda bi, ti: (0,) * len(shape), pipeline_mode=pl.Buffered(1))
    return pl.pallas_call(
        _mlp_kernel,
        grid=(b, s // tm),
        in_specs=[tok, tok, pl.BlockSpec((1, N_MOD, D_MODEL), lambda bi, ti: (bi, 0, 0)),
                  const((1, D_MODEL)), weight((D_MODEL, D_FF)), weight((D_FF, D_MODEL)), const((1, D_MODEL))],
        out_specs=tok,
        out_shape=jax.ShapeDtypeStruct((b, s, D_MODEL), F32),
        compiler_params=_params(("parallel", "parallel")),
        name="mlp",
    )(x, delta, mod, gain, w1, w2, gain_final)


def _trunk(x, mod, weights, tables):
    gain_mix, w_in, m_chan, w_out, gain_mlp, w1, w2, gain_final = weights
    intra, xi_f, xi_b, zeta_f, zeta_b, dec_f, dec_b = tables
    u, q, k, v, g, sb = _inproj(x, mod, gain_mix, w_in, zeta_b, dec_b)
    f = _fnet(u, m_chan)
    delta = _mixer(mod, f, q, k, v, g, sb, w_out, (intra, xi_f, xi_b, zeta_f, dec_f))
    return _mlp(x, delta, mod, gain_mlp, w1, w2, gain_final)


def kernel(x_prompt, x_sample, c_prompt, c_sample, ada_w, ada_b, norm_mix, w_in, w_fnet, w_out,
           norm_mlp, w_mlp_in, w_mlp_out, norm_final):
    assert ada_w.shape[0] == 1, "one encoder layer"
    bp, bs = x_prompt.shape[0], x_sample.shape[0]
    c_all = jnp.concatenate([c_prompt, c_sample], axis=0)
    mod = _modulation(c_all, ada_w[0], ada_b[0]).reshape(bp + bs, N_MOD, D_MODEL)
    weights = (norm_mix[0].reshape(1, -1), w_in[0].astype(BF16), _fnet_channel_weights(w_fnet[0]),
               w_out[0].astype(BF16), norm_mlp[0].reshape(1, -1), w_mlp_in[0].astype(BF16),
               w_mlp_out[0].astype(BF16), norm_final.reshape(1, -1))
    tables = _retention_tables()
    y_prompt = _trunk(x_prompt, mod[:bp], weights, tables)
    y_sample = _trunk(x_sample, mod[bp:], weights, tables)
    return (y_prompt, y_sample)
```

```python
import functools

import numpy as np
import jax
import jax.numpy as jnp
from jax import lax
from jax.experimental import pallas as pl
from jax.experimental.pallas import tpu as pltpu

F32 = jnp.float32
BF16 = jnp.bfloat16

D_MODEL = 1024
FNET_WIDTH = 512
FNET_GROUPS = 4
GROUP_DIM = 128
RET_WIDTH = 512
RET_HEADS = 4
HEAD_DIM = 128
D_FF = 4096
CHUNK = 128
ROPE_BASE = 10000.0
EPS = 1e-6
N_MOD = 6
IN_WIDTH = FNET_WIDTH + 4 * RET_WIDTH
DECAY_OFFSET_FWD = 0.0
DECAY_OFFSET_BWD = 0.5

DFT_N1 = 128
MXU_DIM = 256
BF16_ROWS = 16
F32_ROWS = 8
WORD_ROWS = BF16_ROWS // 2
HALF_GROUPS = MXU_DIM // GROUP_DIM
Y_K1 = 72
FNET_UNROLL = 32
STAGE_A_UNROLL = 2
VMEM_LIMIT = 56 * 1024 * 1024

TM_INPROJ = 1024
TM_MIXER = 2048
TM_MLP = 1024
MLP_FF_PARTS = 4


def _params(semantics):
    return pltpu.CompilerParams(dimension_semantics=semantics, vmem_limit_bytes=VMEM_LIMIT)


def _retention_tables():
    scale = HEAD_DIM ** -0.5
    h = np.arange(RET_HEADS, dtype=np.float64)
    lg_f = np.log1p(-np.exp2(-5.0 - DECAY_OFFSET_FWD - h))[:, None, None]
    lg_b = np.log1p(-np.exp2(-5.0 - DECAY_OFFSET_BWD - h))[:, None, None]
    i = np.arange(CHUNK, dtype=np.float64)[None, :, None]
    j = np.arange(CHUNK, dtype=np.float64)[None, None, :]
    ones = np.ones((1, 1, HEAD_DIM))
    intra = scale * np.where(i >= j, np.exp(lg_f * np.maximum(i - j, 0.0)), np.exp(lg_b * np.maximum(j - i, 0.0)))
    xi_f = np.exp(lg_f * (i + 1.0)) * ones
    xi_b = np.exp(lg_b * (CHUNK - i)) * ones
    zeta_f = scale * np.exp(lg_f * (CHUNK - 1.0 - i)) * ones
    zeta_b = scale * np.exp(lg_b * i) * ones
    dec_f = tuple(float(v) for v in np.exp(lg_f[:, 0, 0] * CHUNK))
    dec_b = tuple(float(v) for v in np.exp(lg_b[:, 0, 0] * CHUNK))
    f = lambda a: jnp.asarray(a.astype(np.float32))
    return f(intra), f(xi_f), f(xi_b), f(zeta_f), f(zeta_b), dec_f, dec_b


def _f32(a):
    return jnp.asarray(np.asarray(a).astype(np.float32))


def _rotary_tables(seq, tm):
    half = HEAD_DIM // 2
    inv = ROPE_BASE ** (-np.arange(half, dtype=np.float64) / half)
    inv = np.concatenate([inv, inv])
    sign = np.concatenate([-np.ones(half), np.ones(half)])
    base = np.arange(0, seq, tm, dtype=np.float64)[:, None] * inv[None, :]
    local = np.arange(tm, dtype=np.float64)[:, None] * inv[None, :]
    base_rows = np.stack([np.cos(base), np.sin(base), sign * np.cos(base), sign * np.sin(base)], axis=1)
    return _f32(base_rows), _f32(np.stack([np.cos(local), np.sin(local)], axis=0))


def _dft_stage_a(n1):
    k = np.arange(n1, dtype=np.int64)
    ang = 2.0 * np.pi * ((k[:Y_K1, None] * k[None, :]) % n1) / n1
    base = np.stack([np.cos(ang), -np.sin(ang)], axis=1).reshape(2 * Y_K1, n1) * (n1 ** -0.5)
    even, odd = np.zeros((2 * Y_K1, 2 * n1)), np.zeros((2 * Y_K1, 2 * n1))
    even[:, 0::2] = base
    odd[:, 1::2] = base
    return _f32(even).astype(BF16), _f32(odd).astype(BF16)


def _dft_stage_b(n1, n2):
    reps = MXU_DIM // (2 * n2)
    seq = n1 * n2
    idx = np.arange(MXU_DIM, dtype=np.int64)
    row_c, row_r, row_k = idx // (reps * n2), (idx // n2) % reps, idx % n2
    col_r, col_s, col_c = idx // (2 * n2), (idx // 2) % n2, idx % 2
    alpha = 2.0 * np.pi * ((row_k[:, None] * col_s[None, :]) % n2) / n2
    ca, sa = np.cos(alpha), np.sin(alpha)
    c_out, c_in = row_c[:, None], col_c[None, :]
    p_mat = np.where(c_out == c_in, ca, np.where(c_out < c_in, sa, -sa))
    q_mat = np.where(c_out == c_in, -sa, np.where(c_out < c_in, ca, -ca))
    keep = (row_r[:, None] == col_r[None, :]) * ((n2 * GROUP_DIM) ** -0.5)
    blk = np.arange(n1 // reps, dtype=np.int64)[:, None]
    k1 = blk * reps + col_r[None, :]
    beta = 2.0 * np.pi * ((col_s[None, :] * k1) % seq) / seq
    conj = np.where((k1 >= Y_K1) & (col_c[None, :] == 1), -1.0, 1.0)
    return (_f32(p_mat * keep), _f32(q_mat * keep),
            _f32((np.cos(beta) * conj)[:, None, :]), _f32((np.sin(beta) * conj)[:, None, :]))


def _channel_dft():
    c = np.arange(GROUP_DIM, dtype=np.int64)
    ang = 2.0 * np.pi * ((c[:, None] * c[None, :]) % GROUP_DIM) / GROUP_DIM
    return jnp.asarray(np.concatenate([np.cos(ang), np.sin(ang)], axis=0).astype(np.float32))


def _mod_kernel(c_ref, w_ref, b_ref, o_ref):
    c = c_ref[...]
    s = c * jax.nn.sigmoid(c)
    w = w_ref[...]
    s_hi, w_hi = s.astype(BF16), w.astype(BF16)
    s_lo, w_lo = (s - s_hi.astype(F32)).astype(BF16), (w - w_hi.astype(F32)).astype(BF16)
    dot = functools.partial(jnp.dot, preferred_element_type=F32)
    o_ref[...] = dot(s_hi, w_hi) + (dot(s_hi, w_lo) + dot(s_lo, w_hi)) + b_ref[...]


def _modulation(c, ada_w, ada_b):
    rows = c.shape[0]
    tn = 1536
    return pl.pallas_call(
        _mod_kernel,
        grid=(N_MOD * D_MODEL // tn,),
        in_specs=[pl.BlockSpec((rows, D_MODEL), lambda n: (0, 0)),
                  pl.BlockSpec((D_MODEL, tn), lambda n: (0, n)),
                  pl.BlockSpec((1, tn), lambda n: (0, n))],
        out_specs=pl.BlockSpec((rows, tn), lambda n: (0, n)),
        out_shape=jax.ShapeDtypeStruct((rows, N_MOD * D_MODEL), F32),
        compiler_params=_params(("arbitrary",)),
        name="adaln_mod",
    )(c, ada_w, ada_b.reshape(1, -1))


def _fnet_weight_kernel(cs_ref, w_ref, o_ref):
    o_ref[0] = jnp.dot(cs_ref[...], w_ref[0], preferred_element_type=F32,
                       precision=lax.Precision.HIGHEST).astype(BF16)


def _fnet_channel_weights(w_fnet):
    return pl.pallas_call(
        _fnet_weight_kernel,
        grid=(FNET_GROUPS,),
        in_specs=[pl.BlockSpec((2 * GROUP_DIM, GROUP_DIM), lambda g: (0, 0)),
                  pl.BlockSpec((1, GROUP_DIM, GROUP_DIM), lambda g: (g, 0, 0))],
        out_specs=pl.BlockSpec((1, 2 * GROUP_DIM, GROUP_DIM), lambda g: (g, 0, 0)),
        out_shape=jax.ShapeDtypeStruct((FNET_GROUPS, 2 * GROUP_DIM, GROUP_DIM), BF16),
        compiler_params=_params(("arbitrary",)),
        name="fnet_channel_weights",
    )(_channel_dft(), w_fnet)


def _rms(x):
    return x * lax.rsqrt(jnp.mean(x * x, axis=-1, keepdims=True) + EPS)


def _kv_outer(k, vz):
    return lax.dot_general(k, vz, (((0,), (0,)), ((), ())), preferred_element_type=F32)


def _inproj_kernel(x_ref, mod_ref, gain_ref, w_ref, rbase_ref, rlocal_ref, zeta_ref,
                   u_ref, q_ref, k_ref, v_ref, g_ref, sb_ref, state_ref, *, n_chunks, dec_b):
    @pl.when(pl.program_id(1) == 0)
    def _():
        state_ref[...] = jnp.zeros_like(state_ref)

    x = x_ref[0]
    a = gain_ref[...] * (1.0 + mod_ref[0, 1:2, :])
    h = (_rms(x) * a + mod_ref[0, 0:1, :]).astype(BF16)
    proj = jnp.dot(h, w_ref[...], preferred_element_type=F32)
    u_ref[0] = proj[:, :FNET_WIDTH].astype(BF16)
    cos_l, sin_l = rlocal_ref[0], rlocal_ref[1]
    cos = rbase_ref[0, 0:1, :] * cos_l - rbase_ref[0, 1:2, :] * sin_l
    sin = rbase_ref[0, 3:4, :] * cos_l + rbase_ref[0, 2:3, :] * sin_l
    for hd in range(RET_HEADS):
        lo = hd * HEAD_DIM
        for off, ref in ((FNET_WIDTH, q_ref), (FNET_WIDTH + RET_WIDTH, k_ref)):
            t = proj[:, off + lo:off + lo + HEAD_DIM]
            ref[0, :, lo:lo + HEAD_DIM] = (t * cos + pltpu.roll(t, HEAD_DIM // 2, 1) * sin).astype(BF16)
    v_ref[0] = proj[:, FNET_WIDTH + 2 * RET_WIDTH:FNET_WIDTH + 3 * RET_WIDTH].astype(BF16)
    g_ref[0] = proj[:, FNET_WIDTH + 3 * RET_WIDTH:].astype(BF16)

    for c in reversed(range(n_chunks)):
        rows = pl.ds(c * CHUNK, CHUNK)
        for hd in range(RET_HEADS):
            lanes = slice(hd * HEAD_DIM, (hd + 1) * HEAD_DIM)
            sb_ref[0, c, hd] = state_ref[hd].astype(BF16)
            vz = (v_ref[0, rows, lanes].astype(F32) * zeta_ref[hd]).astype(BF16)
            state_ref[hd] = state_ref[hd] * dec_b[hd] + _kv_outer(k_ref[0, rows, lanes], vz)


def _inproj(x, mod, gain, w_in, zeta_b, dec_b):
    b, s, _ = x.shape
    tm = min(TM_INPROJ, s)
    nt = s // tm
    nc = tm // CHUNK
    rot_base, rot_local = _rotary_tables(s, tm)
    rev = lambda bi, ti: (bi, nt - 1 - ti, 0)
    tok = lambda width: pl.BlockSpec((1, tm, width), rev)
    const = lambda shape: pl.BlockSpec(shape, lambda bi, ti: (0,) * len(shape))
    act = jax.ShapeDtypeStruct((b, s, RET_WIDTH), BF16)
    states = pl.BlockSpec((1, nc, RET_HEADS, HEAD_DIM, HEAD_DIM), lambda bi, ti: (bi, nt - 1 - ti, 0, 0, 0))
    state_shape = jax.ShapeDtypeStruct((b, s // CHUNK, RET_HEADS, HEAD_DIM, HEAD_DIM), BF16)
    return pl.pallas_call(
        functools.partial(_inproj_kernel, n_chunks=nc, dec_b=dec_b),
        grid=(b, nt),
        in_specs=[tok(D_MODEL),
                  pl.BlockSpec((1, N_MOD, D_MODEL), lambda bi, ti: (bi, 0, 0)),
                  const((1, D_MODEL)),
                  const((D_MODEL, IN_WIDTH)),
                  pl.BlockSpec((1, 4, HEAD_DIM), lambda bi, ti: (nt - 1 - ti, 0, 0)),
                  const((2, tm, HEAD_DIM)),
                  const((RET_HEADS, CHUNK, HEAD_DIM))],
        out_specs=[tok(FNET_WIDTH), tok(RET_WIDTH), tok(RET_WIDTH), tok(RET_WIDTH), tok(RET_WIDTH),
                   states],
        out_shape=[jax.ShapeDtypeStruct((b, s, FNET_WIDTH), BF16), act, act, act, act,
                   state_shape],
        scratch_shapes=[pltpu.VMEM((RET_HEADS, HEAD_DIM, HEAD_DIM), F32)],
        compiler_params=_params(("parallel", "arbitrary")),
        name="inproj",
    )(x, mod, gain, w_in, rot_base, rot_local, zeta_b)


def _group_lanes():
    return [(g, slice(g * GROUP_DIM, (g + 1) * GROUP_DIM)) for g in range(FNET_GROUPS)]


def _unpack_rows(load, dst_ref, slab=0):
    units = dst_ref.shape[2] // BF16_ROWS
    half = units * F32_ROWS
    for g, lanes in _group_lanes():
        rows = load(lanes).reshape(units, BF16_ROWS, GROUP_DIM).astype(F32)
        dst_ref[slab, g, 0:half, :] = rows[:, :F32_ROWS, :].reshape(half, GROUP_DIM)
        dst_ref[slab, g, half:, :] = rows[:, F32_ROWS:, :].reshape(half, GROUP_DIM)


def _strided_rows(src_ref, row, slab=0):
    units = src_ref.shape[2] // BF16_ROWS
    start = (row // F32_ROWS) * (units * F32_ROWS) + row % F32_ROWS
    rows = pl.ds(start, units, stride=F32_ROWS)
    return jnp.concatenate([src_ref[slab, g, rows, :] for g in range(FNET_GROUPS)], axis=1).astype(BF16)


def _word_rows(src_ref, lead, word, units):
    rows = pl.ds(word, units, stride=WORD_ROWS)
    return jnp.concatenate([pltpu.bitcast(src_ref[lead, g, rows, :], BF16) for g in range(HALF_GROUPS)], axis=1)


def _fnet_kernel(x_ref, even_ref, odd_ref, p_ref, q_ref, cb_ref, sb_ref, m_ref, o_ref, xw_ref, yw_ref, *, n2, reps):
    n1 = DFT_N1
    half = MXU_DIM // 2
    groups = [slice(g * GROUP_DIM, (g + 1) * GROUP_DIM) for g in range(HALF_GROUPS)]

    def aligned(start, step):
        return start * step if isinstance(start, int) else pl.multiple_of(start * step, step)

    def stage_a(sl, slot):
        s2_base = aligned(sl, BF16_ROWS)
        for g, lanes in enumerate(groups):
            block = x_ref[0, :, pl.ds(s2_base, BF16_ROWS), lanes]
            xw_ref[slot, g] = pltpu.bitcast(block.reshape(n1 * BF16_ROWS, GROUP_DIM), jnp.uint32)
        for i in range(WORD_ROWS):
            pair = _word_rows(xw_ref, slot, i, n1)
            for parity, mat_ref in enumerate((even_ref, odd_ref)):
                y = jnp.dot(mat_ref[...], pair, preferred_element_type=F32).astype(BF16)
                words = pltpu.bitcast(y, jnp.uint32)
                row0 = aligned(s2_base + 2 * i + parity, WORD_ROWS)
                for ks in range(Y_K1 // WORD_ROWS):
                    for g, lanes in enumerate(groups):
                        yw_ref[ks, g, pl.ds(row0, WORD_ROWS), :] = words[ks * WORD_ROWS:(ks + 1) * WORD_ROWS, lanes]

    def stage_b(blk, slot):
        t = (p_ref[...] * cb_ref[blk] + q_ref[...] * sb_ref[blk]).astype(BF16)
        pieces = []
        for r in range(reps):
            k1 = blk * reps + r
            src = (n1 - k1 if k1 >= Y_K1 else k1) if isinstance(k1, int) else jnp.where(k1 >= Y_K1, n1 - k1, k1)
            pieces.append(_word_rows(yw_ref, src // WORD_ROWS, src % WORD_ROWS, n2))
        yb = pieces[0] if reps == 1 else jnp.concatenate(pieces, axis=0)
        z = jnp.dot(t, yb, preferred_element_type=F32)
        for g, lanes in enumerate(groups):
            zz = jnp.concatenate([z[:half, lanes], z[half:, lanes]], axis=1).astype(BF16)
            res = jnp.dot(zz, m_ref[g], preferred_element_type=F32)
            for r in range(reps):
                o_ref[0, blk * reps + r, :, lanes] = res[r * n2:(r + 1) * n2].astype(BF16)

    _for_each(n2 // BF16_ROWS, stage_a, STAGE_A_UNROLL)
    _for_each(n1 // reps, stage_b, FNET_UNROLL)


def _for_each(count, body, group):
    if count <= group:
        for i in range(count):
            body(i, i)
        return

    def step(j, carry):
        for u in range(group):
            body(j * group + u, u)
        return carry

    lax.fori_loop(0, count // group, step, 0)


def _fnet(u, m_chan):
    b, s, _ = u.shape
    n1 = DFT_N1
    n2 = s // n1
    assert s == n1 * n2 and MXU_DIM % (2 * n2) == 0 and n2 % BF16_ROWS == 0, s
    reps = MXU_DIM // (2 * n2)
    p_mat, q_mat, cos_b, sin_b = _dft_stage_b(n1, n2)
    const = lambda shape: pl.BlockSpec(shape, lambda bi, hi: (0,) * len(shape))
    tokens = pl.BlockSpec((1, n1, n2, MXU_DIM), lambda bi, hi: (bi, 0, 0, hi))
    return pl.pallas_call(
        functools.partial(_fnet_kernel, n2=n2, reps=reps),
        grid=(b, FNET_WIDTH // MXU_DIM),
        in_specs=[tokens, const((2 * Y_K1, 2 * n1)), const((2 * Y_K1, 2 * n1)),
                  const((MXU_DIM, MXU_DIM)), const((MXU_DIM, MXU_DIM)),
                  const((n1 // reps, 1, MXU_DIM)), const((n1 // reps, 1, MXU_DIM)),
                  pl.BlockSpec((HALF_GROUPS, 2 * GROUP_DIM, GROUP_DIM), lambda bi, hi: (hi, 0, 0))],
        out_specs=tokens,
        out_shape=jax.ShapeDtypeStruct((b, n1, n2, FNET_WIDTH), BF16),
        scratch_shapes=[pltpu.VMEM((STAGE_A_UNROLL, HALF_GROUPS, n1 * WORD_ROWS, GROUP_DIM), jnp.uint32),
                        pltpu.VMEM((Y_K1 // WORD_ROWS, HALF_GROUPS, n2 * WORD_ROWS, GROUP_DIM), jnp.uint32)],
        compiler_params=_params(("parallel", "parallel")),
        name="fnet",
    )(u.reshape(b, n1, n2, FNET_WIDTH), *_dft_stage_a(n1), p_mat, q_mat, cos_b, sin_b, m_chan)


def _mixer_kernel(mod_ref, f_ref, q_ref, k_ref, v_ref, g_ref, sb_ref, w_ref,
                  intra_ref, xif_ref, xib_ref, zeta_ref, o_ref, state_ref, r_ref, fs_ref, *, n_chunks, dec_f):
    @pl.when(pl.program_id(1) == 0)
    def _():
        state_ref[...] = jnp.zeros_like(state_ref)

    tiles = BF16_ROWS // n_chunks
    sub = pl.program_id(1) % tiles

    @pl.when(sub == 0)
    def _():
        _unpack_rows(lambda lanes: f_ref[0, :, :, lanes], fs_ref)

    f_rows = [_strided_rows(fs_ref, sub * n_chunks + c) for c in range(n_chunks)]
    mix = jnp.dot(jnp.concatenate(f_rows, axis=0), w_ref[:FNET_WIDTH, :], preferred_element_type=F32)

    for c in range(n_chunks):
        rows = pl.ds(c * CHUNK, CHUNK)
        for hd in range(RET_HEADS):
            lanes = slice(hd * HEAD_DIM, (hd + 1) * HEAD_DIM)
            q = q_ref[0, rows, lanes]
            k = k_ref[0, rows, lanes]
            v = v_ref[0, rows, lanes]
            qf = q.astype(F32)
            scores = lax.dot_general(q, k, (((1,), (1,)), ((), ())), preferred_element_type=F32)
            r = jnp.dot((scores * intra_ref[hd]).astype(BF16), v, preferred_element_type=F32)
            r += jnp.dot((qf * xif_ref[hd]).astype(BF16), state_ref[hd].astype(BF16),
                         preferred_element_type=F32)
            r += jnp.dot((qf * xib_ref[hd]).astype(BF16), sb_ref[0, c, hd], preferred_element_type=F32)
            vz = (v.astype(F32) * zeta_ref[hd]).astype(BF16)
            state_ref[hd] = state_ref[hd] * dec_f[hd] + _kv_outer(k, vz)
            gate = g_ref[0, rows, lanes].astype(F32)
            r_ref[rows, lanes] = (_rms(r) * (gate * jax.nn.sigmoid(gate))).astype(BF16)

    mix += jnp.dot(r_ref[...], w_ref[FNET_WIDTH:, :], preferred_element_type=F32)
    o_ref[0] = (mod_ref[0, 2:3, :] * mix).astype(BF16)


def _mixer(mod, f, q, k, v, g, sb, w_out, tables):
    intra, xi_f, xi_b, zeta_f, dec_f = tables
    b, s, _ = q.shape
    tm = min(TM_MIXER, s)
    nc = tm // CHUNK
    assert f.shape[1] == CHUNK and BF16_ROWS % nc == 0 and (s // tm) % (BF16_ROWS // nc) == 0
    tiles = BF16_ROWS // nc
    tok = lambda width: pl.BlockSpec((1, tm, width), lambda bi, ti: (bi, ti, 0))
    const = lambda shape: pl.BlockSpec(shape, lambda bi, ti: (0,) * len(shape))
    tab = const((RET_HEADS, CHUNK, HEAD_DIM))
    states = pl.BlockSpec((1, nc, RET_HEADS, HEAD_DIM, HEAD_DIM), lambda bi, ti: (bi, ti, 0, 0, 0))
    return pl.pallas_call(
        functools.partial(_mixer_kernel, n_chunks=nc, dec_f=dec_f),
        grid=(b, s // tm),
        in_specs=[pl.BlockSpec((1, N_MOD, D_MODEL), lambda bi, ti: (bi, 0, 0)),
                  pl.BlockSpec((1, CHUNK, BF16_ROWS, FNET_WIDTH), lambda bi, ti: (bi, 0, ti // tiles, 0)),
                  tok(RET_WIDTH), tok(RET_WIDTH), tok(RET_WIDTH), tok(RET_WIDTH),
                  states,
                  const((D_MODEL, D_MODEL)), tab, tab, tab, tab],
        out_specs=tok(D_MODEL),
        out_shape=jax.ShapeDtypeStruct((b, s, D_MODEL), BF16),
        scratch_shapes=[pltpu.VMEM((RET_HEADS, HEAD_DIM, HEAD_DIM), F32),
                        pltpu.VMEM((tm, RET_WIDTH), BF16),
                        pltpu.VMEM((1, FNET_GROUPS, CHUNK * BF16_ROWS, GROUP_DIM), F32)],
        compiler_params=_params(("parallel", "arbitrary")),
        name="mixer",
    )(mod, f, q, k, v, g, sb, w_out, intra, xi_f, xi_b, zeta_f)


def _mlp_kernel(x_ref, d_ref, mod_ref, gain_ref, w1_ref, w2_ref, gfin_ref, o_ref):
    x = x_ref[0] + d_ref[0].astype(F32)
    a = gain_ref[...] * (1.0 + mod_ref[0, 4:5, :])
    h = (_rms(x) * a + mod_ref[0, 3:4, :]).astype(BF16)
    out = None
    for part in range(MLP_FF_PARTS):
        cols = slice(part * (D_FF // MLP_FF_PARTS), (part + 1) * (D_FF // MLP_FF_PARTS))
        mid = jnp.dot(h, w1_ref[:, cols], preferred_element_type=F32)
        mid = jnp.square(jnp.maximum(mid, 0.0)).astype(BF16)
        slab = jnp.dot(mid, w2_ref[cols, :], preferred_element_type=F32)
        out = slab if out is None else out + slab
    y = x + mod_ref[0, 5:6, :] * out
    o_ref[0] = _rms(y) * gfin_ref[...]


def _mlp(x, delta, mod, gain, w1, w2, gain_final):
    b, s, _ = x.shape
    tm = min(TM_MLP, s)
    tok = pl.BlockSpec((1, tm, D_MODEL), lambda bi, ti: (bi, ti, 0))
    const = lambda shape: pl.BlockSpec(shape, lambda bi, ti: (0,) * len(shape))
    weight = lambda shape: pl.BlockSpec(shape, lambda bi, ti: (0,) * len(shape), pipeline_mode=pl.Buffered(1))
    return pl.pallas_call(
        _mlp_kernel,
        grid=(b, s // tm),
        in_specs=[tok, tok, pl.BlockSpec((1, N_MOD, D_MODEL), lambda bi, ti: (bi, 0, 0)),
                  const((1, D_MODEL)), weight((D_MODEL, D_FF)), weight((D_FF, D_MODEL)), const((1, D_MODEL))],
        out_specs=tok,
        out_shape=jax.ShapeDtypeStruct((b, s, D_MODEL), F32),
        compiler_params=_params(("parallel", "parallel")),
        name="mlp",
    )(x, delta, mod, gain, w1, w2, gain_final)


def _trunk(x, mod, weights, tables):
    gain_mix, w_in, m_chan, w_out, gain_mlp, w1, w2, gain_final = weights
    intra, xi_f, xi_b, zeta_f, zeta_b, dec_f, dec_b = tables
    u, q, k, v, g, sb = _inproj(x, mod, gain_mix, w_in, zeta_b, dec_b)
    f = _fnet(u, m_chan)
    delta = _mixer(mod, f, q, k, v, g, sb, w_out, (intra, xi_f, xi_b, zeta_f, dec_f))
    return _mlp(x, delta, mod, gain_mlp, w1, w2, gain_final)


def kernel(x_prompt, x_sample, c_prompt, c_sample, ada_w, ada_b, norm_mix, w_in, w_fnet, w_out,
           norm_mlp, w_mlp_in, w_mlp_out, norm_final):
    assert ada_w.shape[0] == 1, "one encoder layer"
    bp, bs = x_prompt.shape[0], x_sample.shape[0]
    c_all = jnp.concatenate([c_prompt, c_sample], axis=0)
    mod = _modulation(c_all, ada_w[0], ada_b[0]).reshape(bp + bs, N_MOD, D_MODEL)
    weights = (norm_mix[0].reshape(1, -1), w_in[0].astype(BF16), _fnet_channel_weights(w_fnet[0]),
               w_out[0].astype(BF16), norm_mlp[0].reshape(1, -1), w_mlp_in[0].astype(BF16),
               w_mlp_out[0].astype(BF16), norm_final.reshape(1, -1))
    tables = _retention_tables()
    y_prompt = _trunk(x_prompt, mod[:bp], weights, tables)
    y_sample = _trunk(x_sample, mod[bp:], weights, tables)
    return (y_prompt, y_sample)
```

```python
import functools

import numpy as np
import jax
import jax.numpy as jnp
from jax import lax
from jax.experimental import pallas as pl
from jax.experimental.pallas import tpu as pltpu

F32 = jnp.float32
BF16 = jnp.bfloat16

D_MODEL = 1024
FNET_WIDTH = 512
FNET_GROUPS = 4
GROUP_DIM = 128
RET_WIDTH = 512
RET_HEADS = 4
HEAD_DIM = 128
D_FF = 4096
CHUNK = 128
ROPE_BASE = 10000.0
EPS = 1e-6
N_MOD = 6
IN_WIDTH = FNET_WIDTH + 4 * RET_WIDTH
DECAY_OFFSET_FWD = 0.0
DECAY_OFFSET_BWD = 0.5

DFT_N1 = 128
MXU_DIM = 256
BF16_ROWS = 16
F32_ROWS = 8
WORD_ROWS = BF16_ROWS // 2
HALF_GROUPS = MXU_DIM // GROUP_DIM
Y_K1 = 72
FNET_UNROLL = 32
STAGE_A_UNROLL = 2
VMEM_LIMIT = 56 * 1024 * 1024

TM_INPROJ = 1024
TM_MIXER = 2048
TM_MLP = 1024
MLP_FF_PARTS = 4


def _params(semantics):
    return pltpu.CompilerParams(dimension_semantics=semantics, vmem_limit_bytes=VMEM_LIMIT)


def _retention_tables():
    scale = HEAD_DIM ** -0.5
    h = np.arange(RET_HEADS, dtype=np.float64)
    lg_f = np.log1p(-np.exp2(-5.0 - DECAY_OFFSET_FWD - h))[:, None, None]
    lg_b = np.log1p(-np.exp2(-5.0 - DECAY_OFFSET_BWD - h))[:, None, None]
    i = np.arange(CHUNK, dtype=np.float64)[None, :, None]
    j = np.arange(CHUNK, dtype=np.float64)[None, None, :]
    ones = np.ones((1, 1, HEAD_DIM))
    intra = scale * np.where(i >= j, np.exp(lg_f * np.maximum(i - j, 0.0)), np.exp(lg_b * np.maximum(j - i, 0.0)))
    xi_f = np.exp(lg_f * (i + 1.0)) * ones
    xi_b = np.exp(lg_b * (CHUNK - i)) * ones
    zeta_f = scale * np.exp(lg_f * (CHUNK - 1.0 - i)) * ones
    zeta_b = scale * np.exp(lg_b * i) * ones
    dec_f = tuple(float(v) for v in np.exp(lg_f[:, 0, 0] * CHUNK))
    dec_b = tuple(float(v) for v in np.exp(lg_b[:, 0, 0] * CHUNK))
    f = lambda a: jnp.asarray(a.astype(np.float32))
    return f(intra), f(xi_f), f(xi_b), f(zeta_f), f(zeta_b), dec_f, dec_b


def _f32(a):
    return jnp.asarray(np.asarray(a).astype(np.float32))


def _rotary_tables(seq, tm):
    half = HEAD_DIM // 2
    inv = ROPE_BASE ** (-np.arange(half, dtype=np.float64) / half)
    inv = np.concatenate([inv, inv])
    sign = np.concatenate([-np.ones(half), np.ones(half)])
    base = np.arange(0, seq, tm, dtype=np.float64)[:, None] * inv[None, :]
    local = np.arange(tm, dtype=np.float64)[:, None] * inv[None, :]
    base_rows = np.stack([np.cos(base), np.sin(base), sign * np.cos(base), sign * np.sin(base)], axis=1)
    return _f32(base_rows), _f32(np.stack([np.cos(local), np.sin(local)], axis=0))


def _dft_stage_a(n1):
    k = np.arange(n1, dtype=np.int64)
    ang = 2.0 * np.pi * ((k[:Y_K1, None] * k[None, :]) % n1) / n1
    base = np.stack([np.cos(ang), -np.sin(ang)], axis=1).reshape(2 * Y_K1, n1) * (n1 ** -0.5)
    even, odd = np.zeros((2 * Y_K1, 2 * n1)), np.zeros((2 * Y_K1, 2 * n1))
    even[:, 0::2] = base
    odd[:, 1::2] = base
    return _f32(even).astype(BF16), _f32(odd).astype(BF16)


def _dft_stage_b(n1, n2):
    reps = MXU_DIM // (2 * n2)
    seq = n1 * n2
    idx = np.arange(MXU_DIM, dtype=np.int64)
    row_c, row_r, row_k = idx // (reps * n2), (idx // n2) % reps, idx % n2
    col_r, col_s, col_c = idx // (2 * n2), (idx // 2) % n2, idx % 2
    alpha = 2.0 * np.pi * ((row_k[:, None] * col_s[None, :]) % n2) / n2
    ca, sa = np.cos(alpha), np.sin(alpha)
    c_out, c_in = row_c[:, None], col_c[None, :]
    p_mat = np.where(c_out == c_in, ca, np.where(c_out < c_in, sa, -sa))
    q_mat = np.where(c_out == c_in, -sa, np.where(c_out < c_in, ca, -ca))
    keep = (row_r[:, None] == col_r[None, :]) * ((n2 * GROUP_DIM) ** -0.5)
    blk = np.arange(n1 // reps, dtype=np.int64)[:, None]
    k1 = blk * reps + col_r[None, :]
    beta = 2.0 * np.pi * ((col_s[None, :] * k1) % seq) / seq
    conj = np.where((k1 >= Y_K1) & (col_c[None, :] == 1), -1.0, 1.0)
    return (_f32(p_mat * keep), _f32(q_mat * keep),
            _f32((np.cos(beta) * conj)[:, None, :]), _f32((np.sin(beta) * conj)[:, None, :]))


def _channel_dft():
    c = np.arange(GROUP_DIM, dtype=np.int64)
    ang = 2.0 * np.pi * ((c[:, None] * c[None, :]) % GROUP_DIM) / GROUP_DIM
    return jnp.asarray(np.concatenate([np.cos(ang), np.sin(ang)], axis=0).astype(np.float32))


def _mod_kernel(c_ref, w_ref, b_ref, o_ref):
    c = c_ref[...]
    s = c * jax.nn.sigmoid(c)
    w = w_ref[...]
    s_hi, w_hi = s.astype(BF16), w.astype(BF16)
    s_lo, w_lo = (s - s_hi.astype(F32)).astype(BF16), (w - w_hi.astype(F32)).astype(BF16)
    dot = functools.partial(jnp.dot, preferred_element_type=F32)
    o_ref[...] = dot(s_hi, w_hi) + (dot(s_hi, w_lo) + dot(s_lo, w_hi)) + b_ref[...]


def _modulation(c, ada_w, ada_b):
    rows = c.shape[0]
    tn = 768
    return pl.pallas_call(
        _mod_kernel,
        grid=(N_MOD * D_MODEL // tn,),
        in_specs=[pl.BlockSpec((rows, D_MODEL), lambda n: (0, 0)),
                  pl.BlockSpec((D_MODEL, tn), lambda n: (0, n)),
                  pl.BlockSpec((1, tn), lambda n: (0, n))],
        out_specs=pl.BlockSpec((rows, tn), lambda n: (0, n)),
        out_shape=jax.ShapeDtypeStruct((rows, N_MOD * D_MODEL), F32),
        compiler_params=_params(("arbitrary",)),
        name="adaln_mod",
    )(c, ada_w, ada_b.reshape(1, -1))


def _fnet_weight_kernel(cs_ref, w_ref, o_ref):
    o_ref[0] = jnp.dot(cs_ref[...], w_ref[0], preferred_element_type=F32,
                       precision=lax.Precision.HIGHEST).astype(BF16)


def _fnet_channel_weights(w_fnet):
    return pl.pallas_call(
        _fnet_weight_kernel,
        grid=(FNET_GROUPS,),
        in_specs=[pl.BlockSpec((2 * GROUP_DIM, GROUP_DIM), lambda g: (0, 0)),
                  pl.BlockSpec((1, GROUP_DIM, GROUP_DIM), lambda g: (g, 0, 0))],
        out_specs=pl.BlockSpec((1, 2 * GROUP_DIM, GROUP_DIM), lambda g: (g, 0, 0)),
        out_shape=jax.ShapeDtypeStruct((FNET_GROUPS, 2 * GROUP_DIM, GROUP_DIM), BF16),
        compiler_params=_params(("arbitrary",)),
        name="fnet_channel_weights",
    )(_channel_dft(), w_fnet)


def _rms(x):
    return x * lax.rsqrt(jnp.mean(x * x, axis=-1, keepdims=True) + EPS)


def _kv_outer(k, vz):
    return lax.dot_general(k, vz, (((0,), (0,)), ((), ())), preferred_element_type=F32)


def _inproj_kernel(x_ref, mod_ref, gain_ref, w_ref, rbase_ref, rlocal_ref, zeta_ref,
                   u_ref, q_ref, k_ref, v_ref, g_ref, sb_ref, state_ref, *, n_chunks, dec_b):
    @pl.when(pl.program_id(1) == 0)
    def _():
        state_ref[...] = jnp.zeros_like(state_ref)

    x = x_ref[0]
    a = gain_ref[...] * (1.0 + mod_ref[0, 1:2, :])
    h = (_rms(x) * a + mod_ref[0, 0:1, :]).astype(BF16)
    proj = jnp.dot(h, w_ref[...], preferred_element_type=F32)
    u_ref[0] = proj[:, :FNET_WIDTH].astype(BF16)
    cos_l, sin_l = rlocal_ref[0], rlocal_ref[1]
    cos = rbase_ref[0, 0:1, :] * cos_l - rbase_ref[0, 1:2, :] * sin_l
    sin = rbase_ref[0, 3:4, :] * cos_l + rbase_ref[0, 2:3, :] * sin_l
    for hd in range(RET_HEADS):
        lo = hd * HEAD_DIM
        for off, ref in ((FNET_WIDTH, q_ref), (FNET_WIDTH + RET_WIDTH, k_ref)):
            t = proj[:, off + lo:off + lo + HEAD_DIM]
            ref[0, :, lo:lo + HEAD_DIM] = (t * cos + pltpu.roll(t, HEAD_DIM // 2, 1) * sin).astype(BF16)
    v_ref[0] = proj[:, FNET_WIDTH + 2 * RET_WIDTH:FNET_WIDTH + 3 * RET_WIDTH].astype(BF16)
    g_ref[0] = proj[:, FNET_WIDTH + 3 * RET_WIDTH:].astype(BF16)

    for c in reversed(range(n_chunks)):
        rows = pl.ds(c * CHUNK, CHUNK)
        for hd in range(RET_HEADS):
            lanes = slice(hd * HEAD_DIM, (hd + 1) * HEAD_DIM)
            sb_ref[0, c, hd] = state_ref[hd].astype(BF16)
            vz = (v_ref[0, rows, lanes].astype(F32) * zeta_ref[hd]).astype(BF16)
            state_ref[hd] = state_ref[hd] * dec_b[hd] + _kv_outer(k_ref[0, rows, lanes], vz)


def _inproj(x, mod, gain, w_in, zeta_b, dec_b):
    b, s, _ = x.shape
    tm = min(TM_INPROJ, s)
    nt = s // tm
    nc = tm // CHUNK
    rot_base, rot_local = _rotary_tables(s, tm)
    rev = lambda bi, ti: (bi, nt - 1 - ti, 0)
    tok = lambda width: pl.BlockSpec((1, tm, width), rev)
    const = lambda shape: pl.BlockSpec(shape, lambda bi, ti: (0,) * len(shape))
    act = jax.ShapeDtypeStruct((b, s, RET_WIDTH), BF16)
    states = pl.BlockSpec((1, nc, RET_HEADS, HEAD_DIM, HEAD_DIM), lambda bi, ti: (bi, nt - 1 - ti, 0, 0, 0))
    state_shape = jax.ShapeDtypeStruct((b, s // CHUNK, RET_HEADS, HEAD_DIM, HEAD_DIM), BF16)
    return pl.pallas_call(
        functools.partial(_inproj_kernel, n_chunks=nc, dec_b=dec_b),
        grid=(b, nt),
        in_specs=[tok(D_MODEL),
                  pl.BlockSpec((1, N_MOD, D_MODEL), lambda bi, ti: (bi, 0, 0)),
                  const((1, D_MODEL)),
                  const((D_MODEL, IN_WIDTH)),
                  pl.BlockSpec((1, 4, HEAD_DIM), lambda bi, ti: (nt - 1 - ti, 0, 0)),
                  const((2, tm, HEAD_DIM)),
                  const((RET_HEADS, CHUNK, HEAD_DIM))],
        out_specs=[tok(FNET_WIDTH), tok(RET_WIDTH), tok(RET_WIDTH), tok(RET_WIDTH), tok(RET_WIDTH),
                   states],
        out_shape=[jax.ShapeDtypeStruct((b, s, FNET_WIDTH), BF16), act, act, act, act,
                   state_shape],
        scratch_shapes=[pltpu.VMEM((RET_HEADS, HEAD_DIM, HEAD_DIM), F32)],
        compiler_params=_params(("parallel", "arbitrary")),
        name="inproj",
    )(x, mod, gain, w_in, rot_base, rot_local, zeta_b)


def _group_lanes():
    return [(g, slice(g * GROUP_DIM, (g + 1) * GROUP_DIM)) for g in range(FNET_GROUPS)]


def _unpack_rows(load, dst_ref, slab=0):
    units = dst_ref.shape[2] // BF16_ROWS
    half = units * F32_ROWS
    for g, lanes in _group_lanes():
        rows = load(lanes).reshape(units, BF16_ROWS, GROUP_DIM).astype(F32)
        dst_ref[slab, g, 0:half, :] = rows[:, :F32_ROWS, :].reshape(half, GROUP_DIM)
        dst_ref[slab, g, half:, :] = rows[:, F32_ROWS:, :].reshape(half, GROUP_DIM)


def _strided_rows(src_ref, row, slab=0):
    units = src_ref.shape[2] // BF16_ROWS
    start = (row // F32_ROWS) * (units * F32_ROWS) + row % F32_ROWS
    rows = pl.ds(start, units, stride=F32_ROWS)
    return jnp.concatenate([src_ref[slab, g, rows, :] for g in range(FNET_GROUPS)], axis=1).astype(BF16)


def _word_rows(src_ref, lead, word, units):
    rows = pl.ds(word, units, stride=WORD_ROWS)
    return jnp.concatenate([pltpu.bitcast(src_ref[lead, g, rows, :], BF16) for g in range(HALF_GROUPS)], axis=1)


def _fnet_kernel(x_ref, even_ref, odd_ref, p_ref, q_ref, cb_ref, sb_ref, m_ref, o_ref, xw_ref, yw_ref, *, n2, reps):
    n1 = DFT_N1
    half = MXU_DIM // 2
    groups = [slice(g * GROUP_DIM, (g + 1) * GROUP_DIM) for g in range(HALF_GROUPS)]

    def aligned(start, step):
        return start * step if isinstance(start, int) else pl.multiple_of(start * step, step)

    def stage_a(sl, slot):
        s2_base = aligned(sl, BF16_ROWS)
        for g, lanes in enumerate(groups):
            block = x_ref[0, :, pl.ds(s2_base, BF16_ROWS), lanes]
            xw_ref[slot, g] = pltpu.bitcast(block.reshape(n1 * BF16_ROWS, GROUP_DIM), jnp.uint32)
        for i in range(WORD_ROWS):
            pair = _word_rows(xw_ref, slot, i, n1)
            for parity, mat_ref in enumerate((even_ref, odd_ref)):
                y = jnp.dot(mat_ref[...], pair, preferred_element_type=F32).astype(BF16)
                words = pltpu.bitcast(y, jnp.uint32)
                row0 = aligned(s2_base + 2 * i + parity, WORD_ROWS)
                for ks in range(Y_K1 // WORD_ROWS):
                    for g, lanes in enumerate(groups):
                        yw_ref[ks, g, pl.ds(row0, WORD_ROWS), :] = words[ks * WORD_ROWS:(ks + 1) * WORD_ROWS, lanes]

    def stage_b(blk, slot):
        t = (p_ref[...] * cb_ref[blk] + q_ref[...] * sb_ref[blk]).astype(BF16)
        pieces = []
        for r in range(reps):
            k1 = blk * reps + r
            src = (n1 - k1 if k1 >= Y_K1 else k1) if isinstance(k1, int) else jnp.where(k1 >= Y_K1, n1 - k1, k1)
            pieces.append(_word_rows(yw_ref, src // WORD_ROWS, src % WORD_ROWS, n2))
        yb = pieces[0] if reps == 1 else jnp.concatenate(pieces, axis=0)
        z = jnp.dot(t, yb, preferred_element_type=F32)
        for g, lanes in enumerate(groups):
            zz = jnp.concatenate([z[:half, lanes], z[half:, lanes]], axis=1).astype(BF16)
            res = jnp.dot(zz, m_ref[g], preferred_element_type=F32)
            for r in range(reps):
                o_ref[0, blk * reps + r, :, lanes] = res[r * n2:(r + 1) * n2].astype(BF16)

    _for_each(n2 // BF16_ROWS, stage_a, STAGE_A_UNROLL)
    _for_each(n1 // reps, stage_b, FNET_UNROLL)


def _for_each(count, body, group):
    if count <= group:
        for i in range(count):
            body(i, i)
        return

    def step(j, carry):
        for u in range(group):
            body(j * group + u, u)
        return carry

    lax.fori_loop(0, count // group, step, 0)


def _fnet(u, m_chan):
    b, s, _ = u.shape
    n1 = DFT_N1
    n2 = s // n1
    assert s == n1 * n2 and MXU_DIM % (2 * n2) == 0 and n2 % BF16_ROWS == 0, s
    reps = MXU_DIM // (2 * n2)
    p_mat, q_mat, cos_b, sin_b = _dft_stage_b(n1, n2)
    const = lambda shape: pl.BlockSpec(shape, lambda bi, hi: (0,) * len(shape))
    tokens = pl.BlockSpec((1, n1, n2, MXU_DIM), lambda bi, hi: (bi, 0, 0, hi))
    return pl.pallas_call(
        functools.partial(_fnet_kernel, n2=n2, reps=reps),
        grid=(b, FNET_WIDTH // MXU_DIM),
        in_specs=[tokens, const((2 * Y_K1, 2 * n1)), const((2 * Y_K1, 2 * n1)),
                  const((MXU_DIM, MXU_DIM)), const((MXU_DIM, MXU_DIM)),
                  const((n1 // reps, 1, MXU_DIM)), const((n1 // reps, 1, MXU_DIM)),
                  pl.BlockSpec((HALF_GROUPS, 2 * GROUP_DIM, GROUP_DIM), lambda bi, hi: (hi, 0, 0))],
        out_specs=tokens,
        out_shape=jax.ShapeDtypeStruct((b, n1, n2, FNET_WIDTH), BF16),
        scratch_shapes=[pltpu.VMEM((STAGE_A_UNROLL, HALF_GROUPS, n1 * WORD_ROWS, GROUP_DIM), jnp.uint32),
                        pltpu.VMEM((Y_K1 // WORD_ROWS, HALF_GROUPS, n2 * WORD_ROWS, GROUP_DIM), jnp.uint32)],
        compiler_params=_params(("parallel", "parallel")),
        name="fnet",
    )(u.reshape(b, n1, n2, FNET_WIDTH), *_dft_stage_a(n1), p_mat, q_mat, cos_b, sin_b, m_chan)


def _mixer_kernel(mod_ref, f_ref, q_ref, k_ref, v_ref, g_ref, sb_ref, w_ref,
                  intra_ref, xif_ref, xib_ref, zeta_ref, o_ref, state_ref, r_ref, fs_ref, *, n_chunks, dec_f):
    @pl.when(pl.program_id(1) == 0)
    def _():
        state_ref[...] = jnp.zeros_like(state_ref)

    tiles = BF16_ROWS // n_chunks
    sub = pl.program_id(1) % tiles

    @pl.when(sub == 0)
    def _():
        _unpack_rows(lambda lanes: f_ref[0, :, :, lanes], fs_ref)

    f_rows = [_strided_rows(fs_ref, sub * n_chunks + c) for c in range(n_chunks)]
    mix = jnp.dot(jnp.concatenate(f_rows, axis=0), w_ref[:FNET_WIDTH, :], preferred_element_type=F32)

    for c in range(n_chunks):
        rows = pl.ds(c * CHUNK, CHUNK)
        for hd in range(RET_HEADS):
            lanes = slice(hd * HEAD_DIM, (hd + 1) * HEAD_DIM)
            q = q_ref[0, rows, lanes]
            k = k_ref[0, rows, lanes]
            v = v_ref[0, rows, lanes]
            qf = q.astype(F32)
            scores = lax.dot_general(q, k, (((1,), (1,)), ((), ())), preferred_element_type=F32)
            r = jnp.dot((scores * intra_ref[hd]).astype(BF16), v, preferred_element_type=F32)
            r += jnp.dot((qf * xif_ref[hd]).astype(BF16), state_ref[hd].astype(BF16),
                         preferred_element_type=F32)
            r += jnp.dot((qf * xib_ref[hd]).astype(BF16), sb_ref[0, c, hd], preferred_element_type=F32)
            vz = (v.astype(F32) * zeta_ref[hd]).astype(BF16)
            state_ref[hd] = state_ref[hd] * dec_f[hd] + _kv_outer(k, vz)
            gate = g_ref[0, rows, lanes].astype(F32)
            r_ref[rows, lanes] = (_rms(r) * (gate * jax.nn.sigmoid(gate))).astype(BF16)

    mix += jnp.dot(r_ref[...], w_ref[FNET_WIDTH:, :], preferred_element_type=F32)
    o_ref[0] = (mod_ref[0, 2:3, :] * mix).astype(BF16)


def _mixer(mod, f, q, k, v, g, sb, w_out, tables):
    intra, xi_f, xi_b, zeta_f, dec_f = tables
    b, s, _ = q.shape
    tm = min(TM_MIXER, s // 2)
    nc = tm // CHUNK
    assert f.shape[1] == CHUNK and BF16_ROWS % nc == 0 and (s // tm) % (BF16_ROWS // nc) == 0
    tiles = BF16_ROWS // nc
    tok = lambda width: pl.BlockSpec((1, tm, width), lambda bi, ti: (bi, ti, 0))
    const = lambda shape: pl.BlockSpec(shape, lambda bi, ti: (0,) * len(shape))
    tab = const((RET_HEADS, CHUNK, HEAD_DIM))
    states = pl.BlockSpec((1, nc, RET_HEADS, HEAD_DIM, HEAD_DIM), lambda bi, ti: (bi, ti, 0, 0, 0))
    return pl.pallas_call(
        functools.partial(_mixer_kernel, n_chunks=nc, dec_f=dec_f),
        grid=(b, s // tm),
        in_specs=[pl.BlockSpec((1, N_MOD, D_MODEL), lambda bi, ti: (bi, 0, 0)),
                  pl.BlockSpec((1, CHUNK, BF16_ROWS, FNET_WIDTH), lambda bi, ti: (bi, 0, ti // tiles, 0)),
                  tok(RET_WIDTH), tok(RET_WIDTH), tok(RET_WIDTH), tok(RET_WIDTH),
                  states,
                  const((D_MODEL, D_MODEL)), tab, tab, tab, tab],
        out_specs=tok(D_MODEL),
        out_shape=jax.ShapeDtypeStruct((b, s, D_MODEL), BF16),
        scratch_shapes=[pltpu.VMEM((RET_HEADS, HEAD_DIM, HEAD_DIM), F32),
                        pltpu.VMEM((tm, RET_WIDTH), BF16),
                        pltpu.VMEM((1, FNET_GROUPS, CHUNK * BF16_ROWS, GROUP_DIM), F32)],
        compiler_params=_params(("parallel", "arbitrary")),
        name="mixer",
    )(mod, f, q, k, v, g, sb, w_out, intra, xi_f, xi_b, zeta_f)


def _mlp_kernel(x_ref, d_ref, mod_ref, gain_ref, w1_ref, w2_ref, gfin_ref, o_ref):
    x = x_ref[0] + d_ref[0].astype(F32)
    a = gain_ref[...] * (1.0 + mod_ref[0, 4:5, :])
    h = (_rms(x) * a + mod_ref[0, 3:4, :]).astype(BF16)
    out = None
    for part in range(MLP_FF_PARTS):
        cols = slice(part * (D_FF // MLP_FF_PARTS), (part + 1) * (D_FF // MLP_FF_PARTS))
        mid = jnp.dot(h, w1_ref[:, cols], preferred_element_type=F32)
        mid = jnp.square(jnp.maximum(mid, 0.0)).astype(BF16)
        slab = jnp.dot(mid, w2_ref[cols, :], preferred_element_type=F32)
        out = slab if out is None else out + slab
    y = x + mod_ref[0, 5:6, :] * out
    o_ref[0] = _rms(y) * gfin_ref[...]


def _mlp(x, delta, mod, gain, w1, w2, gain_final):
    b, s, _ = x.shape
    tm = min(TM_MLP, s)
    tok = pl.BlockSpec((1, tm, D_MODEL), lambda bi, ti: (bi, ti, 0))
    const = lambda shape: pl.BlockSpec(shape, lambda bi, ti: (0,) * len(shape))
    weight = lambda shape: pl.BlockSpec(shape, lambda bi, ti: (0,) * len(shape), pipeline_mode=pl.Buffered(1))
    return pl.pallas_call(
        _mlp_kernel,
        grid=(b, s // tm),
        in_specs=[tok, tok, pl.BlockSpec((1, N_MOD, D_MODEL), lambda bi, ti: (bi, 0, 0)),
                  const((1, D_MODEL)), weight((D_MODEL, D_FF)), weight((D_FF, D_MODEL)), const((1, D_MODEL))],
        out_specs=tok,
        out_shape=jax.ShapeDtypeStruct((b, s, D_MODEL), F32),
        compiler_params=_params(("parallel", "parallel")),
        name="mlp",
    )(x, delta, mod, gain, w1, w2, gain_final)


def _trunk(x, mod, weights, tables):
    gain_mix, w_in, m_chan, w_out, gain_mlp, w1, w2, gain_final = weights
    intra, xi_f, xi_b, zeta_f, zeta_b, dec_f, dec_b = tables
    u, q, k, v, g, sb = _inproj(x, mod, gain_mix, w_in, zeta_b, dec_b)
    f = _fnet(u, m_chan)
    delta = _mixer(mod, f, q, k, v, g, sb, w_out, (intra, xi_f, xi_b, zeta_f, dec_f))
    return _mlp(x, delta, mod, gain_mlp, w1, w2, gain_final)


def kernel(x_prompt, x_sample, c_prompt, c_sample, ada_w, ada_b, norm_mix, w_in, w_fnet, w_out,
           norm_mlp, w_mlp_in, w_mlp_out, norm_final):
    assert ada_w.shape[0] == 1, "one encoder layer"
    bp, bs = x_prompt.shape[0], x_sample.shape[0]
    c_all = jnp.concatenate([c_prompt, c_sample], axis=0)
    mod = _modulation(c_all, ada_w[0], ada_b[0]).reshape(bp + bs, N_MOD, D_MODEL)
    weights = (norm_mix[0].reshape(1, -1), w_in[0].astype(BF16), _fnet_channel_weights(w_fnet[0]),
               w_out[0].astype(BF16), norm_mlp[0].reshape(1, -1), w_mlp_in[0].astype(BF16),
               w_mlp_out[0].astype(BF16), norm_final.reshape(1, -1))
    tables = _retention_tables()
    y_prompt = _trunk(x_prompt, mod[:bp], weights, tables)
    y_sample = _trunk(x_sample, mod[bp:], weights, tables)
    return (y_prompt, y_sample)
```

```python
import functools

import numpy as np
import jax
import jax.numpy as jnp
from jax import lax
from jax.experimental import pallas as pl
from jax.experimental.pallas import tpu as pltpu

F32 = jnp.float32
BF16 = jnp.bfloat16

D_MODEL = 1024
FNET_WIDTH = 512
FNET_GROUPS = 4
GROUP_DIM = 128
RET_WIDTH = 512
RET_HEADS = 4
HEAD_DIM = 128
D_FF = 4096
CHUNK = 128
ROPE_BASE = 10000.0
EPS = 1e-6
N_MOD = 6
IN_WIDTH = FNET_WIDTH + 4 * RET_WIDTH
DECAY_OFFSET_FWD = 0.0
DECAY_OFFSET_BWD = 0.5

DFT_N1 = 128
MXU_DIM = 256
BF16_ROWS = 16
F32_ROWS = 8
WORD_ROWS = BF16_ROWS // 2
HALF_GROUPS = MXU_DIM // GROUP_DIM
Y_K1 = 72
FNET_UNROLL = 32
STAGE_A_UNROLL = 2
VMEM_LIMIT = 56 * 1024 * 1024

TM_INPROJ = 1024
TM_MIXER = 2048
TM_MLP = 1024
MLP_FF_PARTS = 4


def _params(semantics):
    return pltpu.CompilerParams(dimension_semantics=semantics, vmem_limit_bytes=VMEM_LIMIT)


def _retention_tables():
    scale = HEAD_DIM ** -0.5
    h = np.arange(RET_HEADS, dtype=np.float64)
    lg_f = np.log1p(-np.exp2(-5.0 - DECAY_OFFSET_FWD - h))[:, None, None]
    lg_b = np.log1p(-np.exp2(-5.0 - DECAY_OFFSET_BWD - h))[:, None, None]
    i = np.arange(CHUNK, dtype=np.float64)[None, :, None]
    j = np.arange(CHUNK, dtype=np.float64)[None, None, :]
    ones = np.ones((1, 1, HEAD_DIM))
    intra = scale * np.where(i >= j, np.exp(lg_f * np.maximum(i - j, 0.0)), np.exp(lg_b * np.maximum(j - i, 0.0)))
    xi_f = np.exp(lg_f * (i + 1.0)) * ones
    xi_b = np.exp(lg_b * (CHUNK - i)) * ones
    zeta_f = scale * np.exp(lg_f * (CHUNK - 1.0 - i)) * ones
    zeta_b = scale * np.exp(lg_b * i) * ones
    dec_f = tuple(float(v) for v in np.exp(lg_f[:, 0, 0] * CHUNK))
    dec_b = tuple(float(v) for v in np.exp(lg_b[:, 0, 0] * CHUNK))
    f = lambda a: jnp.asarray(a.astype(np.float32))
    return f(intra), f(xi_f), f(xi_b), f(zeta_f), f(zeta_b), dec_f, dec_b


def _f32(a):
    return jnp.asarray(np.asarray(a).astype(np.float32))


def _rotary_tables(seq, tm):
    half = HEAD_DIM // 2
    inv = ROPE_BASE ** (-np.arange(half, dtype=np.float64) / half)
    inv = np.concatenate([inv, inv])
    sign = np.concatenate([-np.ones(half), np.ones(half)])
    base = np.arange(0, seq, tm, dtype=np.float64)[:, None] * inv[None, :]
    local = np.arange(tm, dtype=np.float64)[:, None] * inv[None, :]
    base_rows = np.stack([np.cos(base), np.sin(base), sign * np.cos(base), sign * np.sin(base)], axis=1)
    return _f32(base_rows), _f32(np.stack([np.cos(local), np.sin(local)], axis=0))


def _dft_stage_a(n1):
    k = np.arange(n1, dtype=np.int64)
    ang = 2.0 * np.pi * ((k[:Y_K1, None] * k[None, :]) % n1) / n1
    base = np.stack([np.cos(ang), -np.sin(ang)], axis=1).reshape(2 * Y_K1, n1) * (n1 ** -0.5)
    even, odd = np.zeros((2 * Y_K1, 2 * n1)), np.zeros((2 * Y_K1, 2 * n1))
    even[:, 0::2] = base
    odd[:, 1::2] = base
    return _f32(even).astype(BF16), _f32(odd).astype(BF16)


def _dft_stage_b(n1, n2):
    reps = MXU_DIM // (2 * n2)
    seq = n1 * n2
    idx = np.arange(MXU_DIM, dtype=np.int64)
    row_c, row_r, row_k = idx // (reps * n2), (idx // n2) % reps, idx % n2
    col_r, col_s, col_c = idx // (2 * n2), (idx // 2) % n2, idx % 2
    alpha = 2.0 * np.pi * ((row_k[:, None] * col_s[None, :]) % n2) / n2
    ca, sa = np.cos(alpha), np.sin(alpha)
    c_out, c_in = row_c[:, None], col_c[None, :]
    p_mat = np.where(c_out == c_in, ca, np.where(c_out < c_in, sa, -sa))
    q_mat = np.where(c_out == c_in, -sa, np.where(c_out < c_in, ca, -ca))
    keep = (row_r[:, None] == col_r[None, :]) * ((n2 * GROUP_DIM) ** -0.5)
    blk = np.arange(n1 // reps, dtype=np.int64)[:, None]
    k1 = blk * reps + col_r[None, :]
    beta = 2.0 * np.pi * ((col_s[None, :] * k1) % seq) / seq
    conj = np.where((k1 >= Y_K1) & (col_c[None, :] == 1), -1.0, 1.0)
    return (_f32(p_mat * keep), _f32(q_mat * keep),
            _f32((np.cos(beta) * conj)[:, None, :]), _f32((np.sin(beta) * conj)[:, None, :]))


def _channel_dft():
    c = np.arange(GROUP_DIM, dtype=np.int64)
    ang = 2.0 * np.pi * ((c[:, None] * c[None, :]) % GROUP_DIM) / GROUP_DIM
    return jnp.asarray(np.concatenate([np.cos(ang), np.sin(ang)], axis=0).astype(np.float32))


def _mod_kernel(c_ref, w_ref, b_ref, o_ref):
    @pl.when(pl.program_id(0) == 0)
    def _():
        o_ref[...] = jnp.broadcast_to(b_ref[...], o_ref.shape)

    c = c_ref[...]
    s = c * jax.nn.sigmoid(c)
    w = w_ref[...]
    s_hi, w_hi = s.astype(BF16), w.astype(BF16)
    s_lo, w_lo = (s - s_hi.astype(F32)).astype(BF16), (w - w_hi.astype(F32)).astype(BF16)
    dot = functools.partial(jnp.dot, preferred_element_type=F32)
    o_ref[...] += dot(s_hi, w_hi) + (dot(s_hi, w_lo) + dot(s_lo, w_hi))


def _modulation(c, ada_w, ada_b):
    rows = c.shape[0]
    tk = 128
    width = N_MOD * D_MODEL
    return pl.pallas_call(
        _mod_kernel,
        grid=(D_MODEL // tk,),
        in_specs=[pl.BlockSpec((rows, tk), lambda k: (0, k)),
                  pl.BlockSpec((tk, width), lambda k: (k, 0)),
                  pl.BlockSpec((1, width), lambda k: (0, 0))],
        out_specs=pl.BlockSpec((rows, width), lambda k: (0, 0)),
        out_shape=jax.ShapeDtypeStruct((rows, width), F32),
        compiler_params=_params(("arbitrary",)),
        name="adaln_mod",
    )(c, ada_w, ada_b.reshape(1, -1))


def _fnet_weight_kernel(cs_ref, w_ref, o_ref):
    o_ref[0] = jnp.dot(cs_ref[...], w_ref[0], preferred_element_type=F32,
                       precision=lax.Precision.HIGHEST).astype(BF16)


def _fnet_channel_weights(w_fnet):
    return pl.pallas_call(
        _fnet_weight_kernel,
        grid=(FNET_GROUPS,),
        in_specs=[pl.BlockSpec((2 * GROUP_DIM, GROUP_DIM), lambda g: (0, 0)),
                  pl.BlockSpec((1, GROUP_DIM, GROUP_DIM), lambda g: (g, 0, 0))],
        out_specs=pl.BlockSpec((1, 2 * GROUP_DIM, GROUP_DIM), lambda g: (g, 0, 0)),
        out_shape=jax.ShapeDtypeStruct((FNET_GROUPS, 2 * GROUP_DIM, GROUP_DIM), BF16),
        compiler_params=_params(("arbitrary",)),
        name="fnet_channel_weights",
    )(_channel_dft(), w_fnet)


def _rms(x):
    return x * lax.rsqrt(jnp.mean(x * x, axis=-1, keepdims=True) + EPS)


def _kv_outer(k, vz):
    return lax.dot_general(k, vz, (((0,), (0,)), ((), ())), preferred_element_type=F32)


def _inproj_kernel(x_ref, mod_ref, gain_ref, w_ref, rbase_ref, rlocal_ref, zeta_ref,
                   u_ref, q_ref, k_ref, v_ref, g_ref, sb_ref, state_ref, *, n_chunks, dec_b):
    @pl.when(pl.program_id(1) == 0)
    def _():
        state_ref[...] = jnp.zeros_like(state_ref)

    x = x_ref[0]
    a = gain_ref[...] * (1.0 + mod_ref[0, 1:2, :])
    h = (_rms(x) * a + mod_ref[0, 0:1, :]).astype(BF16)
    proj = jnp.dot(h, w_ref[...], preferred_element_type=F32)
    u_ref[0] = proj[:, :FNET_WIDTH].astype(BF16)
    cos_l, sin_l = rlocal_ref[0], rlocal_ref[1]
    cos = rbase_ref[0, 0:1, :] * cos_l - rbase_ref[0, 1:2, :] * sin_l
    sin = rbase_ref[0, 3:4, :] * cos_l + rbase_ref[0, 2:3, :] * sin_l
    for hd in range(RET_HEADS):
        lo = hd * HEAD_DIM
        for off, ref in ((FNET_WIDTH, q_ref), (FNET_WIDTH + RET_WIDTH, k_ref)):
            t = proj[:, off + lo:off + lo + HEAD_DIM]
            ref[0, :, lo:lo + HEAD_DIM] = (t * cos + pltpu.roll(t, HEAD_DIM // 2, 1) * sin).astype(BF16)
    v_ref[0] = proj[:, FNET_WIDTH + 2 * RET_WIDTH:FNET_WIDTH + 3 * RET_WIDTH].astype(BF16)
    g_ref[0] = proj[:, FNET_WIDTH + 3 * RET_WIDTH:].astype(BF16)

    for c in reversed(range(n_chunks)):
        rows = pl.ds(c * CHUNK, CHUNK)
        for hd in range(RET_HEADS):
            lanes = slice(hd * HEAD_DIM, (hd + 1) * HEAD_DIM)
            sb_ref[0, c, hd] = state_ref[hd].astype(BF16)
            vz = (v_ref[0, rows, lanes].astype(F32) * zeta_ref[hd]).astype(BF16)
            state_ref[hd] = state_ref[hd] * dec_b[hd] + _kv_outer(k_ref[0, rows, lanes], vz)


def _inproj(x, mod, gain, w_in, zeta_b, dec_b):
    b, s, _ = x.shape
    tm = min(TM_INPROJ, s)
    nt = s // tm
    nc = tm // CHUNK
    rot_base, rot_local = _rotary_tables(s, tm)
    rev = lambda bi, ti: (bi, nt - 1 - ti, 0)
    tok = lambda width: pl.BlockSpec((1, tm, width), rev)
    const = lambda shape: pl.BlockSpec(shape, lambda bi, ti: (0,) * len(shape))
    act = jax.ShapeDtypeStruct((b, s, RET_WIDTH), BF16)
    states = pl.BlockSpec((1, nc, RET_HEADS, HEAD_DIM, HEAD_DIM), lambda bi, ti: (bi, nt - 1 - ti, 0, 0, 0))
    state_shape = jax.ShapeDtypeStruct((b, s // CHUNK, RET_HEADS, HEAD_DIM, HEAD_DIM), BF16)
    return pl.pallas_call(
        functools.partial(_inproj_kernel, n_chunks=nc, dec_b=dec_b),
        grid=(b, nt),
        in_specs=[tok(D_MODEL),
                  pl.BlockSpec((1, N_MOD, D_MODEL), lambda bi, ti: (bi, 0, 0)),
                  const((1, D_MODEL)),
                  const((D_MODEL, IN_WIDTH)),
                  pl.BlockSpec((1, 4, HEAD_DIM), lambda bi, ti: (nt - 1 - ti, 0, 0)),
                  const((2, tm, HEAD_DIM)),
                  const((RET_HEADS, CHUNK, HEAD_DIM))],
        out_specs=[tok(FNET_WIDTH), tok(RET_WIDTH), tok(RET_WIDTH), tok(RET_WIDTH), tok(RET_WIDTH),
                   states],
        out_shape=[jax.ShapeDtypeStruct((b, s, FNET_WIDTH), BF16), act, act, act, act,
                   state_shape],
        scratch_shapes=[pltpu.VMEM((RET_HEADS, HEAD_DIM, HEAD_DIM), F32)],
        compiler_params=_params(("parallel", "arbitrary")),
        name="inproj",
    )(x, mod, gain, w_in, rot_base, rot_local, zeta_b)


def _group_lanes():
    return [(g, slice(g * GROUP_DIM, (g + 1) * GROUP_DIM)) for g in range(FNET_GROUPS)]


def _unpack_rows(load, dst_ref, slab=0):
    units = dst_ref.shape[2] // BF16_ROWS
    half = units * F32_ROWS
    for g, lanes in _group_lanes():
        rows = load(lanes).reshape(units, BF16_ROWS, GROUP_DIM).astype(F32)
        dst_ref[slab, g, 0:half, :] = rows[:, :F32_ROWS, :].reshape(half, GROUP_DIM)
        dst_ref[slab, g, half:, :] = rows[:, F32_ROWS:, :].reshape(half, GROUP_DIM)


def _strided_rows(src_ref, row, slab=0):
    units = src_ref.shape[2] // BF16_ROWS
    start = (row // F32_ROWS) * (units * F32_ROWS) + row % F32_ROWS
    rows = pl.ds(start, units, stride=F32_ROWS)
    return jnp.concatenate([src_ref[slab, g, rows, :] for g in range(FNET_GROUPS)], axis=1).astype(BF16)


def _word_rows(src_ref, lead, word, units):
    rows = pl.ds(word, units, stride=WORD_ROWS)
    return jnp.concatenate([pltpu.bitcast(src_ref[lead, g, rows, :], BF16) for g in range(HALF_GROUPS)], axis=1)


def _fnet_kernel(x_ref, even_ref, odd_ref, p_ref, q_ref, cb_ref, sb_ref, m_ref, o_ref, xw_ref, yw_ref, *, n2, reps):
    n1 = DFT_N1
    half = MXU_DIM // 2
    groups = [slice(g * GROUP_DIM, (g + 1) * GROUP_DIM) for g in range(HALF_GROUPS)]

    def aligned(start, step):
        return start * step if isinstance(start, int) else pl.multiple_of(start * step, step)

    def stage_a(sl, slot):
        s2_base = aligned(sl, BF16_ROWS)
        for g, lanes in enumerate(groups):
            block = x_ref[0, :, pl.ds(s2_base, BF16_ROWS), lanes]
            xw_ref[slot, g] = pltpu.bitcast(block.reshape(n1 * BF16_ROWS, GROUP_DIM), jnp.uint32)
        for i in range(WORD_ROWS):
            pair = _word_rows(xw_ref, slot, i, n1)
            for parity, mat_ref in enumerate((even_ref, odd_ref)):
                y = jnp.dot(mat_ref[...], pair, preferred_element_type=F32).astype(BF16)
                words = pltpu.bitcast(y, jnp.uint32)
                row0 = aligned(s2_base + 2 * i + parity, WORD_ROWS)
                for ks in range(Y_K1 // WORD_ROWS):
                    for g, lanes in enumerate(groups):
                        yw_ref[ks, g, pl.ds(row0, WORD_ROWS), :] = words[ks * WORD_ROWS:(ks + 1) * WORD_ROWS, lanes]

    def stage_b(blk, slot):
        t = (p_ref[...] * cb_ref[blk] + q_ref[...] * sb_ref[blk]).astype(BF16)
        pieces = []
        for r in range(reps):
            k1 = blk * reps + r
            src = (n1 - k1 if k1 >= Y_K1 else k1) if isinstance(k1, int) else jnp.where(k1 >= Y_K1, n1 - k1, k1)
            pieces.append(_word_rows(yw_ref, src // WORD_ROWS, src % WORD_ROWS, n2))
        yb = pieces[0] if reps == 1 else jnp.concatenate(pieces, axis=0)
        z = jnp.dot(t, yb, preferred_element_type=F32)
        for g, lanes in enumerate(groups):
            zz = jnp.concatenate([z[:half, lanes], z[half:, lanes]], axis=1).astype(BF16)
            res = jnp.dot(zz, m_ref[g], preferred_element_type=F32)
            for r in range(reps):
                o_ref[0, blk * reps + r, :, lanes] = res[r * n2:(r + 1) * n2].astype(BF16)

    _for_each(n2 // BF16_ROWS, stage_a, STAGE_A_UNROLL)
    _for_each(n1 // reps, stage_b, FNET_UNROLL)


def _for_each(count, body, group):
    if count <= group:
        for i in range(count):
            body(i, i)
        return

    def step(j, carry):
        for u in range(group):
            body(j * group + u, u)
        return carry

    lax.fori_loop(0, count // group, step, 0)


def _fnet(u, m_chan):
    b, s, _ = u.shape
    n1 = DFT_N1
    n2 = s // n1
    assert s == n1 * n2 and MXU_DIM % (2 * n2) == 0 and n2 % BF16_ROWS == 0, s
    reps = MXU_DIM // (2 * n2)
    p_mat, q_mat, cos_b, sin_b = _dft_stage_b(n1, n2)
    const = lambda shape: pl.BlockSpec(shape, lambda bi, hi: (0,) * len(shape))
    tokens = pl.BlockSpec((1, n1, n2, MXU_DIM), lambda bi, hi: (bi, 0, 0, hi))
    return pl.pallas_call(
        functools.partial(_fnet_kernel, n2=n2, reps=reps),
        grid=(b, FNET_WIDTH // MXU_DIM),
        in_specs=[tokens, const((2 * Y_K1, 2 * n1)), const((2 * Y_K1, 2 * n1)),
                  const((MXU_DIM, MXU_DIM)), const((MXU_DIM, MXU_DIM)),
                  const((n1 // reps, 1, MXU_DIM)), const((n1 // reps, 1, MXU_DIM)),
                  pl.BlockSpec((HALF_GROUPS, 2 * GROUP_DIM, GROUP_DIM), lambda bi, hi: (hi, 0, 0))],
        out_specs=tokens,
        out_shape=jax.ShapeDtypeStruct((b, n1, n2, FNET_WIDTH), BF16),
        scratch_shapes=[pltpu.VMEM((STAGE_A_UNROLL, HALF_GROUPS, n1 * WORD_ROWS, GROUP_DIM), jnp.uint32),
                        pltpu.VMEM((Y_K1 // WORD_ROWS, HALF_GROUPS, n2 * WORD_ROWS, GROUP_DIM), jnp.uint32)],
        compiler_params=_params(("parallel", "parallel")),
        name="fnet",
    )(u.reshape(b, n1, n2, FNET_WIDTH), *_dft_stage_a(n1), p_mat, q_mat, cos_b, sin_b, m_chan)


def _mixer_kernel(mod_ref, f_ref, q_ref, k_ref, v_ref, g_ref, sb_ref, w_ref,
                  intra_ref, xif_ref, xib_ref, zeta_ref, o_ref, state_ref, r_ref, fs_ref, *, n_chunks, dec_f):
    @pl.when(pl.program_id(1) == 0)
    def _():
        state_ref[...] = jnp.zeros_like(state_ref)

    tiles = BF16_ROWS // n_chunks
    sub = pl.program_id(1) % tiles

    @pl.when(sub == 0)
    def _():
        _unpack_rows(lambda lanes: f_ref[0, :, :, lanes], fs_ref)

    f_rows = [_strided_rows(fs_ref, sub * n_chunks + c) for c in range(n_chunks)]
    mix = jnp.dot(jnp.concatenate(f_rows, axis=0), w_ref[:FNET_WIDTH, :], preferred_element_type=F32)

    for c in range(n_chunks):
        rows = pl.ds(c * CHUNK, CHUNK)
        for hd in range(RET_HEADS):
            lanes = slice(hd * HEAD_DIM, (hd + 1) * HEAD_DIM)
            q = q_ref[0, rows, lanes]
            k = k_ref[0, rows, lanes]
            v = v_ref[0, rows, lanes]
            qf = q.astype(F32)
            scores = lax.dot_general(q, k, (((1,), (1,)), ((), ())), preferred_element_type=F32)
            r = jnp.dot((scores * intra_ref[hd]).astype(BF16), v, preferred_element_type=F32)
            r += jnp.dot((qf * xif_ref[hd]).astype(BF16), state_ref[hd].astype(BF16),
                         preferred_element_type=F32)
            r += jnp.dot((qf * xib_ref[hd]).astype(BF16), sb_ref[0, c, hd], preferred_element_type=F32)
            vz = (v.astype(F32) * zeta_ref[hd]).astype(BF16)
            state_ref[hd] = state_ref[hd] * dec_f[hd] + _kv_outer(k, vz)
            gate = g_ref[0, rows, lanes].astype(F32)
            r_ref[rows, lanes] = (_rms(r) * (gate * jax.nn.sigmoid(gate))).astype(BF16)

    mix += jnp.dot(r_ref[...], w_ref[FNET_WIDTH:, :], preferred_element_type=F32)
    o_ref[0] = (mod_ref[0, 2:3, :] * mix).astype(BF16)


def _mixer(mod, f, q, k, v, g, sb, w_out, tables):
    intra, xi_f, xi_b, zeta_f, dec_f = tables
    b, s, _ = q.shape
    tm = min(TM_MIXER, s // 2)
    nc = tm // CHUNK
    assert f.shape[1] == CHUNK and BF16_ROWS % nc == 0 and (s // tm) % (BF16_ROWS // nc) == 0
    tiles = BF16_ROWS // nc
    tok = lambda width: pl.BlockSpec((1, tm, width), lambda bi, ti: (bi, ti, 0))
    const = lambda shape: pl.BlockSpec(shape, lambda bi, ti: (0,) * len(shape))
    tab = const((RET_HEADS, CHUNK, HEAD_DIM))
    states = pl.BlockSpec((1, nc, RET_HEADS, HEAD_DIM, HEAD_DIM), lambda bi, ti: (bi, ti, 0, 0, 0))
    return pl.pallas_call(
        functools.partial(_mixer_kernel, n_chunks=nc, dec_f=dec_f),
        grid=(b, s // tm),
        in_specs=[pl.BlockSpec((1, N_MOD, D_MODEL), lambda bi, ti: (bi, 0, 0)),
                  pl.BlockSpec((1, CHUNK, BF16_ROWS, FNET_WIDTH), lambda bi, ti: (bi, 0, ti // tiles, 0)),
                  tok(RET_WIDTH), tok(RET_WIDTH), tok(RET_WIDTH), tok(RET_WIDTH),
                  states,
                  const((D_MODEL, D_MODEL)), tab, tab, tab, tab],
        out_specs=tok(D_MODEL),
        out_shape=jax.ShapeDtypeStruct((b, s, D_MODEL), BF16),
        scratch_shapes=[pltpu.VMEM((RET_HEADS, HEAD_DIM, HEAD_DIM), F32),
                        pltpu.VMEM((tm, RET_WIDTH), BF16),
                        pltpu.VMEM((1, FNET_GROUPS, CHUNK * BF16_ROWS, GROUP_DIM), F32)],
        compiler_params=_params(("parallel", "arbitrary")),
        name="mixer",
    )(mod, f, q, k, v, g, sb, w_out, intra, xi_f, xi_b, zeta_f)


def _mlp_kernel(x_ref, d_ref, mod_ref, gain_ref, w1_ref, w2_ref, gfin_ref, o_ref):
    x = x_ref[0] + d_ref[0].astype(F32)
    a = gain_ref[...] * (1.0 + mod_ref[0, 4:5, :])
    h = (_rms(x) * a + mod_ref[0, 3:4, :]).astype(BF16)
    out = None
    for part in range(MLP_FF_PARTS):
        cols = slice(part * (D_FF // MLP_FF_PARTS), (part + 1) * (D_FF // MLP_FF_PARTS))
        mid = jnp.dot(h, w1_ref[:, cols], preferred_element_type=F32)
        mid = jnp.square(jnp.maximum(mid, 0.0)).astype(BF16)
        slab = jnp.dot(mid, w2_ref[cols, :], preferred_element_type=F32)
        out = slab if out is None else out + slab
    y = x + mod_ref[0, 5:6, :] * out
    o_ref[0] = _rms(y) * gfin_ref[...]


def _mlp(x, delta, mod, gain, w1, w2, gain_final):
    b, s, _ = x.shape
    tm = min(TM_MLP, s)
    tok = pl.BlockSpec((1, tm, D_MODEL), lambda bi, ti: (bi, ti, 0))
    const = lambda shape: pl.BlockSpec(shape, lambda bi, ti: (0,) * len(shape))
    weight = lambda shape: pl.BlockSpec(shape, lambda bi, ti: (0,) * len(shape), pipeline_mode=pl.Buffered(1))
    return pl.pallas_call(
        _mlp_kernel,
        grid=(b, s // tm),
        in_specs=[tok, tok, pl.BlockSpec((1, N_MOD, D_MODEL), lambda bi, ti: (bi, 0, 0)),
                  const((1, D_MODEL)), weight((D_MODEL, D_FF)), weight((D_FF, D_MODEL)), const((1, D_MODEL))],
        out_specs=tok,
        out_shape=jax.ShapeDtypeStruct((b, s, D_MODEL), F32),
        compiler_params=_params(("parallel", "parallel")),
        name="mlp",
    )(x, delta, mod, gain, w1, w2, gain_final)


def _trunk(x, mod, weights, tables):
    gain_mix, w_in, m_chan, w_out, gain_mlp, w1, w2, gain_final = weights
    intra, xi_f, xi_b, zeta_f, zeta_b, dec_f, dec_b = tables
    u, q, k, v, g, sb = _inproj(x, mod, gain_mix, w_in, zeta_b, dec_b)
    f = _fnet(u, m_chan)
    delta = _mixer(mod, f, q, k, v, g, sb, w_out, (intra, xi_f, xi_b, zeta_f, dec_f))
    return _mlp(x, delta, mod, gain_mlp, w1, w2, gain_final)


def kernel(x_prompt, x_sample, c_prompt, c_sample, ada_w, ada_b, norm_mix, w_in, w_fnet, w_out,
           norm_mlp, w_mlp_in, w_mlp_out, norm_final):
    assert ada_w.shape[0] == 1, "one encoder layer"
    bp, bs = x_prompt.shape[0], x_sample.shape[0]
    c_all = jnp.concatenate([c_prompt, c_sample], axis=0)
    mod = _modulation(c_all, ada_w[0], ada_b[0]).reshape(bp + bs, N_MOD, D_MODEL)
    weights = (norm_mix[0].reshape(1, -1), w_in[0].astype(BF16), _fnet_channel_weights(w_fnet[0]),
               w_out[0].astype(BF16), norm_mlp[0].reshape(1, -1), w_mlp_in[0].astype(BF16),
               w_mlp_out[0].astype(BF16), norm_final.reshape(1, -1))
    tables = _retention_tables()
    y_prompt = _trunk(x_prompt, mod[:bp], weights, tables)
    y_sample = _trunk(x_sample, mod[bp:], weights, tables)
    return (y_prompt, y_sample)
```

```python
import functools

import numpy as np
import jax
import jax.numpy as jnp
from jax import lax
from jax.experimental import pallas as pl
from jax.experimental.pallas import tpu as pltpu

F32 = jnp.float32
BF16 = jnp.bfloat16

D_MODEL = 1024
FNET_WIDTH = 512
FNET_GROUPS = 4
GROUP_DIM = 128
RET_WIDTH = 512
RET_HEADS = 4
HEAD_DIM = 128
D_FF = 4096
CHUNK = 128
ROPE_BASE = 10000.0
EPS = 1e-6
N_MOD = 6
IN_WIDTH = FNET_WIDTH + 4 * RET_WIDTH
DECAY_OFFSET_FWD = 0.0
DECAY_OFFSET_BWD = 0.5

DFT_N1 = 128
MXU_DIM = 256
BF16_ROWS = 16
F32_ROWS = 8
WORD_ROWS = BF16_ROWS // 2
HALF_GROUPS = MXU_DIM // GROUP_DIM
Y_K1 = 72
FNET_UNROLL = 32
STAGE_A_UNROLL = 2
VMEM_LIMIT = 56 * 1024 * 1024

TM_INPROJ = 1024
TM_MIXER = 2048
TM_MLP = 1024
MLP_FF_PARTS = 4
MOD_STEPS = 8


def _params(semantics):
    return pltpu.CompilerParams(dimension_semantics=semantics, vmem_limit_bytes=VMEM_LIMIT)


def _retention_tables():
    scale = HEAD_DIM ** -0.5
    h = np.arange(RET_HEADS, dtype=np.float64)
    lg_f = np.log1p(-np.exp2(-5.0 - DECAY_OFFSET_FWD - h))[:, None, None]
    lg_b = np.log1p(-np.exp2(-5.0 - DECAY_OFFSET_BWD - h))[:, None, None]
    i = np.arange(CHUNK, dtype=np.float64)[None, :, None]
    j = np.arange(CHUNK, dtype=np.float64)[None, None, :]
    ones = np.ones((1, 1, HEAD_DIM))
    intra = scale * np.where(i >= j, np.exp(lg_f * np.maximum(i - j, 0.0)), np.exp(lg_b * np.maximum(j - i, 0.0)))
    xi_f = np.exp(lg_f * (i + 1.0)) * ones
    xi_b = np.exp(lg_b * (CHUNK - i)) * ones
    zeta_f = scale * np.exp(lg_f * (CHUNK - 1.0 - i)) * ones
    zeta_b = scale * np.exp(lg_b * i) * ones
    dec_f = tuple(float(v) for v in np.exp(lg_f[:, 0, 0] * CHUNK))
    dec_b = tuple(float(v) for v in np.exp(lg_b[:, 0, 0] * CHUNK))
    f = lambda a: jnp.asarray(a.astype(np.float32))
    return f(intra), f(xi_f), f(xi_b), f(zeta_f), f(zeta_b), dec_f, dec_b


def _f32(a):
    return jnp.asarray(np.asarray(a).astype(np.float32))


def _rotary_tables(seq, tm):
    half = HEAD_DIM // 2
    inv = ROPE_BASE ** (-np.arange(half, dtype=np.float64) / half)
    inv = np.concatenate([inv, inv])
    sign = np.concatenate([-np.ones(half), np.ones(half)])
    base = np.arange(0, seq, tm, dtype=np.float64)[:, None] * inv[None, :]
    local = np.arange(tm, dtype=np.float64)[:, None] * inv[None, :]
    base_rows = np.stack([np.cos(base), np.sin(base), sign * np.cos(base), sign * np.sin(base)], axis=1)
    return _f32(base_rows), _f32(np.stack([np.cos(local), np.sin(local)], axis=0))


def _dft_stage_a(n1):
    k = np.arange(n1, dtype=np.int64)
    ang = 2.0 * np.pi * ((k[:Y_K1, None] * k[None, :]) % n1) / n1
    base = np.stack([np.cos(ang), -np.sin(ang)], axis=1).reshape(2 * Y_K1, n1) * (n1 ** -0.5)
    even, odd = np.zeros((2 * Y_K1, 2 * n1)), np.zeros((2 * Y_K1, 2 * n1))
    even[:, 0::2] = base
    odd[:, 1::2] = base
    return _f32(even).astype(BF16), _f32(odd).astype(BF16)


def _dft_stage_b(n1, n2):
    reps = MXU_DIM // (2 * n2)
    seq = n1 * n2
    idx = np.arange(MXU_DIM, dtype=np.int64)
    row_c, row_r, row_k = idx // (reps * n2), (idx // n2) % reps, idx % n2
    col_r, col_s, col_c = idx // (2 * n2), (idx // 2) % n2, idx % 2
    alpha = 2.0 * np.pi * ((row_k[:, None] * col_s[None, :]) % n2) / n2
    ca, sa = np.cos(alpha), np.sin(alpha)
    c_out, c_in = row_c[:, None], col_c[None, :]
    p_mat = np.where(c_out == c_in, ca, np.where(c_out < c_in, sa, -sa))
    q_mat = np.where(c_out == c_in, -sa, np.where(c_out < c_in, ca, -ca))
    keep = (row_r[:, None] == col_r[None, :]) * ((n2 * GROUP_DIM) ** -0.5)
    blk = np.arange(n1 // reps, dtype=np.int64)[:, None]
    k1 = blk * reps + col_r[None, :]
    beta = 2.0 * np.pi * ((col_s[None, :] * k1) % seq) / seq
    conj = np.where((k1 >= Y_K1) & (col_c[None, :] == 1), -1.0, 1.0)
    return (_f32(p_mat * keep), _f32(q_mat * keep),
            _f32((np.cos(beta) * conj)[:, None, :]), _f32((np.sin(beta) * conj)[:, None, :]))


def _channel_dft():
    c = np.arange(GROUP_DIM, dtype=np.int64)
    ang = 2.0 * np.pi * ((c[:, None] * c[None, :]) % GROUP_DIM) / GROUP_DIM
    return jnp.asarray(np.concatenate([np.cos(ang), np.sin(ang)], axis=0).astype(np.float32))


def _mod_kernel(c_ref, w_ref, b_ref, *refs):
    c = c_ref[...]
    s = c * jax.nn.sigmoid(c)
    w = w_ref[...]
    s_hi, w_hi = s.astype(BF16), w.astype(BF16)
    s_lo, w_lo = (s - s_hi.astype(F32)).astype(BF16), (w - w_hi.astype(F32)).astype(BF16)
    dot = functools.partial(jnp.dot, preferred_element_type=F32)
    n_cast = (len(refs) - 1) // 2
    o_ref = refs[n_cast]
    o_ref[...] = dot(s_hi, w_hi) + (dot(s_hi, w_lo) + dot(s_lo, w_hi)) + b_ref[...]
    for src, dst in zip(refs[:n_cast], refs[n_cast + 1:]):
        dst[...] = src[...].astype(BF16)


def _modulation_and_casts(c, ada_w, ada_b, weights):
    rows = c.shape[0]
    steps = MOD_STEPS
    tn = N_MOD * D_MODEL // steps
    row_block = lambda w: pl.BlockSpec((w.shape[0] // steps, w.shape[1]), lambda n: (n, 0))
    out = pl.pallas_call(
        _mod_kernel,
        grid=(steps,),
        in_specs=[pl.BlockSpec((rows, D_MODEL), lambda n: (0, 0)),
                  pl.BlockSpec((D_MODEL, tn), lambda n: (0, n)),
                  pl.BlockSpec((1, tn), lambda n: (0, n))] + [row_block(w) for w in weights],
        out_specs=[pl.BlockSpec((rows, tn), lambda n: (0, n))] + [row_block(w) for w in weights],
        out_shape=[jax.ShapeDtypeStruct((rows, N_MOD * D_MODEL), F32)]
                  + [jax.ShapeDtypeStruct(w.shape, BF16) for w in weights],
        compiler_params=_params(("arbitrary",)),
        name="adaln_mod",
    )(c, ada_w, ada_b.reshape(1, -1), *weights)
    return out[0], out[1:]


def _fnet_weight_kernel(cs_ref, w_ref, o_ref):
    o_ref[0] = jnp.dot(cs_ref[...], w_ref[0], preferred_element_type=F32,
                       precision=lax.Precision.HIGHEST).astype(BF16)


def _fnet_channel_weights(w_fnet):
    return pl.pallas_call(
        _fnet_weight_kernel,
        grid=(FNET_GROUPS,),
        in_specs=[pl.BlockSpec((2 * GROUP_DIM, GROUP_DIM), lambda g: (0, 0)),
                  pl.BlockSpec((1, GROUP_DIM, GROUP_DIM), lambda g: (g, 0, 0))],
        out_specs=pl.BlockSpec((1, 2 * GROUP_DIM, GROUP_DIM), lambda g: (g, 0, 0)),
        out_shape=jax.ShapeDtypeStruct((FNET_GROUPS, 2 * GROUP_DIM, GROUP_DIM), BF16),
        compiler_params=_params(("arbitrary",)),
        name="fnet_channel_weights",
    )(_channel_dft(), w_fnet)


def _rms(x):
    return x * lax.rsqrt(jnp.mean(x * x, axis=-1, keepdims=True) + EPS)


def _kv_outer(k, vz):
    return lax.dot_general(k, vz, (((0,), (0,)), ((), ())), preferred_element_type=F32)


def _inproj_kernel(x_ref, mod_ref, gain_ref, w_ref, rbase_ref, rlocal_ref, zeta_ref,
                   u_ref, q_ref, k_ref, v_ref, g_ref, sb_ref, state_ref, *, n_chunks, dec_b):
    @pl.when(pl.program_id(1) == 0)
    def _():
        state_ref[...] = jnp.zeros_like(state_ref)

    x = x_ref[0]
    a = gain_ref[...] * (1.0 + mod_ref[0, 1:2, :])
    h = (_rms(x) * a + mod_ref[0, 0:1, :]).astype(BF16)
    proj = jnp.dot(h, w_ref[...], preferred_element_type=F32)
    u_ref[0] = proj[:, :FNET_WIDTH].astype(BF16)
    cos_l, sin_l = rlocal_ref[0], rlocal_ref[1]
    cos = rbase_ref[0, 0:1, :] * cos_l - rbase_ref[0, 1:2, :] * sin_l
    sin = rbase_ref[0, 3:4, :] * cos_l + rbase_ref[0, 2:3, :] * sin_l
    for hd in range(RET_HEADS):
        lo = hd * HEAD_DIM
        for off, ref in ((FNET_WIDTH, q_ref), (FNET_WIDTH + RET_WIDTH, k_ref)):
            t = proj[:, off + lo:off + lo + HEAD_DIM]
            ref[0, :, lo:lo + HEAD_DIM] = (t * cos + pltpu.roll(t, HEAD_DIM // 2, 1) * sin).astype(BF16)
    v_ref[0] = proj[:, FNET_WIDTH + 2 * RET_WIDTH:FNET_WIDTH + 3 * RET_WIDTH].astype(BF16)
    g_ref[0] = proj[:, FNET_WIDTH + 3 * RET_WIDTH:].astype(BF16)

    for c in reversed(range(n_chunks)):
        rows = pl.ds(c * CHUNK, CHUNK)
        for hd in range(RET_HEADS):
            lanes = slice(hd * HEAD_DIM, (hd + 1) * HEAD_DIM)
            sb_ref[0, c, hd] = state_ref[hd].astype(BF16)
            vz = (v_ref[0, rows, lanes].astype(F32) * zeta_ref[hd]).astype(BF16)
            state_ref[hd] = state_ref[hd] * dec_b[hd] + _kv_outer(k_ref[0, rows, lanes], vz)


def _inproj(x, mod, gain, w_in, zeta_b, dec_b):
    b, s, _ = x.shape
    tm = min(TM_INPROJ, s)
    nt = s // tm
    nc = tm // CHUNK
    rot_base, rot_local = _rotary_tables(s, tm)
    rev = lambda bi, ti: (bi, nt - 1 - ti, 0)
    tok = lambda width: pl.BlockSpec((1, tm, width), rev)
    const = lambda shape: pl.BlockSpec(shape, lambda bi, ti: (0,) * len(shape))
    act = jax.ShapeDtypeStruct((b, s, RET_WIDTH), BF16)
    states = pl.BlockSpec((1, nc, RET_HEADS, HEAD_DIM, HEAD_DIM), lambda bi, ti: (bi, nt - 1 - ti, 0, 0, 0))
    state_shape = jax.ShapeDtypeStruct((b, s // CHUNK, RET_HEADS, HEAD_DIM, HEAD_DIM), BF16)
    return pl.pallas_call(
        functools.partial(_inproj_kernel, n_chunks=nc, dec_b=dec_b),
        grid=(b, nt),
        in_specs=[tok(D_MODEL),
                  pl.BlockSpec((1, N_MOD, D_MODEL), lambda bi, ti: (bi, 0, 0)),
                  const((1, D_MODEL)),
                  const((D_MODEL, IN_WIDTH)),
                  pl.BlockSpec((1, 4, HEAD_DIM), lambda bi, ti: (nt - 1 - ti, 0, 0)),
                  const((2, tm, HEAD_DIM)),
                  const((RET_HEADS, CHUNK, HEAD_DIM))],
        out_specs=[tok(FNET_WIDTH), tok(RET_WIDTH), tok(RET_WIDTH), tok(RET_WIDTH), tok(RET_WIDTH),
                   states],
        out_shape=[jax.ShapeDtypeStruct((b, s, FNET_WIDTH), BF16), act, act, act, act,
                   state_shape],
        scratch_shapes=[pltpu.VMEM((RET_HEADS, HEAD_DIM, HEAD_DIM), F32)],
        compiler_params=_params(("parallel", "arbitrary")),
        name="inproj",
    )(x, mod, gain, w_in, rot_base, rot_local, zeta_b)


def _group_lanes():
    return [(g, slice(g * GROUP_DIM, (g + 1) * GROUP_DIM)) for g in range(FNET_GROUPS)]


def _unpack_rows(load, dst_ref, slab=0):
    units = dst_ref.shape[2] // BF16_ROWS
    half = units * F32_ROWS
    for g, lanes in _group_lanes():
        rows = load(lanes).reshape(units, BF16_ROWS, GROUP_DIM).astype(F32)
        dst_ref[slab, g, 0:half, :] = rows[:, :F32_ROWS, :].reshape(half, GROUP_DIM)
        dst_ref[slab, g, half:, :] = rows[:, F32_ROWS:, :].reshape(half, GROUP_DIM)


def _strided_rows(src_ref, row, slab=0):
    units = src_ref.shape[2] // BF16_ROWS
    start = (row // F32_ROWS) * (units * F32_ROWS) + row % F32_ROWS
    rows = pl.ds(start, units, stride=F32_ROWS)
    return jnp.concatenate([src_ref[slab, g, rows, :] for g in range(FNET_GROUPS)], axis=1).astype(BF16)


def _word_rows(src_ref, lead, word, units):
    rows = pl.ds(word, units, stride=WORD_ROWS)
    return jnp.concatenate([pltpu.bitcast(src_ref[lead, g, rows, :], BF16) for g in range(HALF_GROUPS)], axis=1)


def _fnet_kernel(x_ref, even_ref, odd_ref, p_ref, q_ref, cb_ref, sb_ref, m_ref, o_ref, xw_ref, yw_ref, *, n2, reps):
    n1 = DFT_N1
    half = MXU_DIM // 2
    groups = [slice(g * GROUP_DIM, (g + 1) * GROUP_DIM) for g in range(HALF_GROUPS)]

    def aligned(start, step):
        return start * step if isinstance(start, int) else pl.multiple_of(start * step, step)

    def stage_a(sl, slot):
        s2_base = aligned(sl, BF16_ROWS)
        for g, lanes in enumerate(groups):
            block = x_ref[0, :, pl.ds(s2_base, BF16_ROWS), lanes]
            xw_ref[slot, g] = pltpu.bitcast(block.reshape(n1 * BF16_ROWS, GROUP_DIM), jnp.uint32)
        for i in range(WORD_ROWS):
            pair = _word_rows(xw_ref, slot, i, n1)
            for parity, mat_ref in enumerate((even_ref, odd_ref)):
                y = jnp.dot(mat_ref[...], pair, preferred_element_type=F32).astype(BF16)
                words = pltpu.bitcast(y, jnp.uint32)
                row0 = aligned(s2_base + 2 * i + parity, WORD_ROWS)
                for ks in range(Y_K1 // WORD_ROWS):
                    for g, lanes in enumerate(groups):
                        yw_ref[ks, g, pl.ds(row0, WORD_ROWS), :] = words[ks * WORD_ROWS:(ks + 1) * WORD_ROWS, lanes]

    def stage_b(blk, slot):
        t = (p_ref[...] * cb_ref[blk] + q_ref[...] * sb_ref[blk]).astype(BF16)
        pieces = []
        for r in range(reps):
            k1 = blk * reps + r
            src = (n1 - k1 if k1 >= Y_K1 else k1) if isinstance(k1, int) else jnp.where(k1 >= Y_K1, n1 - k1, k1)
            pieces.append(_word_rows(yw_ref, src // WORD_ROWS, src % WORD_ROWS, n2))
        yb = pieces[0] if reps == 1 else jnp.concatenate(pieces, axis=0)
        z = jnp.dot(t, yb, preferred_element_type=F32)
        for g, lanes in enumerate(groups):
            zz = jnp.concatenate([z[:half, lanes], z[half:, lanes]], axis=1).astype(BF16)
            res = jnp.dot(zz, m_ref[g], preferred_element_type=F32)
            for r in range(reps):
                o_ref[0, blk * reps + r, :, lanes] = res[r * n2:(r + 1) * n2].astype(BF16)

    _for_each(n2 // BF16_ROWS, stage_a, STAGE_A_UNROLL)
    _for_each(n1 // reps, stage_b, FNET_UNROLL)


def _for_each(count, body, group):
    if count <= group:
        for i in range(count):
            body(i, i)
        return

    def step(j, carry):
        for u in range(group):
            body(j * group + u, u)
        return carry

    lax.fori_loop(0, count // group, step, 0)


def _fnet(u, m_chan):
    b, s, _ = u.shape
    n1 = DFT_N1
    n2 = s // n1
    assert s == n1 * n2 and MXU_DIM % (2 * n2) == 0 and n2 % BF16_ROWS == 0, s
    reps = MXU_DIM // (2 * n2)
    p_mat, q_mat, cos_b, sin_b = _dft_stage_b(n1, n2)
    const = lambda shape: pl.BlockSpec(shape, lambda bi, hi: (0,) * len(shape))
    tokens = pl.BlockSpec((1, n1, n2, MXU_DIM), lambda bi, hi: (bi, 0, 0, hi))
    return pl.pallas_call(
        functools.partial(_fnet_kernel, n2=n2, reps=reps),
        grid=(b, FNET_WIDTH // MXU_DIM),
        in_specs=[tokens, const((2 * Y_K1, 2 * n1)), const((2 * Y_K1, 2 * n1)),
                  const((MXU_DIM, MXU_DIM)), const((MXU_DIM, MXU_DIM)),
                  const((n1 // reps, 1, MXU_DIM)), const((n1 // reps, 1, MXU_DIM)),
                  pl.BlockSpec((HALF_GROUPS, 2 * GROUP_DIM, GROUP_DIM), lambda bi, hi: (hi, 0, 0))],
        out_specs=tokens,
        out_shape=jax.ShapeDtypeStruct((b, n1, n2, FNET_WIDTH), BF16),
        scratch_shapes=[pltpu.VMEM((STAGE_A_UNROLL, HALF_GROUPS, n1 * WORD_ROWS, GROUP_DIM), jnp.uint32),
                        pltpu.VMEM((Y_K1 // WORD_ROWS, HALF_GROUPS, n2 * WORD_ROWS, GROUP_DIM), jnp.uint32)],
        compiler_params=_params(("parallel", "parallel")),
        name="fnet",
    )(u.reshape(b, n1, n2, FNET_WIDTH), *_dft_stage_a(n1), p_mat, q_mat, cos_b, sin_b, m_chan)


def _mixer_kernel(mod_ref, f_ref, q_ref, k_ref, v_ref, g_ref, sb_ref, w_ref,
                  intra_ref, xif_ref, xib_ref, zeta_ref, o_ref, state_ref, r_ref, fs_ref, *, n_chunks, dec_f):
    @pl.when(pl.program_id(1) == 0)
    def _():
        state_ref[...] = jnp.zeros_like(state_ref)

    tiles = BF16_ROWS // n_chunks
    sub = pl.program_id(1) % tiles

    @pl.when(sub == 0)
    def _():
        _unpack_rows(lambda lanes: f_ref[0, :, :, lanes], fs_ref)

    f_rows = [_strided_rows(fs_ref, sub * n_chunks + c) for c in range(n_chunks)]
    mix = jnp.dot(jnp.concatenate(f_rows, axis=0), w_ref[:FNET_WIDTH, :], preferred_element_type=F32)

    for c in range(n_chunks):
        rows = pl.ds(c * CHUNK, CHUNK)
        for hd in range(RET_HEADS):
            lanes = slice(hd * HEAD_DIM, (hd + 1) * HEAD_DIM)
            q = q_ref[0, rows, lanes]
            k = k_ref[0, rows, lanes]
            v = v_ref[0, rows, lanes]
            qf = q.astype(F32)
            scores = lax.dot_general(q, k, (((1,), (1,)), ((), ())), preferred_element_type=F32)
            r = jnp.dot((scores * intra_ref[hd]).astype(BF16), v, preferred_element_type=F32)
            r += jnp.dot((qf * xif_ref[hd]).astype(BF16), state_ref[hd].astype(BF16),
                         preferred_element_type=F32)
            r += jnp.dot((qf * xib_ref[hd]).astype(BF16), sb_ref[0, c, hd], preferred_element_type=F32)
            vz = (v.astype(F32) * zeta_ref[hd]).astype(BF16)
            state_ref[hd] = state_ref[hd] * dec_f[hd] + _kv_outer(k, vz)
            gate = g_ref[0, rows, lanes].astype(F32)
            r_ref[rows, lanes] = (_rms(r) * (gate * jax.nn.sigmoid(gate))).astype(BF16)

    mix += jnp.dot(r_ref[...], w_ref[FNET_WIDTH:, :], preferred_element_type=F32)
    o_ref[0] = (mod_ref[0, 2:3, :] * mix).astype(BF16)


def _mixer(mod, f, q, k, v, g, sb, w_out, tables):
    intra, xi_f, xi_b, zeta_f, dec_f = tables
    b, s, _ = q.shape
    tm = min(TM_MIXER, s // 2)
    nc = tm // CHUNK
    assert f.shape[1] == CHUNK and BF16_ROWS % nc == 0 and (s // tm) % (BF16_ROWS // nc) == 0
    tiles = BF16_ROWS // nc
    tok = lambda width: pl.BlockSpec((1, tm, width), lambda bi, ti: (bi, ti, 0))
    const = lambda shape: pl.BlockSpec(shape, lambda bi, ti: (0,) * len(shape))
    tab = const((RET_HEADS, CHUNK, HEAD_DIM))
    states = pl.BlockSpec((1, nc, RET_HEADS, HEAD_DIM, HEAD_DIM), lambda bi, ti: (bi, ti, 0, 0, 0))
    return pl.pallas_call(
        functools.partial(_mixer_kernel, n_chunks=nc, dec_f=dec_f),
        grid=(b, s // tm),
        in_specs=[pl.BlockSpec((1, N_MOD, D_MODEL), lambda bi, ti: (bi, 0, 0)),
                  pl.BlockSpec((1, CHUNK, BF16_ROWS, FNET_WIDTH), lambda bi, ti: (bi, 0, ti // tiles, 0)),
                  tok(RET_WIDTH), tok(RET_WIDTH), tok(RET_WIDTH), tok(RET_WIDTH),
                  states,
                  const((D_MODEL, D_MODEL)), tab, tab, tab, tab],
        out_specs=tok(D_MODEL),
        out_shape=jax.ShapeDtypeStruct((b, s, D_MODEL), BF16),
        scratch_shapes=[pltpu.VMEM((RET_HEADS, HEAD_DIM, HEAD_DIM), F32),
                        pltpu.VMEM((tm, RET_WIDTH), BF16),
                        pltpu.VMEM((1, FNET_GROUPS, CHUNK * BF16_ROWS, GROUP_DIM), F32)],
        compiler_params=_params(("parallel", "arbitrary")),
        name="mixer",
    )(mod, f, q, k, v, g, sb, w_out, intra, xi_f, xi_b, zeta_f)


def _mlp_kernel(x_ref, d_ref, mod_ref, gain_ref, w1_ref, w2_ref, gfin_ref, o_ref):
    x = x_ref[0] + d_ref[0].astype(F32)
    a = gain_ref[...] * (1.0 + mod_ref[0, 4:5, :])
    h = (_rms(x) * a + mod_ref[0, 3:4, :]).astype(BF16)
    out = None
    for part in range(MLP_FF_PARTS):
        cols = slice(part * (D_FF // MLP_FF_PARTS), (part + 1) * (D_FF // MLP_FF_PARTS))
        mid = jnp.dot(h, w1_ref[:, cols], preferred_element_type=F32)
        mid = jnp.square(jnp.maximum(mid, 0.0)).astype(BF16)
        slab = jnp.dot(mid, w2_ref[cols, :], preferred_element_type=F32)
        out = slab if out is None else out + slab
    y = x + mod_ref[0, 5:6, :] * out
    o_ref[0] = _rms(y) * gfin_ref[...]


def _mlp(x, delta, mod, gain, w1, w2, gain_final):
    b, s, _ = x.shape
    tm = min(TM_MLP, s)
    tok = pl.BlockSpec((1, tm, D_MODEL), lambda bi, ti: (bi, ti, 0))
    const = lambda shape: pl.BlockSpec(shape, lambda bi, ti: (0,) * len(shape))
    weight = lambda shape: pl.BlockSpec(shape, lambda bi, ti: (0,) * len(shape), pipeline_mode=pl.Buffered(1))
    return pl.pallas_call(
        _mlp_kernel,
        grid=(b, s // tm),
        in_specs=[tok, tok, pl.BlockSpec((1, N_MOD, D_MODEL), lambda bi, ti: (bi, 0, 0)),
                  const((1, D_MODEL)), weight((D_MODEL, D_FF)), weight((D_FF, D_MODEL)), const((1, D_MODEL))],
        out_specs=tok,
        out_shape=jax.ShapeDtypeStruct((b, s, D_MODEL), F32),
        compiler_params=_params(("parallel", "parallel")),
        name="mlp",
    )(x, delta, mod, gain, w1, w2, gain_final)


def _trunk(x, mod, weights, tables):
    gain_mix, w_in, m_chan, w_out, gain_mlp, w1, w2, gain_final = weights
    intra, xi_f, xi_b, zeta_f, zeta_b, dec_f, dec_b = tables
    u, q, k, v, g, sb = _inproj(x, mod, gain_mix, w_in, zeta_b, dec_b)
    f = _fnet(u, m_chan)
    delta = _mixer(mod, f, q, k, v, g, sb, w_out, (intra, xi_f, xi_b, zeta_f, dec_f))
    return _mlp(x, delta, mod, gain_mlp, w1, w2, gain_final)


def kernel(x_prompt, x_sample, c_prompt, c_sample, ada_w, ada_b, norm_mix, w_in, w_fnet, w_out,
           norm_mlp, w_mlp_in, w_mlp_out, norm_final):
    assert ada_w.shape[0] == 1, "one encoder layer"
    bp, bs = x_prompt.shape[0], x_sample.shape[0]
    c_all = jnp.concatenate([c_prompt, c_sample], axis=0)
    mod, (w_in_b, w_out_b, w1_b, w2_b) = _modulation_and_casts(
        c_all, ada_w[0], ada_b[0], (w_in[0], w_out[0], w_mlp_in[0], w_mlp_out[0]))
    mod = mod.reshape(bp + bs, N_MOD, D_MODEL)
    weights = (norm_mix[0].reshape(1, -1), w_in_b, _fnet_channel_weights(w_fnet[0]),
               w_out_b, norm_mlp[0].reshape(1, -1), w1_b, w2_b, norm_final.reshape(1, -1))
    tables = _retention_tables()
    y_prompt = _trunk(x_prompt, mod[:bp], weights, tables)
    y_sample = _trunk(x_sample, mod[bp:], weights, tables)
    return (y_prompt, y_sample)
```

```python
import functools

import numpy as np
import jax
import jax.numpy as jnp
from jax import lax
from jax.experimental import pallas as pl
from jax.experimental.pallas import tpu as pltpu

F32 = jnp.float32
BF16 = jnp.bfloat16

D_MODEL = 1024
FNET_WIDTH = 512
FNET_GROUPS = 4
GROUP_DIM = 128
RET_WIDTH = 512
RET_HEADS = 4
HEAD_DIM = 128
D_FF = 4096
CHUNK = 128
ROPE_BASE = 10000.0
EPS = 1e-6
N_MOD = 6
IN_WIDTH = FNET_WIDTH + 4 * RET_WIDTH
DECAY_OFFSET_FWD = 0.0
DECAY_OFFSET_BWD = 0.5

DFT_N1 = 128
MXU_DIM = 256
BF16_ROWS = 16
F32_ROWS = 8
WORD_ROWS = BF16_ROWS // 2
HALF_GROUPS = MXU_DIM // GROUP_DIM
Y_K1 = 72
FNET_UNROLL = 32
STAGE_A_UNROLL = 2
VMEM_LIMIT = 56 * 1024 * 1024

TM_INPROJ = 1024
TM_MIXER = 2048
TM_MLP = 1024
MLP_FF_PARTS = 4
MOD_STEPS = 8


def _params(semantics):
    return pltpu.CompilerParams(dimension_semantics=semantics, vmem_limit_bytes=VMEM_LIMIT)


def _retention_tables():
    scale = HEAD_DIM ** -0.5
    h = np.arange(RET_HEADS, dtype=np.float64)
    lg_f = np.log1p(-np.exp2(-5.0 - DECAY_OFFSET_FWD - h))[:, None, None]
    lg_b = np.log1p(-np.exp2(-5.0 - DECAY_OFFSET_BWD - h))[:, None, None]
    i = np.arange(CHUNK, dtype=np.float64)[None, :, None]
    j = np.arange(CHUNK, dtype=np.float64)[None, None, :]
    ones = np.ones((1, 1, HEAD_DIM))
    intra = scale * np.where(i >= j, np.exp(lg_f * np.maximum(i - j, 0.0)), np.exp(lg_b * np.maximum(j - i, 0.0)))
    xi_f = np.exp(lg_f * (i + 1.0)) * ones
    xi_b = np.exp(lg_b * (CHUNK - i)) * ones
    zeta_f = scale * np.exp(lg_f * (CHUNK - 1.0 - i)) * ones
    zeta_b = scale * np.exp(lg_b * i) * ones
    dec_f = tuple(float(v) for v in np.exp(lg_f[:, 0, 0] * CHUNK))
    dec_b = tuple(float(v) for v in np.exp(lg_b[:, 0, 0] * CHUNK))
    f = lambda a: jnp.asarray(a.astype(np.float32))
    return f(intra), f(xi_f), f(xi_b), f(zeta_f), f(zeta_b), dec_f, dec_b


def _f32(a):
    return jnp.asarray(np.asarray(a).astype(np.float32))


def _rotary_tables(seq, tm):
    half = HEAD_DIM // 2
    inv = ROPE_BASE ** (-np.arange(half, dtype=np.float64) / half)
    inv = np.concatenate([inv, inv])
    sign = np.concatenate([-np.ones(half), np.ones(half)])
    base = np.arange(0, seq, tm, dtype=np.float64)[:, None] * inv[None, :]
    local = np.arange(tm, dtype=np.float64)[:, None] * inv[None, :]
    base_rows = np.stack([np.cos(base), np.sin(base), sign * np.cos(base), sign * np.sin(base)], axis=1)
    return _f32(base_rows), _f32(np.stack([np.cos(local), np.sin(local)], axis=0))


def _dft_stage_a(n1):
    k = np.arange(n1, dtype=np.int64)
    ang = 2.0 * np.pi * ((k[:Y_K1, None] * k[None, :]) % n1) / n1
    base = np.stack([np.cos(ang), -np.sin(ang)], axis=1).reshape(2 * Y_K1, n1) * (n1 ** -0.5)
    even, odd = np.zeros((2 * Y_K1, 2 * n1)), np.zeros((2 * Y_K1, 2 * n1))
    even[:, 0::2] = base
    odd[:, 1::2] = base
    return _f32(even).astype(BF16), _f32(odd).astype(BF16)


def _dft_stage_b(n1, n2):
    reps = MXU_DIM // (2 * n2)
    seq = n1 * n2
    idx = np.arange(MXU_DIM, dtype=np.int64)
    row_c, row_r, row_k = idx // (reps * n2), (idx // n2) % reps, idx % n2
    col_r, col_s, col_c = idx // (2 * n2), (idx // 2) % n2, idx % 2
    alpha = 2.0 * np.pi * ((row_k[:, None] * col_s[None, :]) % n2) / n2
    ca, sa = np.cos(alpha), np.sin(alpha)
    c_out, c_in = row_c[:, None], col_c[None, :]
    p_mat = np.where(c_out == c_in, ca, np.where(c_out < c_in, sa, -sa))
    q_mat = np.where(c_out == c_in, -sa, np.where(c_out < c_in, ca, -ca))
    keep = (row_r[:, None] == col_r[None, :]) * ((n2 * GROUP_DIM) ** -0.5)
    blk = np.arange(n1 // reps, dtype=np.int64)[:, None]
    k1 = blk * reps + col_r[None, :]
    beta = 2.0 * np.pi * ((col_s[None, :] * k1) % seq) / seq
    conj = np.where((k1 >= Y_K1) & (col_c[None, :] == 1), -1.0, 1.0)
    return (_f32(p_mat * keep), _f32(q_mat * keep),
            _f32((np.cos(beta) * conj)[:, None, :]), _f32((np.sin(beta) * conj)[:, None, :]))


def _channel_dft():
    c = np.arange(GROUP_DIM, dtype=np.int64)
    ang = 2.0 * np.pi * ((c[:, None] * c[None, :]) % GROUP_DIM) / GROUP_DIM
    return jnp.asarray(np.concatenate([np.cos(ang), np.sin(ang)], axis=0).astype(np.float32))


def _mod_kernel(c_ref, w_ref, b_ref, *refs):
    c = c_ref[...]
    s = c * jax.nn.sigmoid(c)
    w = w_ref[...]
    s_hi, w_hi = s.astype(BF16), w.astype(BF16)
    s_lo, w_lo = (s - s_hi.astype(F32)).astype(BF16), (w - w_hi.astype(F32)).astype(BF16)
    dot = functools.partial(jnp.dot, preferred_element_type=F32)
    n_cast = (len(refs) - 1) // 2
    o_ref = refs[n_cast]
    o_ref[...] = dot(s_hi, w_hi) + (dot(s_hi, w_lo) + dot(s_lo, w_hi)) + b_ref[...]
    for src, dst in zip(refs[:n_cast], refs[n_cast + 1:]):
        dst[...] = src[...].astype(BF16)


def _modulation_and_casts(c, ada_w, ada_b, weights):
    rows = c.shape[0]
    steps = MOD_STEPS
    tn = N_MOD * D_MODEL // steps
    row_block = lambda w: pl.BlockSpec((w.shape[0] // steps, w.shape[1]), lambda n: (n, 0))
    out = pl.pallas_call(
        _mod_kernel,
        grid=(steps,),
        in_specs=[pl.BlockSpec((rows, D_MODEL), lambda n: (0, 0)),
                  pl.BlockSpec((D_MODEL, tn), lambda n: (0, n)),
                  pl.BlockSpec((1, tn), lambda n: (0, n))] + [row_block(w) for w in weights],
        out_specs=[pl.BlockSpec((rows, tn), lambda n: (0, n))] + [row_block(w) for w in weights],
        out_shape=[jax.ShapeDtypeStruct((rows, N_MOD * D_MODEL), F32)]
                  + [jax.ShapeDtypeStruct(w.shape, BF16) for w in weights],
        compiler_params=_params(("arbitrary",)),
        name="adaln_mod",
    )(c, ada_w, ada_b.reshape(1, -1), *weights)
    return out[0], out[1:]


def _fnet_weight_kernel(cs_ref, w_ref, o_ref):
    o_ref[0] = jnp.dot(cs_ref[...], w_ref[0], preferred_element_type=F32,
                       precision=lax.Precision.HIGHEST).astype(BF16)


def _fnet_channel_weights(w_fnet):
    return pl.pallas_call(
        _fnet_weight_kernel,
        grid=(FNET_GROUPS,),
        in_specs=[pl.BlockSpec((2 * GROUP_DIM, GROUP_DIM), lambda g: (0, 0)),
                  pl.BlockSpec((1, GROUP_DIM, GROUP_DIM), lambda g: (g, 0, 0))],
        out_specs=pl.BlockSpec((1, 2 * GROUP_DIM, GROUP_DIM), lambda g: (g, 0, 0)),
        out_shape=jax.ShapeDtypeStruct((FNET_GROUPS, 2 * GROUP_DIM, GROUP_DIM), BF16),
        compiler_params=_params(("arbitrary",)),
        name="fnet_channel_weights",
    )(_channel_dft(), w_fnet)


def _rms(x):
    return x * lax.rsqrt(jnp.mean(x * x, axis=-1, keepdims=True) + EPS)


def _kv_outer(k, vz):
    return lax.dot_general(k, vz, (((0,), (0,)), ((), ())), preferred_element_type=F32)


def _inproj_kernel(x_ref, mod_ref, gain_ref, w_ref, rbase_ref, rlocal_ref, zeta_ref, *refs, n_chunks, dec_b, n_cast):
    u_ref, q_ref, k_ref, v_ref, g_ref, sb_ref = refs[n_cast:n_cast + 6]
    state_ref = refs[-1]
    for src, dst in zip(refs[:n_cast], refs[n_cast + 6:-1]):
        dst[...] = src[...].astype(BF16)

    @pl.when(pl.program_id(1) == 0)
    def _():
        state_ref[...] = jnp.zeros_like(state_ref)

    x = x_ref[0]
    a = gain_ref[...] * (1.0 + mod_ref[0, 1:2, :])
    h = (_rms(x) * a + mod_ref[0, 0:1, :]).astype(BF16)
    proj = jnp.dot(h, w_ref[...], preferred_element_type=F32)
    u_ref[0] = proj[:, :FNET_WIDTH].astype(BF16)
    cos_l, sin_l = rlocal_ref[0], rlocal_ref[1]
    cos = rbase_ref[0, 0:1, :] * cos_l - rbase_ref[0, 1:2, :] * sin_l
    sin = rbase_ref[0, 3:4, :] * cos_l + rbase_ref[0, 2:3, :] * sin_l
    for hd in range(RET_HEADS):
        lo = hd * HEAD_DIM
        for off, ref in ((FNET_WIDTH, q_ref), (FNET_WIDTH + RET_WIDTH, k_ref)):
            t = proj[:, off + lo:off + lo + HEAD_DIM]
            ref[0, :, lo:lo + HEAD_DIM] = (t * cos + pltpu.roll(t, HEAD_DIM // 2, 1) * sin).astype(BF16)
    v_ref[0] = proj[:, FNET_WIDTH + 2 * RET_WIDTH:FNET_WIDTH + 3 * RET_WIDTH].astype(BF16)
    g_ref[0] = proj[:, FNET_WIDTH + 3 * RET_WIDTH:].astype(BF16)

    for c in reversed(range(n_chunks)):
        rows = pl.ds(c * CHUNK, CHUNK)
        for hd in range(RET_HEADS):
            lanes = slice(hd * HEAD_DIM, (hd + 1) * HEAD_DIM)
            sb_ref[0, c, hd] = state_ref[hd].astype(BF16)
            vz = (v_ref[0, rows, lanes].astype(F32) * zeta_ref[hd]).astype(BF16)
            state_ref[hd] = state_ref[hd] * dec_b[hd] + _kv_outer(k_ref[0, rows, lanes], vz)


def _inproj(x, mod, gain, w_in, zeta_b, dec_b, casts=()):
    b, s, _ = x.shape
    tm = min(TM_INPROJ, s)
    nt = s // tm
    nc = tm // CHUNK
    rot_base, rot_local = _rotary_tables(s, tm)
    rev = lambda bi, ti: (bi, nt - 1 - ti, 0)
    tok = lambda width: pl.BlockSpec((1, tm, width), rev)
    const = lambda shape: pl.BlockSpec(shape, lambda bi, ti: (0,) * len(shape))
    act = jax.ShapeDtypeStruct((b, s, RET_WIDTH), BF16)
    states = pl.BlockSpec((1, nc, RET_HEADS, HEAD_DIM, HEAD_DIM), lambda bi, ti: (bi, nt - 1 - ti, 0, 0, 0))
    state_shape = jax.ShapeDtypeStruct((b, s // CHUNK, RET_HEADS, HEAD_DIM, HEAD_DIM), BF16)
    cast_specs = [pl.BlockSpec((w.shape[0] // (b * nt), w.shape[1]), lambda bi, ti: (bi * nt + ti, 0)) for w in casts]
    out = pl.pallas_call(
        functools.partial(_inproj_kernel, n_chunks=nc, dec_b=dec_b, n_cast=len(casts)),
        grid=(b, nt),
        in_specs=[tok(D_MODEL),
                  pl.BlockSpec((1, N_MOD, D_MODEL), lambda bi, ti: (bi, 0, 0)),
                  const((1, D_MODEL)),
                  const((D_MODEL, IN_WIDTH)),
                  pl.BlockSpec((1, 4, HEAD_DIM), lambda bi, ti: (nt - 1 - ti, 0, 0)),
                  const((2, tm, HEAD_DIM)),
                  const((RET_HEADS, CHUNK, HEAD_DIM))] + cast_specs,
        out_specs=[tok(FNET_WIDTH), tok(RET_WIDTH), tok(RET_WIDTH), tok(RET_WIDTH), tok(RET_WIDTH),
                   states] + cast_specs,
        out_shape=[jax.ShapeDtypeStruct((b, s, FNET_WIDTH), BF16), act, act, act, act,
                   state_shape] + [jax.ShapeDtypeStruct(w.shape, BF16) for w in casts],
        scratch_shapes=[pltpu.VMEM((RET_HEADS, HEAD_DIM, HEAD_DIM), F32)],
        compiler_params=_params(("parallel", "arbitrary")),
        name="inproj",
    )(x, mod, gain, w_in, rot_base, rot_local, zeta_b, *casts)
    return out[:6], out[6:]


def _group_lanes():
    return [(g, slice(g * GROUP_DIM, (g + 1) * GROUP_DIM)) for g in range(FNET_GROUPS)]


def _unpack_rows(load, dst_ref, slab=0):
    units = dst_ref.shape[2] // BF16_ROWS
    half = units * F32_ROWS
    for g, lanes in _group_lanes():
        rows = load(lanes).reshape(units, BF16_ROWS, GROUP_DIM).astype(F32)
        dst_ref[slab, g, 0:half, :] = rows[:, :F32_ROWS, :].reshape(half, GROUP_DIM)
        dst_ref[slab, g, half:, :] = rows[:, F32_ROWS:, :].reshape(half, GROUP_DIM)


def _strided_rows(src_ref, row, slab=0):
    units = src_ref.shape[2] // BF16_ROWS
    start = (row // F32_ROWS) * (units * F32_ROWS) + row % F32_ROWS
    rows = pl.ds(start, units, stride=F32_ROWS)
    return jnp.concatenate([src_ref[slab, g, rows, :] for g in range(FNET_GROUPS)], axis=1).astype(BF16)


def _word_rows(src_ref, lead, word, units):
    rows = pl.ds(word, units, stride=WORD_ROWS)
    return jnp.concatenate([pltpu.bitcast(src_ref[lead, g, rows, :], BF16) for g in range(HALF_GROUPS)], axis=1)


def _fnet_kernel(x_ref, even_ref, odd_ref, p_ref, q_ref, cb_ref, sb_ref, m_ref, o_ref, xw_ref, yw_ref, *, n2, reps):
    n1 = DFT_N1
    half = MXU_DIM // 2
    groups = [slice(g * GROUP_DIM, (g + 1) * GROUP_DIM) for g in range(HALF_GROUPS)]

    def aligned(start, step):
        return start * step if isinstance(start, int) else pl.multiple_of(start * step, step)

    def stage_a(sl, slot):
        s2_base = aligned(sl, BF16_ROWS)
        for g, lanes in enumerate(groups):
            block = x_ref[0, :, pl.ds(s2_base, BF16_ROWS), lanes]
            xw_ref[slot, g] = pltpu.bitcast(block.reshape(n1 * BF16_ROWS, GROUP_DIM), jnp.uint32)
        for i in range(WORD_ROWS):
            pair = _word_rows(xw_ref, slot, i, n1)
            for parity, mat_ref in enumerate((even_ref, odd_ref)):
                y = jnp.dot(mat_ref[...], pair, preferred_element_type=F32).astype(BF16)
                words = pltpu.bitcast(y, jnp.uint32)
                row0 = aligned(s2_base + 2 * i + parity, WORD_ROWS)
                for ks in range(Y_K1 // WORD_ROWS):
                    for g, lanes in enumerate(groups):
                        yw_ref[ks, g, pl.ds(row0, WORD_ROWS), :] = words[ks * WORD_ROWS:(ks + 1) * WORD_ROWS, lanes]

    def stage_b(blk, slot):
        t = (p_ref[...] * cb_ref[blk] + q_ref[...] * sb_ref[blk]).astype(BF16)
        pieces = []
        for r in range(reps):
            k1 = blk * reps + r
            src = (n1 - k1 if k1 >= Y_K1 else k1) if isinstance(k1, int) else jnp.where(k1 >= Y_K1, n1 - k1, k1)
            pieces.append(_word_rows(yw_ref, src // WORD_ROWS, src % WORD_ROWS, n2))
        yb = pieces[0] if reps == 1 else jnp.concatenate(pieces, axis=0)
        z = jnp.dot(t, yb, preferred_element_type=F32)
        for g, lanes in enumerate(groups):
            zz = jnp.concatenate([z[:half, lanes], z[half:, lanes]], axis=1).astype(BF16)
            res = jnp.dot(zz, m_ref[g], preferred_element_type=F32)
            for r in range(reps):
                o_ref[0, blk * reps + r, :, lanes] = res[r * n2:(r + 1) * n2].astype(BF16)

    _for_each(n2 // BF16_ROWS, stage_a, STAGE_A_UNROLL)
    _for_each(n1 // reps, stage_b, FNET_UNROLL)


def _for_each(count, body, group):
    if count <= group:
        for i in range(count):
            body(i, i)
        return

    def step(j, carry):
        for u in range(group):
            body(j * group + u, u)
        return carry

    lax.fori_loop(0, count // group, step, 0)


def _fnet(u, m_chan):
    b, s, _ = u.shape
    n1 = DFT_N1
    n2 = s // n1
    assert s == n1 * n2 and MXU_DIM % (2 * n2) == 0 and n2 % BF16_ROWS == 0, s
    reps = MXU_DIM // (2 * n2)
    p_mat, q_mat, cos_b, sin_b = _dft_stage_b(n1, n2)
    const = lambda shape: pl.BlockSpec(shape, lambda bi, hi: (0,) * len(shape))
    tokens = pl.BlockSpec((1, n1, n2, MXU_DIM), lambda bi, hi: (bi, 0, 0, hi))
    return pl.pallas_call(
        functools.partial(_fnet_kernel, n2=n2, reps=reps),
        grid=(b, FNET_WIDTH // MXU_DIM),
        in_specs=[tokens, const((2 * Y_K1, 2 * n1)), const((2 * Y_K1, 2 * n1)),
                  const((MXU_DIM, MXU_DIM)), const((MXU_DIM, MXU_DIM)),
                  const((n1 // reps, 1, MXU_DIM)), const((n1 // reps, 1, MXU_DIM)),
                  pl.BlockSpec((HALF_GROUPS, 2 * GROUP_DIM, GROUP_DIM), lambda bi, hi: (hi, 0, 0))],
        out_specs=tokens,
        out_shape=jax.ShapeDtypeStruct((b, n1, n2, FNET_WIDTH), BF16),
        scratch_shapes=[pltpu.VMEM((STAGE_A_UNROLL, HALF_GROUPS, n1 * WORD_ROWS, GROUP_DIM), jnp.uint32),
                        pltpu.VMEM((Y_K1 // WORD_ROWS, HALF_GROUPS, n2 * WORD_ROWS, GROUP_DIM), jnp.uint32)],
        compiler_params=_params(("parallel", "parallel")),
        name="fnet",
    )(u.reshape(b, n1, n2, FNET_WIDTH), *_dft_stage_a(n1), p_mat, q_mat, cos_b, sin_b, m_chan)


def _mixer_kernel(mod_ref, f_ref, q_ref, k_ref, v_ref, g_ref, sb_ref, w_ref,
                  intra_ref, xif_ref, xib_ref, zeta_ref, o_ref, state_ref, r_ref, fs_ref, *, n_chunks, dec_f):
    @pl.when(pl.program_id(1) == 0)
    def _():
        state_ref[...] = jnp.zeros_like(state_ref)

    tiles = BF16_ROWS // n_chunks
    sub = pl.program_id(1) % tiles

    @pl.when(sub == 0)
    def _():
        _unpack_rows(lambda lanes: f_ref[0, :, :, lanes], fs_ref)

    f_rows = [_strided_rows(fs_ref, sub * n_chunks + c) for c in range(n_chunks)]
    mix = jnp.dot(jnp.concatenate(f_rows, axis=0), w_ref[:FNET_WIDTH, :], preferred_element_type=F32)

    for c in range(n_chunks):
        rows = pl.ds(c * CHUNK, CHUNK)
        for hd in range(RET_HEADS):
            lanes = slice(hd * HEAD_DIM, (hd + 1) * HEAD_DIM)
            q = q_ref[0, rows, lanes]
            k = k_ref[0, rows, lanes]
            v = v_ref[0, rows, lanes]
            qf = q.astype(F32)
            scores = lax.dot_general(q, k, (((1,), (1,)), ((), ())), preferred_element_type=F32)
            r = jnp.dot((scores * intra_ref[hd]).astype(BF16), v, preferred_element_type=F32)
            r += jnp.dot((qf * xif_ref[hd]).astype(BF16), state_ref[hd].astype(BF16),
                         preferred_element_type=F32)
            r += jnp.dot((qf * xib_ref[hd]).astype(BF16), sb_ref[0, c, hd], preferred_element_type=F32)
            vz = (v.astype(F32) * zeta_ref[hd]).astype(BF16)
            state_ref[hd] = state_ref[hd] * dec_f[hd] + _kv_outer(k, vz)
            gate = g_ref[0, rows, lanes].astype(F32)
            r_ref[rows, lanes] = (_rms(r) * (gate * jax.nn.sigmoid(gate))).astype(BF16)

    mix += jnp.dot(r_ref[...], w_ref[FNET_WIDTH:, :], preferred_element_type=F32)
    o_ref[0] = (mod_ref[0, 2:3, :] * mix).astype(BF16)


def _mixer(mod, f, q, k, v, g, sb, w_out, tables):
    intra, xi_f, xi_b, zeta_f, dec_f = tables
    b, s, _ = q.shape
    tm = min(TM_MIXER, s // 2)
    nc = tm // CHUNK
    assert f.shape[1] == CHUNK and BF16_ROWS % nc == 0 and (s // tm) % (BF16_ROWS // nc) == 0
    tiles = BF16_ROWS // nc
    tok = lambda width: pl.BlockSpec((1, tm, width), lambda bi, ti: (bi, ti, 0))
    const = lambda shape: pl.BlockSpec(shape, lambda bi, ti: (0,) * len(shape))
    tab = const((RET_HEADS, CHUNK, HEAD_DIM))
    states = pl.BlockSpec((1, nc, RET_HEADS, HEAD_DIM, HEAD_DIM), lambda bi, ti: (bi, ti, 0, 0, 0))
    return pl.pallas_call(
        functools.partial(_mixer_kernel, n_chunks=nc, dec_f=dec_f),
        grid=(b, s // tm),
        in_specs=[pl.BlockSpec((1, N_MOD, D_MODEL), lambda bi, ti: (bi, 0, 0)),
                  pl.BlockSpec((1, CHUNK, BF16_ROWS, FNET_WIDTH), lambda bi, ti: (bi, 0, ti // tiles, 0)),
                  tok(RET_WIDTH), tok(RET_WIDTH), tok(RET_WIDTH), tok(RET_WIDTH),
                  states,
                  const((D_MODEL, D_MODEL)), tab, tab, tab, tab],
        out_specs=tok(D_MODEL),
        out_shape=jax.ShapeDtypeStruct((b, s, D_MODEL), BF16),
        scratch_shapes=[pltpu.VMEM((RET_HEADS, HEAD_DIM, HEAD_DIM), F32),
                        pltpu.VMEM((tm, RET_WIDTH), BF16),
                        pltpu.VMEM((1, FNET_GROUPS, CHUNK * BF16_ROWS, GROUP_DIM), F32)],
        compiler_params=_params(("parallel", "arbitrary")),
        name="mixer",
    )(mod, f, q, k, v, g, sb, w_out, intra, xi_f, xi_b, zeta_f)


def _mlp_kernel(x_ref, d_ref, mod_ref, gain_ref, w1_ref, w2_ref, gfin_ref, o_ref):
    x = x_ref[0] + d_ref[0].astype(F32)
    a = gain_ref[...] * (1.0 + mod_ref[0, 4:5, :])
    h = (_rms(x) * a + mod_ref[0, 3:4, :]).astype(BF16)
    out = None
    for part in range(MLP_FF_PARTS):
        cols = slice(part * (D_FF // MLP_FF_PARTS), (part + 1) * (D_FF // MLP_FF_PARTS))
        mid = jnp.dot(h, w1_ref[:, cols], preferred_element_type=F32)
        mid = jnp.square(jnp.maximum(mid, 0.0)).astype(BF16)
        slab = jnp.dot(mid, w2_ref[cols, :], preferred_element_type=F32)
        out = slab if out is None else out + slab
    y = x + mod_ref[0, 5:6, :] * out
    o_ref[0] = _rms(y) * gfin_ref[...]


def _mlp(x, delta, mod, gain, w1, w2, gain_final):
    b, s, _ = x.shape
    tm = min(TM_MLP, s)
    tok = pl.BlockSpec((1, tm, D_MODEL), lambda bi, ti: (bi, ti, 0))
    const = lambda shape: pl.BlockSpec(shape, lambda bi, ti: (0,) * len(shape))
    weight = lambda shape: pl.BlockSpec(shape, lambda bi, ti: (0,) * len(shape), pipeline_mode=pl.Buffered(1))
    return pl.pallas_call(
        _mlp_kernel,
        grid=(b, s // tm),
        in_specs=[tok, tok, pl.BlockSpec((1, N_MOD, D_MODEL), lambda bi, ti: (bi, 0, 0)),
                  const((1, D_MODEL)), weight((D_MODEL, D_FF)), weight((D_FF, D_MODEL)), const((1, D_MODEL))],
        out_specs=tok,
        out_shape=jax.ShapeDtypeStruct((b, s, D_MODEL), F32),
        compiler_params=_params(("parallel", "parallel")),
        name="mlp",
    )(x, delta, mod, gain, w1, w2, gain_final)


def _trunk(x, mod, weights, tables):
    gain_mix, w_in, m_chan, w_out, gain_mlp, w1, w2, gain_final = weights
    intra, xi_f, xi_b, zeta_f, zeta_b, dec_f, dec_b = tables
    casts = (w1, w2) if w1.dtype != BF16 else ()
    (u, q, k, v, g, sb), cast = _inproj(x, mod, gain_mix, w_in, zeta_b, dec_b, casts)
    if casts:
        w1, w2 = cast
    f = _fnet(u, m_chan)
    delta = _mixer(mod, f, q, k, v, g, sb, w_out, (intra, xi_f, xi_b, zeta_f, dec_f))
    return _mlp(x, delta, mod, gain_mlp, w1, w2, gain_final), w1, w2


def kernel(x_prompt, x_sample, c_prompt, c_sample, ada_w, ada_b, norm_mix, w_in, w_fnet, w_out,
           norm_mlp, w_mlp_in, w_mlp_out, norm_final):
    assert ada_w.shape[0] == 1, "one encoder layer"
    bp, bs = x_prompt.shape[0], x_sample.shape[0]
    c_all = jnp.concatenate([c_prompt, c_sample], axis=0)
    mod, (w_in_b, w_out_b) = _modulation_and_casts(c_all, ada_w[0], ada_b[0], (w_in[0], w_out[0]))
    mod = mod.reshape(bp + bs, N_MOD, D_MODEL)
    weights = (norm_mix[0].reshape(1, -1), w_in_b, _fnet_channel_weights(w_fnet[0]),
               w_out_b, norm_mlp[0].reshape(1, -1), w_mlp_in[0], w_mlp_out[0], norm_final.reshape(1, -1))
    tables = _retention_tables()
    y_prompt, w1_b, w2_b = _trunk(x_prompt, mod[:bp], weights, tables)
    y_sample, _, _ = _trunk(x_sample, mod[bp:], weights[:5] + (w1_b, w2_b) + weights[7:], tables)
    return (y_prompt, y_sample)
```

```python
import functools

import numpy as np
import jax
import jax.numpy as jnp
from jax import lax
from jax.experimental import pallas as pl
from jax.experimental.pallas import tpu as pltpu

F32 = jnp.float32
BF16 = jnp.bfloat16

D_MODEL = 1024
FNET_WIDTH = 512
FNET_GROUPS = 4
GROUP_DIM = 128
RET_WIDTH = 512
RET_HEADS = 4
HEAD_DIM = 128
D_FF = 4096
CHUNK = 128
ROPE_BASE = 10000.0
EPS = 1e-6
N_MOD = 6
IN_WIDTH = FNET_WIDTH + 4 * RET_WIDTH
DECAY_OFFSET_FWD = 0.0
DECAY_OFFSET_BWD = 0.5

DFT_N1 = 128
MXU_DIM = 256
BF16_ROWS = 16
F32_ROWS = 8
WORD_ROWS = BF16_ROWS // 2
HALF_GROUPS = MXU_DIM // GROUP_DIM
Y_K1 = 72
FNET_UNROLL = 32
STAGE_A_UNROLL = 2
VMEM_LIMIT = 56 * 1024 * 1024

TM_INPROJ = 1024
TM_MIXER = 2048
TM_MLP = 1024
MLP_FF_PARTS = 4
MOD_STEPS = 8


def _params(semantics):
    return pltpu.CompilerParams(dimension_semantics=semantics, vmem_limit_bytes=VMEM_LIMIT)


def _retention_tables():
    scale = HEAD_DIM ** -0.5
    h = np.arange(RET_HEADS, dtype=np.float64)
    lg_f = np.log1p(-np.exp2(-5.0 - DECAY_OFFSET_FWD - h))[:, None, None]
    lg_b = np.log1p(-np.exp2(-5.0 - DECAY_OFFSET_BWD - h))[:, None, None]
    i = np.arange(CHUNK, dtype=np.float64)[None, :, None]
    j = np.arange(CHUNK, dtype=np.float64)[None, None, :]
    ones = np.ones((1, 1, HEAD_DIM))
    intra = scale * np.where(i >= j, np.exp(lg_f * np.maximum(i - j, 0.0)), np.exp(lg_b * np.maximum(j - i, 0.0)))
    xi_f = np.exp(lg_f * (i + 1.0)) * ones
    xi_b = np.exp(lg_b * (CHUNK - i)) * ones
    zeta_f = scale * np.exp(lg_f * (CHUNK - 1.0 - i)) * ones
    zeta_b = scale * np.exp(lg_b * i) * ones
    dec_f = tuple(float(v) for v in np.exp(lg_f[:, 0, 0] * CHUNK))
    dec_b = tuple(float(v) for v in np.exp(lg_b[:, 0, 0] * CHUNK))
    f = lambda a: jnp.asarray(a.astype(np.float32))
    return f(intra), f(xi_f), f(xi_b), f(zeta_f), f(zeta_b), dec_f, dec_b


def _f32(a):
    return jnp.asarray(np.asarray(a).astype(np.float32))


def _rotary_tables(seq, tm):
    half = HEAD_DIM // 2
    inv = ROPE_BASE ** (-np.arange(half, dtype=np.float64) / half)
    inv = np.concatenate([inv, inv])
    sign = np.concatenate([-np.ones(half), np.ones(half)])
    base = np.arange(0, seq, tm, dtype=np.float64)[:, None] * inv[None, :]
    local = np.arange(tm, dtype=np.float64)[:, None] * inv[None, :]
    base_rows = np.stack([np.cos(base), np.sin(base), sign * np.cos(base), sign * np.sin(base)], axis=1)
    return _f32(base_rows), _f32(np.stack([np.cos(local), np.sin(local)], axis=0))


def _dft_stage_a(n1):
    k = np.arange(n1, dtype=np.int64)
    ang = 2.0 * np.pi * ((k[:Y_K1, None] * k[None, :]) % n1) / n1
    base = np.stack([np.cos(ang), -np.sin(ang)], axis=1).reshape(2 * Y_K1, n1) * (n1 ** -0.5)
    even, odd = np.zeros((2 * Y_K1, 2 * n1)), np.zeros((2 * Y_K1, 2 * n1))
    even[:, 0::2] = base
    odd[:, 1::2] = base
    return _f32(even).astype(BF16), _f32(odd).astype(BF16)


def _dft_stage_b(n1, n2):
    reps = MXU_DIM // (2 * n2)
    seq = n1 * n2
    idx = np.arange(MXU_DIM, dtype=np.int64)
    row_c, row_r, row_k = idx // (reps * n2), (idx // n2) % reps, idx % n2
    col_r, col_s, col_c = idx // (2 * n2), (idx // 2) % n2, idx % 2
    alpha = 2.0 * np.pi * ((row_k[:, None] * col_s[None, :]) % n2) / n2
    ca, sa = np.cos(alpha), np.sin(alpha)
    c_out, c_in = row_c[:, None], col_c[None, :]
    p_mat = np.where(c_out == c_in, ca, np.where(c_out < c_in, sa, -sa))
    q_mat = np.where(c_out == c_in, -sa, np.where(c_out < c_in, ca, -ca))
    keep = (row_r[:, None] == col_r[None, :]) * ((n2 * GROUP_DIM) ** -0.5)
    blk = np.arange(n1 // reps, dtype=np.int64)[:, None]
    k1 = blk * reps + col_r[None, :]
    beta = 2.0 * np.pi * ((col_s[None, :] * k1) % seq) / seq
    conj = np.where((k1 >= Y_K1) & (col_c[None, :] == 1), -1.0, 1.0)
    return (_f32(p_mat * keep), _f32(q_mat * keep),
            _f32((np.cos(beta) * conj)[:, None, :]), _f32((np.sin(beta) * conj)[:, None, :]))


def _channel_dft():
    c = np.arange(GROUP_DIM, dtype=np.int64)
    ang = 2.0 * np.pi * ((c[:, None] * c[None, :]) % GROUP_DIM) / GROUP_DIM
    return jnp.asarray(np.concatenate([np.cos(ang), np.sin(ang)], axis=0).astype(np.float32))


def _mod_kernel(c_ref, w_ref, b_ref, *refs):
    c = c_ref[...]
    s = c * jax.nn.sigmoid(c)
    w = w_ref[...]
    s_hi, w_hi = s.astype(BF16), w.astype(BF16)
    s_lo, w_lo = (s - s_hi.astype(F32)).astype(BF16), (w - w_hi.astype(F32)).astype(BF16)
    dot = functools.partial(jnp.dot, preferred_element_type=F32)
    n_cast = (len(refs) - 1) // 2
    o_ref = refs[n_cast]
    o_ref[...] = dot(s_hi, w_hi) + (dot(s_hi, w_lo) + dot(s_lo, w_hi)) + b_ref[...]
    for src, dst in zip(refs[:n_cast], refs[n_cast + 1:]):
        dst[...] = src[...].astype(BF16)


def _modulation_and_casts(c, ada_w, ada_b, weights):
    rows = c.shape[0]
    steps = MOD_STEPS
    tn = N_MOD * D_MODEL // steps
    row_block = lambda w: pl.BlockSpec((w.shape[0] // steps, w.shape[1]), lambda n: (n, 0))
    out = pl.pallas_call(
        _mod_kernel,
        grid=(steps,),
        in_specs=[pl.BlockSpec((rows, D_MODEL), lambda n: (0, 0)),
                  pl.BlockSpec((D_MODEL, tn), lambda n: (0, n)),
                  pl.BlockSpec((1, tn), lambda n: (0, n))] + [row_block(w) for w in weights],
        out_specs=[pl.BlockSpec((rows, tn), lambda n: (0, n))] + [row_block(w) for w in weights],
        out_shape=[jax.ShapeDtypeStruct((rows, N_MOD * D_MODEL), F32)]
                  + [jax.ShapeDtypeStruct(w.shape, BF16) for w in weights],
        compiler_params=_params(("arbitrary",)),
        name="adaln_mod",
    )(c, ada_w, ada_b.reshape(1, -1), *weights)
    return out[0], out[1:]


def _fnet_weight_kernel(cs_ref, w_ref, o_ref):
    o_ref[0] = jnp.dot(cs_ref[...], w_ref[0], preferred_element_type=F32,
                       precision=lax.Precision.HIGHEST).astype(BF16)


def _fnet_channel_weights(w_fnet):
    return pl.pallas_call(
        _fnet_weight_kernel,
        grid=(FNET_GROUPS,),
        in_specs=[pl.BlockSpec((2 * GROUP_DIM, GROUP_DIM), lambda g: (0, 0)),
                  pl.BlockSpec((1, GROUP_DIM, GROUP_DIM), lambda g: (g, 0, 0))],
        out_specs=pl.BlockSpec((1, 2 * GROUP_DIM, GROUP_DIM), lambda g: (g, 0, 0)),
        out_shape=jax.ShapeDtypeStruct((FNET_GROUPS, 2 * GROUP_DIM, GROUP_DIM), BF16),
        compiler_params=_params(("arbitrary",)),
        name="fnet_channel_weights",
    )(_channel_dft(), w_fnet)


def _rms(x):
    return x * lax.rsqrt(jnp.mean(x * x, axis=-1, keepdims=True) + EPS)


def _kv_outer(k, vz):
    return lax.dot_general(k, vz, (((0,), (0,)), ((), ())), preferred_element_type=F32)


def _inproj_kernel(x_ref, mod_ref, gain_ref, w_ref, rbase_ref, rlocal_ref, zeta_ref, *refs, n_chunks, dec_b, n_cast):
    u_ref, q_ref, k_ref, v_ref, g_ref, sb_ref = refs[n_cast:n_cast + 6]
    state_ref = refs[-1]
    for src, dst in zip(refs[:n_cast], refs[n_cast + 6:-1]):
        dst[...] = src[...].astype(BF16)

    @pl.when(pl.program_id(1) == 0)
    def _():
        state_ref[...] = jnp.zeros_like(state_ref)

    x = x_ref[0]
    a = gain_ref[...] * (1.0 + mod_ref[0, 1:2, :])
    h = (_rms(x) * a + mod_ref[0, 0:1, :]).astype(BF16)
    proj = jnp.dot(h, w_ref[...], preferred_element_type=F32)
    u_ref[0] = proj[:, :FNET_WIDTH].astype(BF16)
    cos_l, sin_l = rlocal_ref[0], rlocal_ref[1]
    cos = rbase_ref[0, 0:1, :] * cos_l - rbase_ref[0, 1:2, :] * sin_l
    sin = rbase_ref[0, 3:4, :] * cos_l + rbase_ref[0, 2:3, :] * sin_l
    for hd in range(RET_HEADS):
        lo = hd * HEAD_DIM
        for off, ref in ((FNET_WIDTH, q_ref), (FNET_WIDTH + RET_WIDTH, k_ref)):
            t = proj[:, off + lo:off + lo + HEAD_DIM]
            ref[0, :, lo:lo + HEAD_DIM] = (t * cos + pltpu.roll(t, HEAD_DIM // 2, 1) * sin).astype(BF16)
    v_ref[0] = proj[:, FNET_WIDTH + 2 * RET_WIDTH:FNET_WIDTH + 3 * RET_WIDTH].astype(BF16)
    g_ref[0] = proj[:, FNET_WIDTH + 3 * RET_WIDTH:].astype(BF16)

    for c in reversed(range(n_chunks)):
        rows = pl.ds(c * CHUNK, CHUNK)
        for hd in range(RET_HEADS):
            lanes = slice(hd * HEAD_DIM, (hd + 1) * HEAD_DIM)
            sb_ref[0, c, hd] = state_ref[hd].astype(BF16)
            vz = (v_ref[0, rows, lanes].astype(F32) * zeta_ref[hd]).astype(BF16)
            state_ref[hd] = state_ref[hd] * dec_b[hd] + _kv_outer(k_ref[0, rows, lanes], vz)


def _inproj(x, mod, gain, w_in, zeta_b, dec_b, casts=()):
    b, s, _ = x.shape
    tm = min(TM_INPROJ, s)
    nt = s // tm
    nc = tm // CHUNK
    rot_base, rot_local = _rotary_tables(s, tm)
    rev = lambda bi, ti: (bi, nt - 1 - ti, 0)
    tok = lambda width: pl.BlockSpec((1, tm, width), rev)
    const = lambda shape: pl.BlockSpec(shape, lambda bi, ti: (0,) * len(shape))
    act = jax.ShapeDtypeStruct((b, s, RET_WIDTH), BF16)
    states = pl.BlockSpec((1, nc, RET_HEADS, HEAD_DIM, HEAD_DIM), lambda bi, ti: (bi, nt - 1 - ti, 0, 0, 0))
    state_shape = jax.ShapeDtypeStruct((b, s // CHUNK, RET_HEADS, HEAD_DIM, HEAD_DIM), BF16)
    cast_specs = [pl.BlockSpec((w.shape[0] // (b * nt), w.shape[1]), lambda bi, ti: (bi * nt + ti, 0)) for w in casts]
    out = pl.pallas_call(
        functools.partial(_inproj_kernel, n_chunks=nc, dec_b=dec_b, n_cast=len(casts)),
        grid=(b, nt),
        in_specs=[tok(D_MODEL),
                  pl.BlockSpec((1, N_MOD, D_MODEL), lambda bi, ti: (bi, 0, 0)),
                  const((1, D_MODEL)),
                  const((D_MODEL, IN_WIDTH)),
                  pl.BlockSpec((1, 4, HEAD_DIM), lambda bi, ti: (nt - 1 - ti, 0, 0)),
                  const((2, tm, HEAD_DIM)),
                  const((RET_HEADS, CHUNK, HEAD_DIM))] + cast_specs,
        out_specs=[tok(FNET_WIDTH), tok(RET_WIDTH), tok(RET_WIDTH), tok(RET_WIDTH), tok(RET_WIDTH),
                   states] + cast_specs,
        out_shape=[jax.ShapeDtypeStruct((b, s, FNET_WIDTH), BF16), act, act, act, act,
                   state_shape] + [jax.ShapeDtypeStruct(w.shape, BF16) for w in casts],
        scratch_shapes=[pltpu.VMEM((RET_HEADS, HEAD_DIM, HEAD_DIM), F32)],
        compiler_params=_params(("parallel", "arbitrary")),
        name="inproj",
    )(x, mod, gain, w_in, rot_base, rot_local, zeta_b, *casts)
    return out[:6], out[6:]


def _group_lanes():
    return [(g, slice(g * GROUP_DIM, (g + 1) * GROUP_DIM)) for g in range(FNET_GROUPS)]


def _unpack_rows(load, dst_ref, slab=0):
    units = dst_ref.shape[2] // BF16_ROWS
    half = units * F32_ROWS
    for g, lanes in _group_lanes():
        rows = load(lanes).reshape(units, BF16_ROWS, GROUP_DIM).astype(F32)
        dst_ref[slab, g, 0:half, :] = rows[:, :F32_ROWS, :].reshape(half, GROUP_DIM)
        dst_ref[slab, g, half:, :] = rows[:, F32_ROWS:, :].reshape(half, GROUP_DIM)


def _strided_rows(src_ref, row, slab=0):
    units = src_ref.shape[2] // BF16_ROWS
    start = (row // F32_ROWS) * (units * F32_ROWS) + row % F32_ROWS
    rows = pl.ds(start, units, stride=F32_ROWS)
    return jnp.concatenate([src_ref[slab, g, rows, :] for g in range(FNET_GROUPS)], axis=1).astype(BF16)


def _word_rows(src_ref, lead, word, units):
    rows = pl.ds(word, units, stride=WORD_ROWS)
    return jnp.concatenate([pltpu.bitcast(src_ref[lead, g, rows, :], BF16) for g in range(HALF_GROUPS)], axis=1)


def _fnet_kernel(x_ref, even_ref, odd_ref, p_ref, q_ref, cb_ref, sb_ref, m_ref, o_ref, xw_ref, yw_ref, *, n2, reps):
    n1 = DFT_N1
    half = MXU_DIM // 2
    groups = [slice(g * GROUP_DIM, (g + 1) * GROUP_DIM) for g in range(HALF_GROUPS)]

    def aligned(start, step):
        return start * step if isinstance(start, int) else pl.multiple_of(start * step, step)

    def stage_a(sl, slot):
        s2_base = aligned(sl, BF16_ROWS)
        for g, lanes in enumerate(groups):
            block = x_ref[0, :, pl.ds(s2_base, BF16_ROWS), lanes]
            xw_ref[slot, g] = pltpu.bitcast(block.reshape(n1 * BF16_ROWS, GROUP_DIM), jnp.uint32)
        for i in range(WORD_ROWS):
            pair = _word_rows(xw_ref, slot, i, n1)
            for parity, mat_ref in enumerate((even_ref, odd_ref)):
                y = jnp.dot(mat_ref[...], pair, preferred_element_type=F32).astype(BF16)
                words = pltpu.bitcast(y, jnp.uint32)
                row0 = aligned(s2_base + 2 * i + parity, WORD_ROWS)
                for ks in range(Y_K1 // WORD_ROWS):
                    for g, lanes in enumerate(groups):
                        yw_ref[ks, g, pl.ds(row0, WORD_ROWS), :] = words[ks * WORD_ROWS:(ks + 1) * WORD_ROWS, lanes]

    def stage_b(blk, slot):
        t = (p_ref[...] * cb_ref[blk] + q_ref[...] * sb_ref[blk]).astype(BF16)
        pieces = []
        for r in range(reps):
            k1 = blk * reps + r
            src = (n1 - k1 if k1 >= Y_K1 else k1) if isinstance(k1, int) else jnp.where(k1 >= Y_K1, n1 - k1, k1)
            pieces.append(_word_rows(yw_ref, src // WORD_ROWS, src % WORD_ROWS, n2))
        yb = pieces[0] if reps == 1 else jnp.concatenate(pieces, axis=0)
        z = jnp.dot(t, yb, preferred_element_type=F32)
        for g, lanes in enumerate(groups):
            zz = jnp.concatenate([z[:half, lanes], z[half:, lanes]], axis=1).astype(BF16)
            res = jnp.dot(zz, m_ref[g], preferred_element_type=F32)
            for r in range(reps):
                o_ref[0, blk * reps + r, :, lanes] = res[r * n2:(r + 1) * n2].astype(BF16)

    _for_each(n2 // BF16_ROWS, stage_a, STAGE_A_UNROLL)
    _for_each(n1 // reps, stage_b, FNET_UNROLL)


def _for_each(count, body, group):
    if count <= group:
        for i in range(count):
            body(i, i)
        return

    def step(j, carry):
        for u in range(group):
            body(j * group + u, u)
        return carry

    lax.fori_loop(0, count // group, step, 0)


def _fnet(u, m_chan):
    b, s, _ = u.shape
    n1 = DFT_N1
    n2 = s // n1
    assert s == n1 * n2 and MXU_DIM % (2 * n2) == 0 and n2 % BF16_ROWS == 0, s
    reps = MXU_DIM // (2 * n2)
    p_mat, q_mat, cos_b, sin_b = _dft_stage_b(n1, n2)
    const = lambda shape: pl.BlockSpec(shape, lambda bi, hi: (0,) * len(shape))
    tokens = pl.BlockSpec((1, n1, n2, MXU_DIM), lambda bi, hi: (bi, 0, 0, hi))
    return pl.pallas_call(
        functools.partial(_fnet_kernel, n2=n2, reps=reps),
        grid=(b, FNET_WIDTH // MXU_DIM),
        in_specs=[tokens, const((2 * Y_K1, 2 * n1)), const((2 * Y_K1, 2 * n1)),
                  const((MXU_DIM, MXU_DIM)), const((MXU_DIM, MXU_DIM)),
                  const((n1 // reps, 1, MXU_DIM)), const((n1 // reps, 1, MXU_DIM)),
                  pl.BlockSpec((HALF_GROUPS, 2 * GROUP_DIM, GROUP_DIM), lambda bi, hi: (hi, 0, 0))],
        out_specs=tokens,
        out_shape=jax.ShapeDtypeStruct((b, n1, n2, FNET_WIDTH), BF16),
        scratch_shapes=[pltpu.VMEM((STAGE_A_UNROLL, HALF_GROUPS, n1 * WORD_ROWS, GROUP_DIM), jnp.uint32),
                        pltpu.VMEM((Y_K1 // WORD_ROWS, HALF_GROUPS, n2 * WORD_ROWS, GROUP_DIM), jnp.uint32)],
        compiler_params=_params(("parallel", "parallel")),
        name="fnet",
    )(u.reshape(b, n1, n2, FNET_WIDTH), *_dft_stage_a(n1), p_mat, q_mat, cos_b, sin_b, m_chan)


def _mixer_kernel(mod_ref, f_ref, q_ref, k_ref, v_ref, g_ref, sb_ref, w_ref,
                  intra_ref, xif_ref, xib_ref, zeta_ref, o_ref, state_ref, r_ref, fs_ref, *, n_chunks, dec_f):
    @pl.when(pl.program_id(1) == 0)
    def _():
        state_ref[...] = jnp.zeros_like(state_ref)

    tiles = BF16_ROWS // n_chunks
    sub = pl.program_id(1) % tiles

    @pl.when(sub == 0)
    def _():
        _unpack_rows(lambda lanes: f_ref[0, :, :, lanes], fs_ref)

    f_rows = [_strided_rows(fs_ref, sub * n_chunks + c) for c in range(n_chunks)]
    mix = jnp.dot(jnp.concatenate(f_rows, axis=0), w_ref[:FNET_WIDTH, :], preferred_element_type=F32)

    for c in range(n_chunks):
        rows = pl.ds(c * CHUNK, CHUNK)
        for hd in range(RET_HEADS):
            lanes = slice(hd * HEAD_DIM, (hd + 1) * HEAD_DIM)
            q = q_ref[0, rows, lanes]
            k = k_ref[0, rows, lanes]
            v = v_ref[0, rows, lanes]
            qf = q.astype(F32)
            scores = lax.dot_general(q, k, (((1,), (1,)), ((), ())), preferred_element_type=F32)
            r = jnp.dot((scores * intra_ref[hd]).astype(BF16), v, preferred_element_type=F32)
            r += jnp.dot((qf * xif_ref[hd]).astype(BF16), state_ref[hd].astype(BF16),
                         preferred_element_type=F32)
            r += jnp.dot((qf * xib_ref[hd]).astype(BF16), sb_ref[0, c, hd], preferred_element_type=F32)
            vz = (v.astype(F32) * zeta_ref[hd]).astype(BF16)
            state_ref[hd] = state_ref[hd] * dec_f[hd] + _kv_outer(k, vz)
            gate = g_ref[0, rows, lanes].astype(F32)
            r_ref[rows, lanes] = (_rms(r) * (gate * jax.nn.sigmoid(gate))).astype(BF16)

    mix += jnp.dot(r_ref[...], w_ref[FNET_WIDTH:, :], preferred_element_type=F32)
    o_ref[0] = (mod_ref[0, 2:3, :] * mix).astype(BF16)


def _mixer(mod, f, q, k, v, g, sb, w_out, tables):
    intra, xi_f, xi_b, zeta_f, dec_f = tables
    b, s, _ = q.shape
    tm = min(TM_MIXER, s // 2)
    nc = tm // CHUNK
    assert f.shape[1] == CHUNK and BF16_ROWS % nc == 0 and (s // tm) % (BF16_ROWS // nc) == 0
    tiles = BF16_ROWS // nc
    tok = lambda width: pl.BlockSpec((1, tm, width), lambda bi, ti: (bi, ti, 0))
    const = lambda shape: pl.BlockSpec(shape, lambda bi, ti: (0,) * len(shape))
    tab = const((RET_HEADS, CHUNK, HEAD_DIM))
    states = pl.BlockSpec((1, nc, RET_HEADS, HEAD_DIM, HEAD_DIM), lambda bi, ti: (bi, ti, 0, 0, 0))
    return pl.pallas_call(
        functools.partial(_mixer_kernel, n_chunks=nc, dec_f=dec_f),
        grid=(b, s // tm),
        in_specs=[pl.BlockSpec((1, N_MOD, D_MODEL), lambda bi, ti: (bi, 0, 0)),
                  pl.BlockSpec((1, CHUNK, BF16_ROWS, FNET_WIDTH), lambda bi, ti: (bi, 0, ti // tiles, 0)),
                  tok(RET_WIDTH), tok(RET_WIDTH), tok(RET_WIDTH), tok(RET_WIDTH),
                  states,
                  const((D_MODEL, D_MODEL)), tab, tab, tab, tab],
        out_specs=tok(D_MODEL),
        out_shape=jax.ShapeDtypeStruct((b, s, D_MODEL), BF16),
        scratch_shapes=[pltpu.VMEM((RET_HEADS, HEAD_DIM, HEAD_DIM), F32),
                        pltpu.VMEM((tm, RET_WIDTH), BF16),
                        pltpu.VMEM((1, FNET_GROUPS, CHUNK * BF16_ROWS, GROUP_DIM), F32)],
        compiler_params=_params(("parallel", "arbitrary")),
        name="mixer",
    )(mod, f, q, k, v, g, sb, w_out, intra, xi_f, xi_b, zeta_f)


def _mlp_kernel(x_ref, d_ref, mod_ref, gain_ref, w1_ref, w2_ref, gfin_ref, o_ref):
    x = x_ref[0] + d_ref[0].astype(F32)
    a = gain_ref[...] * (1.0 + mod_ref[0, 4:5, :])
    h = (_rms(x) * a + mod_ref[0, 3:4, :]).astype(BF16)
    out = None
    for part in range(MLP_FF_PARTS):
        cols = slice(part * (D_FF // MLP_FF_PARTS), (part + 1) * (D_FF // MLP_FF_PARTS))
        mid = jnp.dot(h, w1_ref[:, cols], preferred_element_type=F32)
        mid = jnp.square(jnp.maximum(mid, 0.0)).astype(BF16)
        slab = jnp.dot(mid, w2_ref[cols, :], preferred_element_type=F32)
        out = slab if out is None else out + slab
    y = x + mod_ref[0, 5:6, :] * out
    o_ref[0] = _rms(y) * gfin_ref[...]


def _mlp(x, delta, mod, gain, w1, w2, gain_final):
    b, s, _ = x.shape
    tm = min(TM_MLP, s)
    tok = pl.BlockSpec((1, tm, D_MODEL), lambda bi, ti: (bi, ti, 0))
    const = lambda shape: pl.BlockSpec(shape, lambda bi, ti: (0,) * len(shape))
    weight = lambda shape: pl.BlockSpec(shape, lambda bi, ti: (0,) * len(shape), pipeline_mode=pl.Buffered(1))
    return pl.pallas_call(
        _mlp_kernel,
        grid=(b, s // tm),
        in_specs=[tok, tok, pl.BlockSpec((1, N_MOD, D_MODEL), lambda bi, ti: (bi, 0, 0)),
                  const((1, D_MODEL)), weight((D_MODEL, D_FF)), weight((D_FF, D_MODEL)), const((1, D_MODEL))],
        out_specs=tok,
        out_shape=jax.ShapeDtypeStruct((b, s, D_MODEL), F32),
        compiler_params=_params(("parallel", "parallel")),
        name="mlp",
    )(x, delta, mod, gain, w1, w2, gain_final)


def _trunk(x, mod, weights, tables):
    gain_mix, w_in, m_chan, w_out, gain_mlp, w1, w2, gain_final = weights
    intra, xi_f, xi_b, zeta_f, zeta_b, dec_f, dec_b = tables
    casts = (w_out, w1, w2) if w1.dtype != BF16 else ()
    (u, q, k, v, g, sb), cast = _inproj(x, mod, gain_mix, w_in, zeta_b, dec_b, casts)
    if casts:
        w_out, w1, w2 = cast
    f = _fnet(u, m_chan)
    delta = _mixer(mod, f, q, k, v, g, sb, w_out, (intra, xi_f, xi_b, zeta_f, dec_f))
    y = _mlp(x, delta, mod, gain_mlp, w1, w2, gain_final)
    return y, (gain_mix, w_in, m_chan, w_out, gain_mlp, w1, w2, gain_final)


def kernel(x_prompt, x_sample, c_prompt, c_sample, ada_w, ada_b, norm_mix, w_in, w_fnet, w_out,
           norm_mlp, w_mlp_in, w_mlp_out, norm_final):
    assert ada_w.shape[0] == 1, "one encoder layer"
    bp, bs = x_prompt.shape[0], x_sample.shape[0]
    c_all = jnp.concatenate([c_prompt, c_sample], axis=0)
    mod, (w_in_b,) = _modulation_and_casts(c_all, ada_w[0], ada_b[0], (w_in[0],))
    mod = mod.reshape(bp + bs, N_MOD, D_MODEL)
    weights = (norm_mix[0].reshape(1, -1), w_in_b, _fnet_channel_weights(w_fnet[0]),
               w_out[0], norm_mlp[0].reshape(1, -1), w_mlp_in[0], w_mlp_out[0], norm_final.reshape(1, -1))
    tables = _retention_tables()
    y_prompt, weights = _trunk(x_prompt, mod[:bp], weights, tables)
    y_sample, _ = _trunk(x_sample, mod[bp:], weights, tables)
    return (y_prompt, y_sample)
```

```python
import functools

import numpy as np
import jax
import jax.numpy as jnp
from jax import lax
from jax.experimental import pallas as pl
from jax.experimental.pallas import tpu as pltpu

F32 = jnp.float32
BF16 = jnp.bfloat16

D_MODEL = 1024
FNET_WIDTH = 512
FNET_GROUPS = 4
GROUP_DIM = 128
RET_WIDTH = 512
RET_HEADS = 4
HEAD_DIM = 128
D_FF = 4096
CHUNK = 128
ROPE_BASE = 10000.0
EPS = 1e-6
N_MOD = 6
IN_WIDTH = FNET_WIDTH + 4 * RET_WIDTH
Q_OFF, K_OFF, V_OFF, G_OFF, U_OFF = (i * RET_WIDTH for i in range(5))
DECAY_OFFSET_FWD = 0.0
DECAY_OFFSET_BWD = 0.5

DFT_N1 = 128
MXU_DIM = 256
BF16_ROWS = 16
F32_ROWS = 8
WORD_ROWS = BF16_ROWS // 2
HALF_GROUPS = MXU_DIM // GROUP_DIM
Y_K1 = 72
FNET_UNROLL = 32
STAGE_A_UNROLL = 2
VMEM_LIMIT = 56 * 1024 * 1024

TM_INPROJ = 1024
TM_MIXER = 2048
TM_MLP = 1024
MLP_FF_PARTS = 4
MOD_STEPS = 8


def _params(semantics):
    return pltpu.CompilerParams(dimension_semantics=semantics, vmem_limit_bytes=VMEM_LIMIT)


def _retention_tables():
    scale = HEAD_DIM ** -0.5
    h = np.arange(RET_HEADS, dtype=np.float64)
    lg_f = np.log1p(-np.exp2(-5.0 - DECAY_OFFSET_FWD - h))[:, None, None]
    lg_b = np.log1p(-np.exp2(-5.0 - DECAY_OFFSET_BWD - h))[:, None, None]
    i = np.arange(CHUNK, dtype=np.float64)[None, :, None]
    j = np.arange(CHUNK, dtype=np.float64)[None, None, :]
    ones = np.ones((1, 1, HEAD_DIM))
    intra = scale * np.where(i >= j, np.exp(lg_f * np.maximum(i - j, 0.0)), np.exp(lg_b * np.maximum(j - i, 0.0)))
    xi_f = np.exp(lg_f * (i + 1.0)) * ones
    xi_b = np.exp(lg_b * (CHUNK - i)) * ones
    zeta_f = scale * np.exp(lg_f * (CHUNK - 1.0 - i)) * ones
    zeta_b = scale * np.exp(lg_b * i) * ones
    dec_f = tuple(float(v) for v in np.exp(lg_f[:, 0, 0] * CHUNK))
    dec_b = tuple(float(v) for v in np.exp(lg_b[:, 0, 0] * CHUNK))
    f = lambda a: jnp.asarray(a.astype(np.float32))
    return f(intra), f(xi_f), f(xi_b), f(zeta_f), f(zeta_b), dec_f, dec_b


def _f32(a):
    return jnp.asarray(np.asarray(a).astype(np.float32))


def _rotary_tables(seq, tm):
    half = HEAD_DIM // 2
    inv = ROPE_BASE ** (-np.arange(half, dtype=np.float64) / half)
    inv = np.concatenate([inv, inv])
    sign = np.concatenate([-np.ones(half), np.ones(half)])
    base = np.arange(0, seq, tm, dtype=np.float64)[:, None] * inv[None, :]
    local = np.arange(tm, dtype=np.float64)[:, None] * inv[None, :]
    base_rows = np.stack([np.cos(base), np.sin(base), sign * np.cos(base), sign * np.sin(base)], axis=1)
    return _f32(base_rows), _f32(np.stack([np.cos(local), np.sin(local)], axis=0))


def _dft_stage_a(n1):
    k = np.arange(n1, dtype=np.int64)
    ang = 2.0 * np.pi * ((k[:Y_K1, None] * k[None, :]) % n1) / n1
    base = np.stack([np.cos(ang), -np.sin(ang)], axis=1).reshape(2 * Y_K1, n1) * (n1 ** -0.5)
    even, odd = np.zeros((2 * Y_K1, 2 * n1)), np.zeros((2 * Y_K1, 2 * n1))
    even[:, 0::2] = base
    odd[:, 1::2] = base
    return _f32(even).astype(BF16), _f32(odd).astype(BF16)


def _dft_stage_b(n1, n2):
    reps = MXU_DIM // (2 * n2)
    seq = n1 * n2
    idx = np.arange(MXU_DIM, dtype=np.int64)
    row_c, row_r, row_k = idx // (reps * n2), (idx // n2) % reps, idx % n2
    col_r, col_s, col_c = idx // (2 * n2), (idx // 2) % n2, idx % 2
    alpha = 2.0 * np.pi * ((row_k[:, None] * col_s[None, :]) % n2) / n2
    ca, sa = np.cos(alpha), np.sin(alpha)
    c_out, c_in = row_c[:, None], col_c[None, :]
    p_mat = np.where(c_out == c_in, ca, np.where(c_out < c_in, sa, -sa))
    q_mat = np.where(c_out == c_in, -sa, np.where(c_out < c_in, ca, -ca))
    keep = (row_r[:, None] == col_r[None, :]) * ((n2 * GROUP_DIM) ** -0.5)
    blk = np.arange(n1 // reps, dtype=np.int64)[:, None]
    k1 = blk * reps + col_r[None, :]
    beta = 2.0 * np.pi * ((col_s[None, :] * k1) % seq) / seq
    conj = np.where((k1 >= Y_K1) & (col_c[None, :] == 1), -1.0, 1.0)
    return (_f32(p_mat * keep), _f32(q_mat * keep),
            _f32((np.cos(beta) * conj)[:, None, :]), _f32((np.sin(beta) * conj)[:, None, :]))


def _channel_dft():
    c = np.arange(GROUP_DIM, dtype=np.int64)
    ang = 2.0 * np.pi * ((c[:, None] * c[None, :]) % GROUP_DIM) / GROUP_DIM
    return jnp.asarray(np.concatenate([np.cos(ang), np.sin(ang)], axis=0).astype(np.float32))


def _mod_kernel(c_ref, w_ref, b_ref, *refs):
    c = c_ref[...]
    s = c * jax.nn.sigmoid(c)
    w = w_ref[...]
    s_hi, w_hi = s.astype(BF16), w.astype(BF16)
    s_lo, w_lo = (s - s_hi.astype(F32)).astype(BF16), (w - w_hi.astype(F32)).astype(BF16)
    dot = functools.partial(jnp.dot, preferred_element_type=F32)
    n_cast = (len(refs) - 1) // 2
    o_ref = refs[n_cast]
    o_ref[...] = dot(s_hi, w_hi) + (dot(s_hi, w_lo) + dot(s_lo, w_hi)) + b_ref[...]
    for src, dst in zip(refs[:n_cast], refs[n_cast + 1:]):
        dst[...] = src[...].astype(BF16)


def _modulation_and_casts(c, ada_w, ada_b, weights):
    rows = c.shape[0]
    steps = MOD_STEPS
    tn = N_MOD * D_MODEL // steps
    row_block = lambda w: pl.BlockSpec((w.shape[0] // steps, w.shape[1]), lambda n: (n, 0))
    out = pl.pallas_call(
        _mod_kernel,
        grid=(steps,),
        in_specs=[pl.BlockSpec((rows, D_MODEL), lambda n: (0, 0)),
                  pl.BlockSpec((D_MODEL, tn), lambda n: (0, n)),
                  pl.BlockSpec((1, tn), lambda n: (0, n))] + [row_block(w) for w in weights],
        out_specs=[pl.BlockSpec((rows, tn), lambda n: (0, n))] + [row_block(w) for w in weights],
        out_shape=[jax.ShapeDtypeStruct((rows, N_MOD * D_MODEL), F32)]
                  + [jax.ShapeDtypeStruct(w.shape, BF16) for w in weights],
        compiler_params=_params(("arbitrary",)),
        name="adaln_mod",
    )(c, ada_w, ada_b.reshape(1, -1), *weights)
    return out[0], out[1:]


def _fnet_weight_kernel(cs_ref, w_ref, o_ref):
    o_ref[0] = jnp.dot(cs_ref[...], w_ref[0], preferred_element_type=F32,
                       precision=lax.Precision.HIGHEST).astype(BF16)


def _fnet_channel_weights(w_fnet):
    return pl.pallas_call(
        _fnet_weight_kernel,
        grid=(FNET_GROUPS,),
        in_specs=[pl.BlockSpec((2 * GROUP_DIM, GROUP_DIM), lambda g: (0, 0)),
                  pl.BlockSpec((1, GROUP_DIM, GROUP_DIM), lambda g: (g, 0, 0))],
        out_specs=pl.BlockSpec((1, 2 * GROUP_DIM, GROUP_DIM), lambda g: (g, 0, 0)),
        out_shape=jax.ShapeDtypeStruct((FNET_GROUPS, 2 * GROUP_DIM, GROUP_DIM), BF16),
        compiler_params=_params(("arbitrary",)),
        name="fnet_channel_weights",
    )(_channel_dft(), w_fnet)


def _rms(x):
    return x * lax.rsqrt(jnp.mean(x * x, axis=-1, keepdims=True) + EPS)


def _kv_outer(k, vz):
    return lax.dot_general(k, vz, (((0,), (0,)), ((), ())), preferred_element_type=F32)


def _inproj_kernel(x_ref, mod_ref, gain_ref, w_ref, rbase_ref, rlocal_ref, zeta_ref, *refs, n_chunks, dec_b, n_cast):
    act_ref, sb_ref = refs[n_cast:n_cast + 2]
    state_ref = refs[-1]
    for src, dst in zip(refs[:n_cast], refs[n_cast + 2:-1]):
        dst[...] = src[...].astype(BF16)

    @pl.when(pl.program_id(1) == 0)
    def _():
        state_ref[...] = jnp.zeros_like(state_ref)

    x = x_ref[0]
    a = gain_ref[...] * (1.0 + mod_ref[0, 1:2, :])
    h = (_rms(x) * a + mod_ref[0, 0:1, :]).astype(BF16)
    proj = jnp.dot(h, w_ref[...], preferred_element_type=F32)
    act_ref[0, :, U_OFF:U_OFF + FNET_WIDTH] = proj[:, :FNET_WIDTH].astype(BF16)
    cos_l, sin_l = rlocal_ref[0], rlocal_ref[1]
    cos = rbase_ref[0, 0:1, :] * cos_l - rbase_ref[0, 1:2, :] * sin_l
    sin = rbase_ref[0, 3:4, :] * cos_l + rbase_ref[0, 2:3, :] * sin_l
    for hd in range(RET_HEADS):
        lo = hd * HEAD_DIM
        for off, dst in ((FNET_WIDTH, Q_OFF), (FNET_WIDTH + RET_WIDTH, K_OFF)):
            t = proj[:, off + lo:off + lo + HEAD_DIM]
            act_ref[0, :, dst + lo:dst + lo + HEAD_DIM] = (
                t * cos + pltpu.roll(t, HEAD_DIM // 2, 1) * sin).astype(BF16)
    act_ref[0, :, V_OFF:U_OFF] = proj[:, FNET_WIDTH + 2 * RET_WIDTH:].astype(BF16)

    for c in reversed(range(n_chunks)):
        rows = pl.ds(c * CHUNK, CHUNK)
        for hd in range(RET_HEADS):
            lo = hd * HEAD_DIM
            sb_ref[0, c, hd] = state_ref[hd].astype(BF16)
            v = act_ref[0, rows, V_OFF + lo:V_OFF + lo + HEAD_DIM]
            vz = (v.astype(F32) * zeta_ref[hd]).astype(BF16)
            state_ref[hd] = state_ref[hd] * dec_b[hd] + _kv_outer(act_ref[0, rows, K_OFF + lo:K_OFF + lo + HEAD_DIM], vz)


def _inproj(x, mod, gain, w_in, zeta_b, dec_b, casts=()):
    b, s, _ = x.shape
    tm = min(TM_INPROJ, s)
    nt = s // tm
    nc = tm // CHUNK
    rot_base, rot_local = _rotary_tables(s, tm)
    rev = lambda bi, ti: (bi, nt - 1 - ti, 0)
    tok = lambda width: pl.BlockSpec((1, tm, width), rev)
    const = lambda shape: pl.BlockSpec(shape, lambda bi, ti: (0,) * len(shape))
    states = pl.BlockSpec((1, nc, RET_HEADS, HEAD_DIM, HEAD_DIM), lambda bi, ti: (bi, nt - 1 - ti, 0, 0, 0))
    state_shape = jax.ShapeDtypeStruct((b, s // CHUNK, RET_HEADS, HEAD_DIM, HEAD_DIM), BF16)
    cast_specs = [pl.BlockSpec((w.shape[0] // (b * nt), w.shape[1]), lambda bi, ti: (bi * nt + ti, 0)) for w in casts]
    out = pl.pallas_call(
        functools.partial(_inproj_kernel, n_chunks=nc, dec_b=dec_b, n_cast=len(casts)),
        grid=(b, nt),
        in_specs=[tok(D_MODEL),
                  pl.BlockSpec((1, N_MOD, D_MODEL), lambda bi, ti: (bi, 0, 0)),
                  const((1, D_MODEL)),
                  const((D_MODEL, IN_WIDTH)),
                  pl.BlockSpec((1, 4, HEAD_DIM), lambda bi, ti: (nt - 1 - ti, 0, 0)),
                  const((2, tm, HEAD_DIM)),
                  const((RET_HEADS, CHUNK, HEAD_DIM))] + cast_specs,
        out_specs=[tok(IN_WIDTH), states] + cast_specs,
        out_shape=[jax.ShapeDtypeStruct((b, s, IN_WIDTH), BF16), state_shape]
                  + [jax.ShapeDtypeStruct(w.shape, BF16) for w in casts],
        scratch_shapes=[pltpu.VMEM((RET_HEADS, HEAD_DIM, HEAD_DIM), F32)],
        compiler_params=_params(("parallel", "arbitrary")),
        name="inproj",
    )(x, mod, gain, w_in, rot_base, rot_local, zeta_b, *casts)
    return out[:2], out[2:]


def _group_lanes():
    return [(g, slice(g * GROUP_DIM, (g + 1) * GROUP_DIM)) for g in range(FNET_GROUPS)]


def _unpack_rows(load, dst_ref, slab=0):
    units = dst_ref.shape[2] // BF16_ROWS
    half = units * F32_ROWS
    for g, lanes in _group_lanes():
        rows = load(lanes).reshape(units, BF16_ROWS, GROUP_DIM).astype(F32)
        dst_ref[slab, g, 0:half, :] = rows[:, :F32_ROWS, :].reshape(half, GROUP_DIM)
        dst_ref[slab, g, half:, :] = rows[:, F32_ROWS:, :].reshape(half, GROUP_DIM)


def _strided_rows(src_ref, row, slab=0):
    units = src_ref.shape[2] // BF16_ROWS
    start = (row // F32_ROWS) * (units * F32_ROWS) + row % F32_ROWS
    rows = pl.ds(start, units, stride=F32_ROWS)
    return jnp.concatenate([src_ref[slab, g, rows, :] for g in range(FNET_GROUPS)], axis=1).astype(BF16)


def _word_rows(src_ref, lead, word, units):
    rows = pl.ds(word, units, stride=WORD_ROWS)
    return jnp.concatenate([pltpu.bitcast(src_ref[lead, g, rows, :], BF16) for g in range(HALF_GROUPS)], axis=1)


def _fnet_kernel(x_ref, even_ref, odd_ref, p_ref, q_ref, cb_ref, sb_ref, m_ref, o_ref, xw_ref, yw_ref, *, n2, reps):
    n1 = DFT_N1
    half = MXU_DIM // 2
    groups = [slice(g * GROUP_DIM, (g + 1) * GROUP_DIM) for g in range(HALF_GROUPS)]

    def aligned(start, step):
        return start * step if isinstance(start, int) else pl.multiple_of(start * step, step)

    def stage_a(sl, slot):
        s2_base = aligned(sl, BF16_ROWS)
        for g, lanes in enumerate(groups):
            block = x_ref[0, :, pl.ds(s2_base, BF16_ROWS), lanes]
            xw_ref[slot, g] = pltpu.bitcast(block.reshape(n1 * BF16_ROWS, GROUP_DIM), jnp.uint32)
        for i in range(WORD_ROWS):
            pair = _word_rows(xw_ref, slot, i, n1)
            for parity, mat_ref in enumerate((even_ref, odd_ref)):
                y = jnp.dot(mat_ref[...], pair, preferred_element_type=F32).astype(BF16)
                words = pltpu.bitcast(y, jnp.uint32)
                row0 = aligned(s2_base + 2 * i + parity, WORD_ROWS)
                for ks in range(Y_K1 // WORD_ROWS):
                    for g, lanes in enumerate(groups):
                        yw_ref[ks, g, pl.ds(row0, WORD_ROWS), :] = words[ks * WORD_ROWS:(ks + 1) * WORD_ROWS, lanes]

    def stage_b(blk, slot):
        t = (p_ref[...] * cb_ref[blk] + q_ref[...] * sb_ref[blk]).astype(BF16)
        pieces = []
        for r in range(reps):
            k1 = blk * reps + r
            src = (n1 - k1 if k1 >= Y_K1 else k1) if isinstance(k1, int) else jnp.where(k1 >= Y_K1, n1 - k1, k1)
            pieces.append(_word_rows(yw_ref, src // WORD_ROWS, src % WORD_ROWS, n2))
        yb = pieces[0] if reps == 1 else jnp.concatenate(pieces, axis=0)
        z = jnp.dot(t, yb, preferred_element_type=F32)
        for g, lanes in enumerate(groups):
            zz = jnp.concatenate([z[:half, lanes], z[half:, lanes]], axis=1).astype(BF16)
            res = jnp.dot(zz, m_ref[g], preferred_element_type=F32)
            for r in range(reps):
                o_ref[0, blk * reps + r, :, lanes] = res[r * n2:(r + 1) * n2].astype(BF16)

    _for_each(n2 // BF16_ROWS, stage_a, STAGE_A_UNROLL)
    _for_each(n1 // reps, stage_b, FNET_UNROLL)


def _for_each(count, body, group):
    if count <= group:
        for i in range(count):
            body(i, i)
        return

    def step(j, carry):
        for u in range(group):
            body(j * group + u, u)
        return carry

    lax.fori_loop(0, count // group, step, 0)


def _fnet(act, m_chan):
    b, s, _ = act.shape
    n1 = DFT_N1
    n2 = s // n1
    assert s == n1 * n2 and MXU_DIM % (2 * n2) == 0 and n2 % BF16_ROWS == 0, s
    reps = MXU_DIM // (2 * n2)
    p_mat, q_mat, cos_b, sin_b = _dft_stage_b(n1, n2)
    const = lambda shape: pl.BlockSpec(shape, lambda bi, hi: (0,) * len(shape))
    tokens = pl.BlockSpec((1, n1, n2, MXU_DIM), lambda bi, hi: (bi, 0, 0, hi))
    fourier = pl.BlockSpec((1, n1, n2, MXU_DIM), lambda bi, hi: (bi, 0, 0, U_OFF // MXU_DIM + hi))
    return pl.pallas_call(
        functools.partial(_fnet_kernel, n2=n2, reps=reps),
        grid=(b, FNET_WIDTH // MXU_DIM),
        in_specs=[fourier, const((2 * Y_K1, 2 * n1)), const((2 * Y_K1, 2 * n1)),
                  const((MXU_DIM, MXU_DIM)), const((MXU_DIM, MXU_DIM)),
                  const((n1 // reps, 1, MXU_DIM)), const((n1 // reps, 1, MXU_DIM)),
                  pl.BlockSpec((HALF_GROUPS, 2 * GROUP_DIM, GROUP_DIM), lambda bi, hi: (hi, 0, 0))],
        out_specs=tokens,
        out_shape=jax.ShapeDtypeStruct((b, n1, n2, FNET_WIDTH), BF16),
        scratch_shapes=[pltpu.VMEM((STAGE_A_UNROLL, HALF_GROUPS, n1 * WORD_ROWS, GROUP_DIM), jnp.uint32),
                        pltpu.VMEM((Y_K1 // WORD_ROWS, HALF_GROUPS, n2 * WORD_ROWS, GROUP_DIM), jnp.uint32)],
        compiler_params=_params(("parallel", "parallel")),
        name="fnet",
    )(act.reshape(b, n1, n2, IN_WIDTH), *_dft_stage_a(n1), p_mat, q_mat, cos_b, sin_b, m_chan)


def _mixer_kernel(mod_ref, f_ref, a_ref, sb_ref, w_ref,
                  intra_ref, xif_ref, xib_ref, zeta_ref, o_ref, state_ref, r_ref, fs_ref, *, n_chunks, dec_f):
    @pl.when(pl.program_id(1) == 0)
    def _():
        state_ref[...] = jnp.zeros_like(state_ref)

    tiles = BF16_ROWS // n_chunks
    sub = pl.program_id(1) % tiles

    @pl.when(sub == 0)
    def _():
        _unpack_rows(lambda lanes: f_ref[0, :, :, lanes], fs_ref)

    f_rows = [_strided_rows(fs_ref, sub * n_chunks + c) for c in range(n_chunks)]
    mix = jnp.dot(jnp.concatenate(f_rows, axis=0), w_ref[:FNET_WIDTH, :], preferred_element_type=F32)

    for c in range(n_chunks):
        rows = pl.ds(c * CHUNK, CHUNK)
        for hd in range(RET_HEADS):
            lanes = slice(hd * HEAD_DIM, (hd + 1) * HEAD_DIM)
            head = lambda off: a_ref[0, rows, off + hd * HEAD_DIM:off + (hd + 1) * HEAD_DIM]
            q, k, v = head(Q_OFF), head(K_OFF), head(V_OFF)
            qf = q.astype(F32)
            scores = lax.dot_general(q, k, (((1,), (1,)), ((), ())), preferred_element_type=F32)
            r = jnp.dot((scores * intra_ref[hd]).astype(BF16), v, preferred_element_type=F32)
            r += jnp.dot((qf * xif_ref[hd]).astype(BF16), state_ref[hd].astype(BF16),
                         preferred_element_type=F32)
            r += jnp.dot((qf * xib_ref[hd]).astype(BF16), sb_ref[0, c, hd], preferred_element_type=F32)
            vz = (v.astype(F32) * zeta_ref[hd]).astype(BF16)
            state_ref[hd] = state_ref[hd] * dec_f[hd] + _kv_outer(k, vz)
            gate = head(G_OFF).astype(F32)
            r_ref[rows, lanes] = (_rms(r) * (gate * jax.nn.sigmoid(gate))).astype(BF16)

    mix += jnp.dot(r_ref[...], w_ref[FNET_WIDTH:, :], preferred_element_type=F32)
    o_ref[0] = (mod_ref[0, 2:3, :] * mix).astype(BF16)


def _mixer(mod, f, act, sb, w_out, tables):
    intra, xi_f, xi_b, zeta_f, dec_f = tables
    b, s, _ = act.shape
    tm = min(TM_MIXER, s // 2)
    nc = tm // CHUNK
    assert f.shape[1] == CHUNK and BF16_ROWS % nc == 0 and (s // tm) % (BF16_ROWS // nc) == 0
    tiles = BF16_ROWS // nc
    tok = lambda width: pl.BlockSpec((1, tm, width), lambda bi, ti: (bi, ti, 0))
    const = lambda shape: pl.BlockSpec(shape, lambda bi, ti: (0,) * len(shape))
    tab = const((RET_HEADS, CHUNK, HEAD_DIM))
    states = pl.BlockSpec((1, nc, RET_HEADS, HEAD_DIM, HEAD_DIM), lambda bi, ti: (bi, ti, 0, 0, 0))
    return pl.pallas_call(
        functools.partial(_mixer_kernel, n_chunks=nc, dec_f=dec_f),
        grid=(b, s // tm),
        in_specs=[pl.BlockSpec((1, N_MOD, D_MODEL), lambda bi, ti: (bi, 0, 0)),
                  pl.BlockSpec((1, CHUNK, BF16_ROWS, FNET_WIDTH), lambda bi, ti: (bi, 0, ti // tiles, 0)),
                  tok(U_OFF),
                  states,
                  const((D_MODEL, D_MODEL)), tab, tab, tab, tab],
        out_specs=tok(D_MODEL),
        out_shape=jax.ShapeDtypeStruct((b, s, D_MODEL), BF16),
        scratch_shapes=[pltpu.VMEM((RET_HEADS, HEAD_DIM, HEAD_DIM), F32),
                        pltpu.VMEM((tm, RET_WIDTH), BF16),
                        pltpu.VMEM((1, FNET_GROUPS, CHUNK * BF16_ROWS, GROUP_DIM), F32)],
        compiler_params=_params(("parallel", "arbitrary")),
        name="mixer",
    )(mod, f, act, sb, w_out, intra, xi_f, xi_b, zeta_f)


def _mlp_kernel(x_ref, d_ref, mod_ref, gain_ref, w1_ref, w2_ref, gfin_ref, o_ref):
    x = x_ref[0] + d_ref[0].astype(F32)
    a = gain_ref[...] * (1.0 + mod_ref[0, 4:5, :])
    h = (_rms(x) * a + mod_ref[0, 3:4, :]).astype(BF16)
    out = None
    for part in range(MLP_FF_PARTS):
        cols = slice(part * (D_FF // MLP_FF_PARTS), (part + 1) * (D_FF // MLP_FF_PARTS))
        mid = jnp.dot(h, w1_ref[:, cols], preferred_element_type=F32)
        mid = jnp.square(jnp.maximum(mid, 0.0)).astype(BF16)
        slab = jnp.dot(mid, w2_ref[cols, :], preferred_element_type=F32)
        out = slab if out is None else out + slab
    y = x + mod_ref[0, 5:6, :] * out
    o_ref[0] = _rms(y) * gfin_ref[...]


def _mlp(x, delta, mod, gain, w1, w2, gain_final):
    b, s, _ = x.shape
    tm = min(TM_MLP, s)
    tok = pl.BlockSpec((1, tm, D_MODEL), lambda bi, ti: (bi, ti, 0))
    const = lambda shape: pl.BlockSpec(shape, lambda bi, ti: (0,) * len(shape))
    weight = lambda shape: pl.BlockSpec(shape, lambda bi, ti: (0,) * len(shape), pipeline_mode=pl.Buffered(1))
    return pl.pallas_call(
        _mlp_kernel,
        grid=(b, s // tm),
        in_specs=[tok, tok, pl.BlockSpec((1, N_MOD, D_MODEL), lambda bi, ti: (bi, 0, 0)),
                  const((1, D_MODEL)), weight((D_MODEL, D_FF)), weight((D_FF, D_MODEL)), const((1, D_MODEL))],
        out_specs=tok,
        out_shape=jax.ShapeDtypeStruct((b, s, D_MODEL), F32),
        compiler_params=_params(("parallel", "parallel")),
        name="mlp",
    )(x, delta, mod, gain, w1, w2, gain_final)


def _trunk(x, mod, weights, tables):
    gain_mix, w_in, m_chan, w_out, gain_mlp, w1, w2, gain_final = weights
    intra, xi_f, xi_b, zeta_f, zeta_b, dec_f, dec_b = tables
    casts = (w1, w2) if w1.dtype != BF16 else ()
    (act, sb), cast = _inproj(x, mod, gain_mix, w_in, zeta_b, dec_b, casts)
    if casts:
        w1, w2 = cast
    f = _fnet(act, m_chan)
    delta = _mixer(mod, f, act, sb, w_out, (intra, xi_f, xi_b, zeta_f, dec_f))
    return _mlp(x, delta, mod, gain_mlp, w1, w2, gain_final), w1, w2


def kernel(x_prompt, x_sample, c_prompt, c_sample, ada_w, ada_b, norm_mix, w_in, w_fnet, w_out,
           norm_mlp, w_mlp_in, w_mlp_out, norm_final):
    assert ada_w.shape[0] == 1, "one encoder layer"
    bp, bs = x_prompt.shape[0], x_sample.shape[0]
    c_all = jnp.concatenate([c_prompt, c_sample], axis=0)
    mod, (w_in_b, w_out_b) = _modulation_and_casts(c_all, ada_w[0], ada_b[0], (w_in[0], w_out[0]))
    mod = mod.reshape(bp + bs, N_MOD, D_MODEL)
    weights = (norm_mix[0].reshape(1, -1), w_in_b, _fnet_channel_weights(w_fnet[0]),
               w_out_b, norm_mlp[0].reshape(1, -1), w_mlp_in[0], w_mlp_out[0], norm_final.reshape(1, -1))
    tables = _retention_tables()
    y_prompt, w1_b, w2_b = _trunk(x_prompt, mod[:bp], weights, tables)
    y_sample, _, _ = _trunk(x_sample, mod[bp:], weights[:5] + (w1_b, w2_b) + weights[7:], tables)
    return (y_prompt, y_sample)
```

```python
import functools

import numpy as np
import jax
import jax.numpy as jnp
from jax import lax
from jax.experimental import pallas as pl
from jax.experimental.pallas import tpu as pltpu

F32 = jnp.float32
BF16 = jnp.bfloat16

D_MODEL = 1024
FNET_WIDTH = 512
FNET_GROUPS = 4
GROUP_DIM = 128
RET_WIDTH = 512
RET_HEADS = 4
HEAD_DIM = 128
D_FF = 4096
CHUNK = 128
ROPE_BASE = 10000.0
EPS = 1e-6
N_MOD = 6
IN_WIDTH = FNET_WIDTH + 4 * RET_WIDTH
DECAY_OFFSET_FWD = 0.0
DECAY_OFFSET_BWD = 0.5

DFT_N1 = 128
MXU_DIM = 256
BF16_ROWS = 16
F32_ROWS = 8
WORD_ROWS = BF16_ROWS // 2
HALF_GROUPS = MXU_DIM // GROUP_DIM
Y_K1 = 72
FNET_UNROLL = 32
STAGE_A_UNROLL = 2
VMEM_LIMIT = 56 * 1024 * 1024

TM_INPROJ = 1024
TM_MIXER = 2048
TM_MLP = 1024
MLP_FF_PARTS = 4
MOD_STEPS = 8


def _params(semantics):
    return pltpu.CompilerParams(dimension_semantics=semantics, vmem_limit_bytes=VMEM_LIMIT)


def _retention_tables():
    scale = HEAD_DIM ** -0.5
    h = np.arange(RET_HEADS, dtype=np.float64)
    lg_f = np.log1p(-np.exp2(-5.0 - DECAY_OFFSET_FWD - h))[:, None, None]
    lg_b = np.log1p(-np.exp2(-5.0 - DECAY_OFFSET_BWD - h))[:, None, None]
    i = np.arange(CHUNK, dtype=np.float64)[None, :, None]
    j = np.arange(CHUNK, dtype=np.float64)[None, None, :]
    ones = np.ones((1, 1, HEAD_DIM))
    intra = scale * np.where(i >= j, np.exp(lg_f * np.maximum(i - j, 0.0)), np.exp(lg_b * np.maximum(j - i, 0.0)))
    xi_f = np.exp(lg_f * (i + 1.0)) * ones
    xi_b = np.exp(lg_b * (CHUNK - i)) * ones
    zeta_f = scale * np.exp(lg_f * (CHUNK - 1.0 - i)) * ones
    zeta_b = scale * np.exp(lg_b * i) * ones
    dec_f = tuple(float(v) for v in np.exp(lg_f[:, 0, 0] * CHUNK))
    dec_b = tuple(float(v) for v in np.exp(lg_b[:, 0, 0] * CHUNK))
    f = lambda a: jnp.asarray(a.astype(np.float32))
    return f(intra), f(xi_f), f(xi_b), f(zeta_f), f(zeta_b), dec_f, dec_b


def _f32(a):
    return jnp.asarray(np.asarray(a).astype(np.float32))


def _rotary_tables(seq, tm):
    half = HEAD_DIM // 2
    inv = ROPE_BASE ** (-np.arange(half, dtype=np.float64) / half)
    inv = np.concatenate([inv, inv])
    sign = np.concatenate([-np.ones(half), np.ones(half)])
    base = np.arange(0, seq, tm, dtype=np.float64)[:, None] * inv[None, :]
    local = np.arange(tm, dtype=np.float64)[:, None] * inv[None, :]
    base_rows = np.stack([np.cos(base), np.sin(base), sign * np.cos(base), sign * np.sin(base)], axis=1)
    return _f32(base_rows), _f32(np.stack([np.cos(local), np.sin(local)], axis=0))


def _dft_stage_a(n1):
    k = np.arange(n1, dtype=np.int64)
    ang = 2.0 * np.pi * ((k[:Y_K1, None] * k[None, :]) % n1) / n1
    base = np.stack([np.cos(ang), -np.sin(ang)], axis=1).reshape(2 * Y_K1, n1) * (n1 ** -0.5)
    even, odd = np.zeros((2 * Y_K1, 2 * n1)), np.zeros((2 * Y_K1, 2 * n1))
    even[:, 0::2] = base
    odd[:, 1::2] = base
    return _f32(even).astype(BF16), _f32(odd).astype(BF16)


def _dft_stage_b(n1, n2):
    reps = MXU_DIM // (2 * n2)
    seq = n1 * n2
    idx = np.arange(MXU_DIM, dtype=np.int64)
    row_c, row_r, row_k = idx // (reps * n2), (idx // n2) % reps, idx % n2
    col_r, col_s, col_c = idx // (2 * n2), (idx // 2) % n2, idx % 2
    alpha = 2.0 * np.pi * ((row_k[:, None] * col_s[None, :]) % n2) / n2
    ca, sa = np.cos(alpha), np.sin(alpha)
    c_out, c_in = row_c[:, None], col_c[None, :]
    p_mat = np.where(c_out == c_in, ca, np.where(c_out < c_in, sa, -sa))
    q_mat = np.where(c_out == c_in, -sa, np.where(c_out < c_in, ca, -ca))
    keep = (row_r[:, None] == col_r[None, :]) * ((n2 * GROUP_DIM) ** -0.5)
    blk = np.arange(n1 // reps, dtype=np.int64)[:, None]
    k1 = blk * reps + col_r[None, :]
    beta = 2.0 * np.pi * ((col_s[None, :] * k1) % seq) / seq
    conj = np.where((k1 >= Y_K1) & (col_c[None, :] == 1), -1.0, 1.0)
    return (_f32(p_mat * keep), _f32(q_mat * keep),
            _f32((np.cos(beta) * conj)[:, None, :]), _f32((np.sin(beta) * conj)[:, None, :]))


def _channel_dft():
    c = np.arange(GROUP_DIM, dtype=np.int64)
    ang = 2.0 * np.pi * ((c[:, None] * c[None, :]) % GROUP_DIM) / GROUP_DIM
    return jnp.asarray(np.concatenate([np.cos(ang), np.sin(ang)], axis=0).astype(np.float32))


def _mod_kernel(c_ref, w_ref, b_ref, *refs):
    c = c_ref[...]
    s = c * jax.nn.sigmoid(c)
    w = w_ref[...]
    s_hi, w_hi = s.astype(BF16), w.astype(BF16)
    s_lo, w_lo = (s - s_hi.astype(F32)).astype(BF16), (w - w_hi.astype(F32)).astype(BF16)
    dot = functools.partial(jnp.dot, preferred_element_type=F32)
    n_cast = (len(refs) - 1) // 2
    o_ref = refs[n_cast]
    o_ref[...] = dot(s_hi, w_hi) + (dot(s_hi, w_lo) + dot(s_lo, w_hi)) + b_ref[...]
    for src, dst in zip(refs[:n_cast], refs[n_cast + 1:]):
        dst[...] = src[...].astype(BF16)


def _modulation_and_casts(c, ada_w, ada_b, weights):
    rows = c.shape[0]
    steps = MOD_STEPS
    tn = N_MOD * D_MODEL // steps
    row_block = lambda w: pl.BlockSpec((w.shape[0] // steps, w.shape[1]), lambda n: (n, 0))
    out = pl.pallas_call(
        _mod_kernel,
        grid=(steps,),
        in_specs=[pl.BlockSpec((rows, D_MODEL), lambda n: (0, 0)),
                  pl.BlockSpec((D_MODEL, tn), lambda n: (0, n)),
                  pl.BlockSpec((1, tn), lambda n: (0, n))] + [row_block(w) for w in weights],
        out_specs=[pl.BlockSpec((rows, tn), lambda n: (0, n))] + [row_block(w) for w in weights],
        out_shape=[jax.ShapeDtypeStruct((rows, N_MOD * D_MODEL), F32)]
                  + [jax.ShapeDtypeStruct(w.shape, BF16) for w in weights],
        compiler_params=_params(("arbitrary",)),
        name="adaln_mod",
    )(c, ada_w, ada_b.reshape(1, -1), *weights)
    return out[0], out[1:]


def _fnet_weight_kernel(cs_ref, w_ref, o_ref):
    o_ref[0] = jnp.dot(cs_ref[...], w_ref[0], preferred_element_type=F32,
                       precision=lax.Precision.HIGHEST).astype(BF16)


def _fnet_channel_weights(w_fnet):
    return pl.pallas_call(
        _fnet_weight_kernel,
        grid=(FNET_GROUPS,),
        in_specs=[pl.BlockSpec((2 * GROUP_DIM, GROUP_DIM), lambda g: (0, 0)),
                  pl.BlockSpec((1, GROUP_DIM, GROUP_DIM), lambda g: (g, 0, 0))],
        out_specs=pl.BlockSpec((1, 2 * GROUP_DIM, GROUP_DIM), lambda g: (g, 0, 0)),
        out_shape=jax.ShapeDtypeStruct((FNET_GROUPS, 2 * GROUP_DIM, GROUP_DIM), BF16),
        compiler_params=_params(("arbitrary",)),
        name="fnet_channel_weights",
    )(_channel_dft(), w_fnet)


def _rms(x):
    return x * lax.rsqrt(jnp.mean(x * x, axis=-1, keepdims=True) + EPS)


def _kv_outer(k, vz):
    return lax.dot_general(k, vz, (((0,), (0,)), ((), ())), preferred_element_type=F32)


def _inproj_kernel(x_ref, mod_ref, gain_ref, w_ref, rbase_ref, rlocal_ref, zeta_ref, *refs, n_chunks, dec_b, n_cast):
    u_ref, q_ref, k_ref, v_ref, g_ref, sb_ref = refs[n_cast:n_cast + 6]
    state_ref = refs[-1]
    for src, dst in zip(refs[:n_cast], refs[n_cast + 6:-1]):
        dst[...] = src[...].astype(BF16)

    @pl.when(pl.program_id(1) == 0)
    def _():
        state_ref[...] = jnp.zeros_like(state_ref)

    x = x_ref[0]
    a = gain_ref[...] * (1.0 + mod_ref[0, 1:2, :])
    h = (_rms(x) * a + mod_ref[0, 0:1, :]).astype(BF16)
    proj = jnp.dot(h, w_ref[...], preferred_element_type=F32)
    u_ref[0] = proj[:, :FNET_WIDTH].astype(BF16)
    cos_l, sin_l = rlocal_ref[0], rlocal_ref[1]
    cos = rbase_ref[0, 0:1, :] * cos_l - rbase_ref[0, 1:2, :] * sin_l
    sin = rbase_ref[0, 3:4, :] * cos_l + rbase_ref[0, 2:3, :] * sin_l
    for hd in range(RET_HEADS):
        lo = hd * HEAD_DIM
        for off, ref in ((FNET_WIDTH, q_ref), (FNET_WIDTH + RET_WIDTH, k_ref)):
            t = proj[:, off + lo:off + lo + HEAD_DIM]
            ref[0, :, lo:lo + HEAD_DIM] = (t * cos + pltpu.roll(t, HEAD_DIM // 2, 1) * sin).astype(BF16)
    v_ref[0] = proj[:, FNET_WIDTH + 2 * RET_WIDTH:FNET_WIDTH + 3 * RET_WIDTH].astype(BF16)
    g_ref[0] = proj[:, FNET_WIDTH + 3 * RET_WIDTH:].astype(BF16)

    for c in reversed(range(n_chunks)):
        rows = pl.ds(c * CHUNK, CHUNK)
        for hd in range(RET_HEADS):
            lanes = slice(hd * HEAD_DIM, (hd + 1) * HEAD_DIM)
            sb_ref[0, c, hd] = state_ref[hd].astype(BF16)
            vz = (v_ref[0, rows, lanes].astype(F32) * zeta_ref[hd]).astype(BF16)
            state_ref[hd] = state_ref[hd] * dec_b[hd] + _kv_outer(k_ref[0, rows, lanes], vz)


def _inproj(x, mod, gain, w_in, zeta_b, dec_b, casts=()):
    b, s, _ = x.shape
    tm = min(TM_INPROJ, s)
    nt = s // tm
    nc = tm // CHUNK
    rot_base, rot_local = _rotary_tables(s, tm)
    rev = lambda bi, ti: (bi, nt - 1 - ti, 0)
    tok = lambda width: pl.BlockSpec((1, tm, width), rev)
    const = lambda shape: pl.BlockSpec(shape, lambda bi, ti: (0,) * len(shape))
    act = jax.ShapeDtypeStruct((b, s, RET_WIDTH), BF16)
    states = pl.BlockSpec((1, nc, RET_HEADS, HEAD_DIM, HEAD_DIM), lambda bi, ti: (bi, nt - 1 - ti, 0, 0, 0))
    state_shape = jax.ShapeDtypeStruct((b, s // CHUNK, RET_HEADS, HEAD_DIM, HEAD_DIM), BF16)
    cast_specs = [pl.BlockSpec((w.shape[0] // (b * nt), w.shape[1]), lambda bi, ti: (bi * nt + ti, 0)) for w in casts]
    out = pl.pallas_call(
        functools.partial(_inproj_kernel, n_chunks=nc, dec_b=dec_b, n_cast=len(casts)),
        grid=(b, nt),
        in_specs=[tok(D_MODEL),
                  pl.BlockSpec((1, N_MOD, D_MODEL), lambda bi, ti: (bi, 0, 0)),
                  const((1, D_MODEL)),
                  const((D_MODEL, IN_WIDTH)),
                  pl.BlockSpec((1, 4, HEAD_DIM), lambda bi, ti: (nt - 1 - ti, 0, 0)),
                  const((2, tm, HEAD_DIM)),
                  const((RET_HEADS, CHUNK, HEAD_DIM))] + cast_specs,
        out_specs=[tok(FNET_WIDTH), tok(RET_WIDTH), tok(RET_WIDTH), tok(RET_WIDTH), tok(RET_WIDTH),
                   states] + cast_specs,
        out_shape=[jax.ShapeDtypeStruct((b, s, FNET_WIDTH), BF16), act, act, act, act,
                   state_shape] + [jax.ShapeDtypeStruct(w.shape, BF16) for w in casts],
        scratch_shapes=[pltpu.VMEM((RET_HEADS, HEAD_DIM, HEAD_DIM), F32)],
        compiler_params=_params(("parallel", "arbitrary")),
        name="inproj",
    )(x, mod, gain, w_in, rot_base, rot_local, zeta_b, *casts)
    return out[:6], out[6:]


def _group_lanes():
    return [(g, slice(g * GROUP_DIM, (g + 1) * GROUP_DIM)) for g in range(FNET_GROUPS)]


def _unpack_rows(load, dst_ref, slab=0):
    units = dst_ref.shape[2] // BF16_ROWS
    half = units * F32_ROWS
    for g, lanes in _group_lanes():
        rows = load(lanes).reshape(units, BF16_ROWS, GROUP_DIM).astype(F32)
        dst_ref[slab, g, 0:half, :] = rows[:, :F32_ROWS, :].reshape(half, GROUP_DIM)
        dst_ref[slab, g, half:, :] = rows[:, F32_ROWS:, :].reshape(half, GROUP_DIM)


def _strided_rows(src_ref, row, slab=0):
    units = src_ref.shape[2] // BF16_ROWS
    start = (row // F32_ROWS) * (units * F32_ROWS) + row % F32_ROWS
    rows = pl.ds(start, units, stride=F32_ROWS)
    return jnp.concatenate([src_ref[slab, g, rows, :] for g in range(FNET_GROUPS)], axis=1).astype(BF16)


def _word_rows(src_ref, lead, word, units):
    rows = pl.ds(word, units, stride=WORD_ROWS)
    return jnp.concatenate([pltpu.bitcast(src_ref[lead, g, rows, :], BF16) for g in range(HALF_GROUPS)], axis=1)


def _fnet_kernel(x_ref, even_ref, odd_ref, p_ref, q_ref, cb_ref, sb_ref, m_ref, o_ref, xw_ref, yw_ref, *, n2, reps):
    n1 = DFT_N1
    half = MXU_DIM // 2
    groups = [slice(g * GROUP_DIM, (g + 1) * GROUP_DIM) for g in range(HALF_GROUPS)]

    def aligned(start, step):
        return start * step if isinstance(start, int) else pl.multiple_of(start * step, step)

    def stage_a(sl, slot):
        s2_base = aligned(sl, BF16_ROWS)
        for g, lanes in enumerate(groups):
            block = x_ref[0, :, pl.ds(s2_base, BF16_ROWS), lanes]
            xw_ref[slot, g] = pltpu.bitcast(block.reshape(n1 * BF16_ROWS, GROUP_DIM), jnp.uint32)
        for i in range(WORD_ROWS):
            pair = _word_rows(xw_ref, slot, i, n1)
            for parity, mat_ref in enumerate((even_ref, odd_ref)):
                y = jnp.dot(mat_ref[...], pair, preferred_element_type=F32).astype(BF16)
                words = pltpu.bitcast(y, jnp.uint32)
                row0 = aligned(s2_base + 2 * i + parity, WORD_ROWS)
                for ks in range(Y_K1 // WORD_ROWS):
                    for g, lanes in enumerate(groups):
                        yw_ref[ks, g, pl.ds(row0, WORD_ROWS), :] = words[ks * WORD_ROWS:(ks + 1) * WORD_ROWS, lanes]

    def stage_b(blk, slot):
        t = (p_ref[...] * cb_ref[blk] + q_ref[...] * sb_ref[blk]).astype(BF16)
        pieces = []
        for r in range(reps):
            k1 = blk * reps + r
            src = (n1 - k1 if k1 >= Y_K1 else k1) if isinstance(k1, int) else jnp.where(k1 >= Y_K1, n1 - k1, k1)
            pieces.append(_word_rows(yw_ref, src // WORD_ROWS, src % WORD_ROWS, n2))
        yb = pieces[0] if reps == 1 else jnp.concatenate(pieces, axis=0)
        z = jnp.dot(t, yb, preferred_element_type=F32)
        for g, lanes in enumerate(groups):
            zz = jnp.concatenate([z[:half, lanes], z[half:, lanes]], axis=1).astype(BF16)
            res = jnp.dot(zz, m_ref[g], preferred_element_type=F32)
            for r in range(reps):
                o_ref[0, blk * reps + r, :, lanes] = res[r * n2:(r + 1) * n2].astype(BF16)

    _for_each(n2 // BF16_ROWS, stage_a, STAGE_A_UNROLL)
    _for_each(n1 // reps, stage_b, FNET_UNROLL)


def _for_each(count, body, group):
    if count <= group:
        for i in range(count):
            body(i, i)
        return

    def step(j, carry):
        for u in range(group):
            body(j * group + u, u)
        return carry

    lax.fori_loop(0, count // group, step, 0)


def _fnet(u, m_chan):
    b, s, _ = u.shape
    n1 = DFT_N1
    n2 = s // n1
    assert s == n1 * n2 and MXU_DIM % (2 * n2) == 0 and n2 % BF16_ROWS == 0, s
    reps = MXU_DIM // (2 * n2)
    p_mat, q_mat, cos_b, sin_b = _dft_stage_b(n1, n2)
    const = lambda shape: pl.BlockSpec(shape, lambda bi, hi: (0,) * len(shape))
    tokens = pl.BlockSpec((1, n1, n2, MXU_DIM), lambda bi, hi: (bi, 0, 0, hi))
    return pl.pallas_call(
        functools.partial(_fnet_kernel, n2=n2, reps=reps),
        grid=(b, FNET_WIDTH // MXU_DIM),
        in_specs=[tokens, const((2 * Y_K1, 2 * n1)), const((2 * Y_K1, 2 * n1)),
                  const((MXU_DIM, MXU_DIM)), const((MXU_DIM, MXU_DIM)),
                  const((n1 // reps, 1, MXU_DIM)), const((n1 // reps, 1, MXU_DIM)),
                  pl.BlockSpec((HALF_GROUPS, 2 * GROUP_DIM, GROUP_DIM), lambda bi, hi: (hi, 0, 0))],
        out_specs=tokens,
        out_shape=jax.ShapeDtypeStruct((b, n1, n2, FNET_WIDTH), BF16),
        scratch_shapes=[pltpu.VMEM((STAGE_A_UNROLL, HALF_GROUPS, n1 * WORD_ROWS, GROUP_DIM), jnp.uint32),
                        pltpu.VMEM((Y_K1 // WORD_ROWS, HALF_GROUPS, n2 * WORD_ROWS, GROUP_DIM), jnp.uint32)],
        compiler_params=_params(("parallel", "parallel")),
        name="fnet",
    )(u.reshape(b, n1, n2, FNET_WIDTH), *_dft_stage_a(n1), p_mat, q_mat, cos_b, sin_b, m_chan)


def _mixer_kernel(mod_ref, f_ref, q_ref, k_ref, v_ref, g_ref, sb_ref, w_ref,
                  intra_ref, xif_ref, xib_ref, zeta_ref, o_ref, state_ref, r_ref, fs_ref, *, n_chunks, dec_f):
    @pl.when(pl.program_id(1) == 0)
    def _():
        state_ref[...] = jnp.zeros_like(state_ref)

    tiles = BF16_ROWS // n_chunks
    sub = pl.program_id(1) % tiles

    @pl.when(sub == 0)
    def _():
        _unpack_rows(lambda lanes: f_ref[0, :, :, lanes], fs_ref)

    f_rows = [_strided_rows(fs_ref, sub * n_chunks + c) for c in range(n_chunks)]
    mix = jnp.dot(jnp.concatenate(f_rows, axis=0), w_ref[:FNET_WIDTH, :], preferred_element_type=F32)

    for c in range(n_chunks):
        rows = pl.ds(c * CHUNK, CHUNK)
        for hd in range(RET_HEADS):
            lanes = slice(hd * HEAD_DIM, (hd + 1) * HEAD_DIM)
            q = q_ref[0, rows, lanes]
            k = k_ref[0, rows, lanes]
            v = v_ref[0, rows, lanes]
            qf = q.astype(F32)
            scores = lax.dot_general(q, k, (((1,), (1,)), ((), ())), preferred_element_type=F32)
            r = jnp.dot((scores * intra_ref[hd]).astype(BF16), v, preferred_element_type=F32)
            r += jnp.dot((qf * xif_ref[hd]).astype(BF16), state_ref[hd].astype(BF16),
                         preferred_element_type=F32)
            r += jnp.dot((qf * xib_ref[hd]).astype(BF16), sb_ref[0, c, hd], preferred_element_type=F32)
            vz = (v.astype(F32) * zeta_ref[hd]).astype(BF16)
            state_ref[hd] = state_ref[hd] * dec_f[hd] + _kv_outer(k, vz)
            gate = g_ref[0, rows, lanes].astype(F32)
            r_ref[rows, lanes] = (_rms(r) * (gate * jax.nn.sigmoid(gate))).astype(BF16)

    mix += jnp.dot(r_ref[...], w_ref[FNET_WIDTH:, :], preferred_element_type=F32)
    o_ref[0] = (mod_ref[0, 2:3, :] * mix).astype(BF16)


def _mixer(mod, f, q, k, v, g, sb, w_out, tables):
    intra, xi_f, xi_b, zeta_f, dec_f = tables
    b, s, _ = q.shape
    tm = min(TM_MIXER, s // 2)
    nc = tm // CHUNK
    assert f.shape[1] == CHUNK and BF16_ROWS % nc == 0 and (s // tm) % (BF16_ROWS // nc) == 0
    tiles = BF16_ROWS // nc
    tok = lambda width: pl.BlockSpec((1, tm, width), lambda bi, ti: (bi, ti, 0))
    const = lambda shape: pl.BlockSpec(shape, lambda bi, ti: (0,) * len(shape))
    tab = const((RET_HEADS, CHUNK, HEAD_DIM))
    states = pl.BlockSpec((1, nc, RET_HEADS, HEAD_DIM, HEAD_DIM), lambda bi, ti: (bi, ti, 0, 0, 0))
    return pl.pallas_call(
        functools.partial(_mixer_kernel, n_chunks=nc, dec_f=dec_f),
        grid=(b, s // tm),
        in_specs=[pl.BlockSpec((1, N_MOD, D_MODEL), lambda bi, ti: (bi, 0, 0)),
                  pl.BlockSpec((1, CHUNK, BF16_ROWS, FNET_WIDTH), lambda bi, ti: (bi, 0, ti // tiles, 0)),
                  tok(RET_WIDTH), tok(RET_WIDTH), tok(RET_WIDTH), tok(RET_WIDTH),
                  states,
                  const((D_MODEL, D_MODEL)), tab, tab, tab, tab],
        out_specs=tok(D_MODEL),
        out_shape=jax.ShapeDtypeStruct((b, s, D_MODEL), BF16),
        scratch_shapes=[pltpu.VMEM((RET_HEADS, HEAD_DIM, HEAD_DIM), F32),
                        pltpu.VMEM((tm, RET_WIDTH), BF16),
                        pltpu.VMEM((1, FNET_GROUPS, CHUNK * BF16_ROWS, GROUP_DIM), F32)],
        compiler_params=_params(("parallel", "arbitrary")),
        name="mixer",
    )(mod, f, q, k, v, g, sb, w_out, intra, xi_f, xi_b, zeta_f)


def _mlp_kernel(x_ref, d_ref, mod_ref, gain_ref, w1_hbm, w2_hbm, gfin_ref, o_ref, w1_ref, w2_ref, sem):
    ff = D_FF // MLP_FF_PARTS

    def copies(part):
        return (pltpu.make_async_copy(w1_hbm.at[:, pl.ds(part * ff, ff)], w1_ref.at[part], sem.at[0, part]),
                pltpu.make_async_copy(w2_hbm.at[pl.ds(part * ff, ff), :], w2_ref.at[part], sem.at[1, part]))

    def body(fetching):
        x = x_ref[0] + d_ref[0].astype(F32)
        a = gain_ref[...] * (1.0 + mod_ref[0, 4:5, :])
        h = (_rms(x) * a + mod_ref[0, 3:4, :]).astype(BF16)
        out = None
        for part in range(MLP_FF_PARTS):
            if fetching:
                for cp in copies(part):
                    cp.wait()
            mid = jnp.dot(h, w1_ref[part], preferred_element_type=F32)
            mid = jnp.square(jnp.maximum(mid, 0.0)).astype(BF16)
            slab = jnp.dot(mid, w2_ref[part], preferred_element_type=F32)
            out = slab if out is None else out + slab
        y = x + mod_ref[0, 5:6, :] * out
        o_ref[0] = _rms(y) * gfin_ref[...]

    first = (pl.program_id(0) == 0) & (pl.program_id(1) == 0)

    @pl.when(first)
    def _():
        for part in range(MLP_FF_PARTS):
            for cp in copies(part):
                cp.start()
        body(True)

    @pl.when(jnp.logical_not(first))
    def _():
        body(False)


def _mlp(x, delta, mod, gain, w1, w2, gain_final):
    b, s, _ = x.shape
    tm = min(TM_MLP, s)
    ff = D_FF // MLP_FF_PARTS
    tok = pl.BlockSpec((1, tm, D_MODEL), lambda bi, ti: (bi, ti, 0))
    const = lambda shape: pl.BlockSpec(shape, lambda bi, ti: (0,) * len(shape))
    weight = pl.BlockSpec(memory_space=pl.ANY)
    return pl.pallas_call(
        _mlp_kernel,
        grid=(b, s // tm),
        in_specs=[tok, tok, pl.BlockSpec((1, N_MOD, D_MODEL), lambda bi, ti: (bi, 0, 0)),
                  const((1, D_MODEL)), weight, weight, const((1, D_MODEL))],
        out_specs=tok,
        out_shape=jax.ShapeDtypeStruct((b, s, D_MODEL), F32),
        scratch_shapes=[pltpu.VMEM((MLP_FF_PARTS, D_MODEL, ff), BF16),
                        pltpu.VMEM((MLP_FF_PARTS, ff, D_MODEL), BF16),
                        pltpu.SemaphoreType.DMA((2, MLP_FF_PARTS))],
        compiler_params=_params(("arbitrary", "arbitrary")),
        name="mlp",
    )(x, delta, mod, gain, w1, w2, gain_final)


def _trunk(x, mod, weights, tables):
    gain_mix, w_in, m_chan, w_out, gain_mlp, w1, w2, gain_final = weights
    intra, xi_f, xi_b, zeta_f, zeta_b, dec_f, dec_b = tables
    casts = (w1, w2) if w1.dtype != BF16 else ()
    (u, q, k, v, g, sb), cast = _inproj(x, mod, gain_mix, w_in, zeta_b, dec_b, casts)
    if casts:
        w1, w2 = cast
    f = _fnet(u, m_chan)
    delta = _mixer(mod, f, q, k, v, g, sb, w_out, (intra, xi_f, xi_b, zeta_f, dec_f))
    return _mlp(x, delta, mod, gain_mlp, w1, w2, gain_final), w1, w2


def kernel(x_prompt, x_sample, c_prompt, c_sample, ada_w, ada_b, norm_mix, w_in, w_fnet, w_out,
           norm_mlp, w_mlp_in, w_mlp_out, norm_final):
    assert ada_w.shape[0] == 1, "one encoder layer"
    bp, bs = x_prompt.shape[0], x_sample.shape[0]
    c_all = jnp.concatenate([c_prompt, c_sample], axis=0)
    mod, (w_in_b, w_out_b) = _modulation_and_casts(c_all, ada_w[0], ada_b[0], (w_in[0], w_out[0]))
    mod = mod.reshape(bp + bs, N_MOD, D_MODEL)
    weights = (norm_mix[0].reshape(1, -1), w_in_b, _fnet_channel_weights(w_fnet[0]),
               w_out_b, norm_mlp[0].reshape(1, -1), w_mlp_in[0], w_mlp_out[0], norm_final.reshape(1, -1))
    tables = _retention_tables()
    y_prompt, w1_b, w2_b = _trunk(x_prompt, mod[:bp], weights, tables)
    y_sample, _, _ = _trunk(x_sample, mod[bp:], weights[:5] + (w1_b, w2_b) + weights[7:], tables)
    return (y_prompt, y_sample)
```

```python
import functools

import numpy as np
import jax
import jax.numpy as jnp
from jax import lax
from jax.experimental import pallas as pl
from jax.experimental.pallas import tpu as pltpu

F32 = jnp.float32
BF16 = jnp.bfloat16

D_MODEL = 1024
FNET_WIDTH = 512
FNET_GROUPS = 4
GROUP_DIM = 128
RET_WIDTH = 512
RET_HEADS = 4
HEAD_DIM = 128
D_FF = 4096
CHUNK = 128
ROPE_BASE = 10000.0
EPS = 1e-6
N_MOD = 6
IN_WIDTH = FNET_WIDTH + 4 * RET_WIDTH
DECAY_OFFSET_FWD = 0.0
DECAY_OFFSET_BWD = 0.5

DFT_N1 = 128
MXU_DIM = 256
BF16_ROWS = 16
F32_ROWS = 8
WORD_ROWS = BF16_ROWS // 2
HALF_GROUPS = MXU_DIM // GROUP_DIM
Y_K1 = 72
FNET_UNROLL = 32
STAGE_A_UNROLL = 2
VMEM_LIMIT = 56 * 1024 * 1024

TM_INPROJ = 1024
TM_MIXER = 2048
TM_MLP = 1024
MLP_FF_PARTS = 4
MOD_STEPS = 8


def _params(semantics):
    return pltpu.CompilerParams(dimension_semantics=semantics, vmem_limit_bytes=VMEM_LIMIT)


def _retention_tables():
    scale = HEAD_DIM ** -0.5
    h = np.arange(RET_HEADS, dtype=np.float64)
    lg_f = np.log1p(-np.exp2(-5.0 - DECAY_OFFSET_FWD - h))[:, None, None]
    lg_b = np.log1p(-np.exp2(-5.0 - DECAY_OFFSET_BWD - h))[:, None, None]
    i = np.arange(CHUNK, dtype=np.float64)[None, :, None]
    j = np.arange(CHUNK, dtype=np.float64)[None, None, :]
    ones = np.ones((1, 1, HEAD_DIM))
    intra = scale * np.where(i >= j, np.exp(lg_f * np.maximum(i - j, 0.0)), np.exp(lg_b * np.maximum(j - i, 0.0)))
    xi_f = np.exp(lg_f * (i + 1.0)) * ones
    xi_b = np.exp(lg_b * (CHUNK - i)) * ones
    zeta_f = scale * np.exp(lg_f * (CHUNK - 1.0 - i)) * ones
    zeta_b = scale * np.exp(lg_b * i) * ones
    dec_f = tuple(float(v) for v in np.exp(lg_f[:, 0, 0] * CHUNK))
    dec_b = tuple(float(v) for v in np.exp(lg_b[:, 0, 0] * CHUNK))
    f = lambda a: jnp.asarray(a.astype(np.float32))
    return f(intra), f(xi_f), f(xi_b), f(zeta_f), f(zeta_b), dec_f, dec_b


def _f32(a):
    return jnp.asarray(np.asarray(a).astype(np.float32))


def _rotary_tables(seq, tm):
    half = HEAD_DIM // 2
    inv = ROPE_BASE ** (-np.arange(half, dtype=np.float64) / half)
    inv = np.concatenate([inv, inv])
    sign = np.concatenate([-np.ones(half), np.ones(half)])
    base = np.arange(0, seq, tm, dtype=np.float64)[:, None] * inv[None, :]
    local = np.arange(tm, dtype=np.float64)[:, None] * inv[None, :]
    base_rows = np.stack([np.cos(base), np.sin(base), sign * np.cos(base), sign * np.sin(base)], axis=1)
    return _f32(base_rows), _f32(np.stack([np.cos(local), np.sin(local)], axis=0))


def _dft_stage_a(n1):
    k = np.arange(n1, dtype=np.int64)
    ang = 2.0 * np.pi * ((k[:Y_K1, None] * k[None, :]) % n1) / n1
    base = np.stack([np.cos(ang), -np.sin(ang)], axis=1).reshape(2 * Y_K1, n1) * (n1 ** -0.5)
    even, odd = np.zeros((2 * Y_K1, 2 * n1)), np.zeros((2 * Y_K1, 2 * n1))
    even[:, 0::2] = base
    odd[:, 1::2] = base
    return _f32(even).astype(BF16), _f32(odd).astype(BF16)


def _dft_stage_b(n1, n2):
    reps = MXU_DIM // (2 * n2)
    seq = n1 * n2
    idx = np.arange(MXU_DIM, dtype=np.int64)
    row_c, row_r, row_k = idx // (reps * n2), (idx // n2) % reps, idx % n2
    col_r, col_s, col_c = idx // (2 * n2), (idx // 2) % n2, idx % 2
    alpha = 2.0 * np.pi * ((row_k[:, None] * col_s[None, :]) % n2) / n2
    ca, sa = np.cos(alpha), np.sin(alpha)
    c_out, c_in = row_c[:, None], col_c[None, :]
    p_mat = np.where(c_out == c_in, ca, np.where(c_out < c_in, sa, -sa))
    q_mat = np.where(c_out == c_in, -sa, np.where(c_out < c_in, ca, -ca))
    keep = (row_r[:, None] == col_r[None, :]) * ((n2 * GROUP_DIM) ** -0.5)
    blk = np.arange(n1 // reps, dtype=np.int64)[:, None]
    k1 = blk * reps + col_r[None, :]
    beta = 2.0 * np.pi * ((col_s[None, :] * k1) % seq) / seq
    conj = np.where((k1 >= Y_K1) & (col_c[None, :] == 1), -1.0, 1.0)
    return (_f32(p_mat * keep), _f32(q_mat * keep),
            _f32((np.cos(beta) * conj)[:, None, :]), _f32((np.sin(beta) * conj)[:, None, :]))


def _channel_dft():
    c = np.arange(GROUP_DIM, dtype=np.int64)
    ang = 2.0 * np.pi * ((c[:, None] * c[None, :]) % GROUP_DIM) / GROUP_DIM
    return jnp.asarray(np.concatenate([np.cos(ang), np.sin(ang)], axis=0).astype(np.float32))


def _mod_kernel(c_ref, w_ref, b_ref, cs_ref, wf_ref, *refs):
    refs[-1][0] = jnp.dot(cs_ref[...], wf_ref[0], preferred_element_type=F32,
                          precision=lax.Precision.HIGHEST).astype(BF16)
    refs = refs[:-1]
    c = c_ref[...]
    s = c * jax.nn.sigmoid(c)
    w = w_ref[...]
    s_hi, w_hi = s.astype(BF16), w.astype(BF16)
    s_lo, w_lo = (s - s_hi.astype(F32)).astype(BF16), (w - w_hi.astype(F32)).astype(BF16)
    dot = functools.partial(jnp.dot, preferred_element_type=F32)
    n_cast = (len(refs) - 1) // 2
    o_ref = refs[n_cast]
    o_ref[...] = dot(s_hi, w_hi) + (dot(s_hi, w_lo) + dot(s_lo, w_hi)) + b_ref[...]
    for src, dst in zip(refs[:n_cast], refs[n_cast + 1:]):
        dst[...] = src[...].astype(BF16)


def _modulation_and_casts(c, ada_w, ada_b, weights, w_fnet):
    rows = c.shape[0]
    steps = MOD_STEPS
    per = steps // FNET_GROUPS
    tn = N_MOD * D_MODEL // steps
    row_block = lambda w: pl.BlockSpec((w.shape[0] // steps, w.shape[1]), lambda n: (n, 0))
    out = pl.pallas_call(
        _mod_kernel,
        grid=(steps,),
        in_specs=[pl.BlockSpec((rows, D_MODEL), lambda n: (0, 0)),
                  pl.BlockSpec((D_MODEL, tn), lambda n: (0, n)),
                  pl.BlockSpec((1, tn), lambda n: (0, n)),
                  pl.BlockSpec((2 * GROUP_DIM, GROUP_DIM), lambda n: (0, 0)),
                  pl.BlockSpec((1, GROUP_DIM, GROUP_DIM), lambda n: (n // per, 0, 0))]
                 + [row_block(w) for w in weights],
        out_specs=[pl.BlockSpec((rows, tn), lambda n: (0, n))] + [row_block(w) for w in weights]
                  + [pl.BlockSpec((1, 2 * GROUP_DIM, GROUP_DIM), lambda n: (n // per, 0, 0))],
        out_shape=[jax.ShapeDtypeStruct((rows, N_MOD * D_MODEL), F32)]
                  + [jax.ShapeDtypeStruct(w.shape, BF16) for w in weights]
                  + [jax.ShapeDtypeStruct((FNET_GROUPS, 2 * GROUP_DIM, GROUP_DIM), BF16)],
        compiler_params=_params(("arbitrary",)),
        name="adaln_mod",
    )(c, ada_w, ada_b.reshape(1, -1), _channel_dft(), w_fnet, *weights)
    return out[0], out[1:-1], out[-1]


def _rms(x):
    return x * lax.rsqrt(jnp.mean(x * x, axis=-1, keepdims=True) + EPS)


def _kv_outer(k, vz):
    return lax.dot_general(k, vz, (((0,), (0,)), ((), ())), preferred_element_type=F32)


def _inproj_kernel(x_ref, mod_ref, gain_ref, w_ref, rbase_ref, rlocal_ref, zeta_ref, *refs, n_chunks, dec_b, n_cast):
    u_ref, q_ref, k_ref, v_ref, g_ref, sb_ref = refs[n_cast:n_cast + 6]
    state_ref = refs[-1]
    for src, dst in zip(refs[:n_cast], refs[n_cast + 6:-1]):
        dst[...] = src[...].astype(BF16)

    @pl.when(pl.program_id(1) == 0)
    def _():
        state_ref[...] = jnp.zeros_like(state_ref)

    x = x_ref[0]
    a = gain_ref[...] * (1.0 + mod_ref[0, 1:2, :])
    h = (_rms(x) * a + mod_ref[0, 0:1, :]).astype(BF16)
    proj = jnp.dot(h, w_ref[...], preferred_element_type=F32)
    u_ref[0] = proj[:, :FNET_WIDTH].astype(BF16)
    cos_l, sin_l = rlocal_ref[0], rlocal_ref[1]
    cos = rbase_ref[0, 0:1, :] * cos_l - rbase_ref[0, 1:2, :] * sin_l
    sin = rbase_ref[0, 3:4, :] * cos_l + rbase_ref[0, 2:3, :] * sin_l
    for hd in range(RET_HEADS):
        lo = hd * HEAD_DIM
        for off, ref in ((FNET_WIDTH, q_ref), (FNET_WIDTH + RET_WIDTH, k_ref)):
            t = proj[:, off + lo:off + lo + HEAD_DIM]
            ref[0, :, lo:lo + HEAD_DIM] = (t * cos + pltpu.roll(t, HEAD_DIM // 2, 1) * sin).astype(BF16)
    v_ref[0] = proj[:, FNET_WIDTH + 2 * RET_WIDTH:FNET_WIDTH + 3 * RET_WIDTH].astype(BF16)
    g_ref[0] = proj[:, FNET_WIDTH + 3 * RET_WIDTH:].astype(BF16)

    for c in reversed(range(n_chunks)):
        rows = pl.ds(c * CHUNK, CHUNK)
        for hd in range(RET_HEADS):
            lanes = slice(hd * HEAD_DIM, (hd + 1) * HEAD_DIM)
            sb_ref[0, c, hd] = state_ref[hd].astype(BF16)
            vz = (v_ref[0, rows, lanes].astype(F32) * zeta_ref[hd]).astype(BF16)
            state_ref[hd] = state_ref[hd] * dec_b[hd] + _kv_outer(k_ref[0, rows, lanes], vz)


def _inproj(x, mod, gain, w_in, zeta_b, dec_b, casts=()):
    b, s, _ = x.shape
    tm = min(TM_INPROJ, s)
    nt = s // tm
    nc = tm // CHUNK
    rot_base, rot_local = _rotary_tables(s, tm)
    rev = lambda bi, ti: (bi, nt - 1 - ti, 0)
    tok = lambda width: pl.BlockSpec((1, tm, width), rev)
    const = lambda shape: pl.BlockSpec(shape, lambda bi, ti: (0,) * len(shape))
    act = jax.ShapeDtypeStruct((b, s, RET_WIDTH), BF16)
    states = pl.BlockSpec((1, nc, RET_HEADS, HEAD_DIM, HEAD_DIM), lambda bi, ti: (bi, nt - 1 - ti, 0, 0, 0))
    state_shape = jax.ShapeDtypeStruct((b, s // CHUNK, RET_HEADS, HEAD_DIM, HEAD_DIM), BF16)
    cast_specs = [pl.BlockSpec((w.shape[0] // (b * nt), w.shape[1]), lambda bi, ti: (bi * nt + ti, 0)) for w in casts]
    out = pl.pallas_call(
        functools.partial(_inproj_kernel, n_chunks=nc, dec_b=dec_b, n_cast=len(casts)),
        grid=(b, nt),
        in_specs=[tok(D_MODEL),
                  pl.BlockSpec((1, N_MOD, D_MODEL), lambda bi, ti: (bi, 0, 0)),
                  const((1, D_MODEL)),
                  const((D_MODEL, IN_WIDTH)),
                  pl.BlockSpec((1, 4, HEAD_DIM), lambda bi, ti: (nt - 1 - ti, 0, 0)),
                  const((2, tm, HEAD_DIM)),
                  const((RET_HEADS, CHUNK, HEAD_DIM))] + cast_specs,
        out_specs=[tok(FNET_WIDTH), tok(RET_WIDTH), tok(RET_WIDTH), tok(RET_WIDTH), tok(RET_WIDTH),
                   states] + cast_specs,
        out_shape=[jax.ShapeDtypeStruct((b, s, FNET_WIDTH), BF16), act, act, act, act,
                   state_shape] + [jax.ShapeDtypeStruct(w.shape, BF16) for w in casts],
        scratch_shapes=[pltpu.VMEM((RET_HEADS, HEAD_DIM, HEAD_DIM), F32)],
        compiler_params=_params(("parallel", "arbitrary")),
        name="inproj",
    )(x, mod, gain, w_in, rot_base, rot_local, zeta_b, *casts)
    return out[:6], out[6:]


def _group_lanes():
    return [(g, slice(g * GROUP_DIM, (g + 1) * GROUP_DIM)) for g in range(FNET_GROUPS)]


def _unpack_rows(load, dst_ref, slab=0):
    units = dst_ref.shape[2] // BF16_ROWS
    half = units * F32_ROWS
    for g, lanes in _group_lanes():
        rows = load(lanes).reshape(units, BF16_ROWS, GROUP_DIM).astype(F32)
        dst_ref[slab, g, 0:half, :] = rows[:, :F32_ROWS, :].reshape(half, GROUP_DIM)
        dst_ref[slab, g, half:, :] = rows[:, F32_ROWS:, :].reshape(half, GROUP_DIM)


def _strided_rows(src_ref, row, slab=0):
    units = src_ref.shape[2] // BF16_ROWS
    start = (row // F32_ROWS) * (units * F32_ROWS) + row % F32_ROWS
    rows = pl.ds(start, units, stride=F32_ROWS)
    return jnp.concatenate([src_ref[slab, g, rows, :] for g in range(FNET_GROUPS)], axis=1).astype(BF16)


def _word_rows(src_ref, lead, word, units):
    rows = pl.ds(word, units, stride=WORD_ROWS)
    return jnp.concatenate([pltpu.bitcast(src_ref[lead, g, rows, :], BF16) for g in range(HALF_GROUPS)], axis=1)


def _fnet_kernel(x_ref, even_ref, odd_ref, p_ref, q_ref, cb_ref, sb_ref, m_ref, o_ref, xw_ref, yw_ref, *, n2, reps):
    n1 = DFT_N1
    half = MXU_DIM // 2
    groups = [slice(g * GROUP_DIM, (g + 1) * GROUP_DIM) for g in range(HALF_GROUPS)]

    def aligned(start, step):
        return start * step if isinstance(start, int) else pl.multiple_of(start * step, step)

    def stage_a(sl, slot):
        s2_base = aligned(sl, BF16_ROWS)
        for g, lanes in enumerate(groups):
            block = x_ref[0, :, pl.ds(s2_base, BF16_ROWS), lanes]
            xw_ref[slot, g] = pltpu.bitcast(block.reshape(n1 * BF16_ROWS, GROUP_DIM), jnp.uint32)
        for i in range(WORD_ROWS):
            pair = _word_rows(xw_ref, slot, i, n1)
            for parity, mat_ref in enumerate((even_ref, odd_ref)):
                y = jnp.dot(mat_ref[...], pair, preferred_element_type=F32).astype(BF16)
                words = pltpu.bitcast(y, jnp.uint32)
                row0 = aligned(s2_base + 2 * i + parity, WORD_ROWS)
                for ks in range(Y_K1 // WORD_ROWS):
                    for g, lanes in enumerate(groups):
                        yw_ref[ks, g, pl.ds(row0, WORD_ROWS), :] = words[ks * WORD_ROWS:(ks + 1) * WORD_ROWS, lanes]

    def stage_b(blk, slot):
        t = (p_ref[...] * cb_ref[blk] + q_ref[...] * sb_ref[blk]).astype(BF16)
        pieces = []
        for r in range(reps):
            k1 = blk * reps + r
            src = (n1 - k1 if k1 >= Y_K1 else k1) if isinstance(k1, int) else jnp.where(k1 >= Y_K1, n1 - k1, k1)
            pieces.append(_word_rows(yw_ref, src // WORD_ROWS, src % WORD_ROWS, n2))
        yb = pieces[0] if reps == 1 else jnp.concatenate(pieces, axis=0)
        z = jnp.dot(t, yb, preferred_element_type=F32)
        for g, lanes in enumerate(groups):
            zz = jnp.concatenate([z[:half, lanes], z[half:, lanes]], axis=1).astype(BF16)
            res = jnp.dot(zz, m_ref[g], preferred_element_type=F32)
            for r in range(reps):
                o_ref[0, blk * reps + r, :, lanes] = res[r * n2:(r + 1) * n2].astype(BF16)

    _for_each(n2 // BF16_ROWS, stage_a, STAGE_A_UNROLL)
    _for_each(n1 // reps, stage_b, FNET_UNROLL)


def _for_each(count, body, group):
    if count <= group:
        for i in range(count):
            body(i, i)
        return

    def step(j, carry):
        for u in range(group):
            body(j * group + u, u)
        return carry

    lax.fori_loop(0, count // group, step, 0)


def _fnet(u, m_chan):
    b, s, _ = u.shape
    n1 = DFT_N1
    n2 = s // n1
    assert s == n1 * n2 and MXU_DIM % (2 * n2) == 0 and n2 % BF16_ROWS == 0, s
    reps = MXU_DIM // (2 * n2)
    p_mat, q_mat, cos_b, sin_b = _dft_stage_b(n1, n2)
    const = lambda shape: pl.BlockSpec(shape, lambda bi, hi: (0,) * len(shape))
    tokens = pl.BlockSpec((1, n1, n2, MXU_DIM), lambda bi, hi: (bi, 0, 0, hi))
    return pl.pallas_call(
        functools.partial(_fnet_kernel, n2=n2, reps=reps),
        grid=(b, FNET_WIDTH // MXU_DIM),
        in_specs=[tokens, const((2 * Y_K1, 2 * n1)), const((2 * Y_K1, 2 * n1)),
                  const((MXU_DIM, MXU_DIM)), const((MXU_DIM, MXU_DIM)),
                  const((n1 // reps, 1, MXU_DIM)), const((n1 // reps, 1, MXU_DIM)),
                  pl.BlockSpec((HALF_GROUPS, 2 * GROUP_DIM, GROUP_DIM), lambda bi, hi: (hi, 0, 0))],
        out_specs=tokens,
        out_shape=jax.ShapeDtypeStruct((b, n1, n2, FNET_WIDTH), BF16),
        scratch_shapes=[pltpu.VMEM((STAGE_A_UNROLL, HALF_GROUPS, n1 * WORD_ROWS, GROUP_DIM), jnp.uint32),
                        pltpu.VMEM((Y_K1 // WORD_ROWS, HALF_GROUPS, n2 * WORD_ROWS, GROUP_DIM), jnp.uint32)],
        compiler_params=_params(("parallel", "parallel")),
        name="fnet",
    )(u.reshape(b, n1, n2, FNET_WIDTH), *_dft_stage_a(n1), p_mat, q_mat, cos_b, sin_b, m_chan)


def _mixer_kernel(mod_ref, f_ref, q_ref, k_ref, v_ref, g_ref, sb_ref, w_ref,
                  intra_ref, xif_ref, xib_ref, zeta_ref, o_ref, state_ref, r_ref, fs_ref, *, n_chunks, dec_f):
    @pl.when(pl.program_id(1) == 0)
    def _():
        state_ref[...] = jnp.zeros_like(state_ref)

    tiles = BF16_ROWS // n_chunks
    sub = pl.program_id(1) % tiles

    @pl.when(sub == 0)
    def _():
        _unpack_rows(lambda lanes: f_ref[0, :, :, lanes], fs_ref)

    f_rows = [_strided_rows(fs_ref, sub * n_chunks + c) for c in range(n_chunks)]
    mix = jnp.dot(jnp.concatenate(f_rows, axis=0), w_ref[:FNET_WIDTH, :], preferred_element_type=F32)

    for c in range(n_chunks):
        rows = pl.ds(c * CHUNK, CHUNK)
        for hd in range(RET_HEADS):
            lanes = slice(hd * HEAD_DIM, (hd + 1) * HEAD_DIM)
            q = q_ref[0, rows, lanes]
            k = k_ref[0, rows, lanes]
            v = v_ref[0, rows, lanes]
            qf = q.astype(F32)
            scores = lax.dot_general(q, k, (((1,), (1,)), ((), ())), preferred_element_type=F32)
            r = jnp.dot((scores * intra_ref[hd]).astype(BF16), v, preferred_element_type=F32)
            r += jnp.dot((qf * xif_ref[hd]).astype(BF16), state_ref[hd].astype(BF16),
                         preferred_element_type=F32)
            r += jnp.dot((qf * xib_ref[hd]).astype(BF16), sb_ref[0, c, hd], preferred_element_type=F32)
            vz = (v.astype(F32) * zeta_ref[hd]).astype(BF16)
            state_ref[hd] = state_ref[hd] * dec_f[hd] + _kv_outer(k, vz)
            gate = g_ref[0, rows, lanes].astype(F32)
            r_ref[rows, lanes] = (_rms(r) * (gate * jax.nn.sigmoid(gate))).astype(BF16)

    mix += jnp.dot(r_ref[...], w_ref[FNET_WIDTH:, :], preferred_element_type=F32)
    o_ref[0] = (mod_ref[0, 2:3, :] * mix).astype(BF16)


def _mixer(mod, f, q, k, v, g, sb, w_out, tables):
    intra, xi_f, xi_b, zeta_f, dec_f = tables
    b, s, _ = q.shape
    tm = min(TM_MIXER, s // 2)
    nc = tm // CHUNK
    assert f.shape[1] == CHUNK and BF16_ROWS % nc == 0 and (s // tm) % (BF16_ROWS // nc) == 0
    tiles = BF16_ROWS // nc
    tok = lambda width: pl.BlockSpec((1, tm, width), lambda bi, ti: (bi, ti, 0))
    const = lambda shape: pl.BlockSpec(shape, lambda bi, ti: (0,) * len(shape))
    tab = const((RET_HEADS, CHUNK, HEAD_DIM))
    states = pl.BlockSpec((1, nc, RET_HEADS, HEAD_DIM, HEAD_DIM), lambda bi, ti: (bi, ti, 0, 0, 0))
    return pl.pallas_call(
        functools.partial(_mixer_kernel, n_chunks=nc, dec_f=dec_f),
        grid=(b, s // tm),
        in_specs=[pl.BlockSpec((1, N_MOD, D_MODEL), lambda bi, ti: (bi, 0, 0)),
                  pl.BlockSpec((1, CHUNK, BF16_ROWS, FNET_WIDTH), lambda bi, ti: (bi, 0, ti // tiles, 0)),
                  tok(RET_WIDTH), tok(RET_WIDTH), tok(RET_WIDTH), tok(RET_WIDTH),
                  states,
                  const((D_MODEL, D_MODEL)), tab, tab, tab, tab],
        out_specs=tok(D_MODEL),
        out_shape=jax.ShapeDtypeStruct((b, s, D_MODEL), BF16),
        scratch_shapes=[pltpu.VMEM((RET_HEADS, HEAD_DIM, HEAD_DIM), F32),
                        pltpu.VMEM((tm, RET_WIDTH), BF16),
                        pltpu.VMEM((1, FNET_GROUPS, CHUNK * BF16_ROWS, GROUP_DIM), F32)],
        compiler_params=_params(("parallel", "arbitrary")),
        name="mixer",
    )(mod, f, q, k, v, g, sb, w_out, intra, xi_f, xi_b, zeta_f)


def _mlp_kernel(x_ref, d_ref, mod_ref, gain_ref, w1_ref, w2_ref, gfin_ref, o_ref):
    x = x_ref[0] + d_ref[0].astype(F32)
    a = gain_ref[...] * (1.0 + mod_ref[0, 4:5, :])
    h = (_rms(x) * a + mod_ref[0, 3:4, :]).astype(BF16)
    out = None
    for part in range(MLP_FF_PARTS):
        cols = slice(part * (D_FF // MLP_FF_PARTS), (part + 1) * (D_FF // MLP_FF_PARTS))
        mid = jnp.dot(h, w1_ref[:, cols], preferred_element_type=F32)
        mid = jnp.square(jnp.maximum(mid, 0.0)).astype(BF16)
        slab = jnp.dot(mid, w2_ref[cols, :], preferred_element_type=F32)
        out = slab if out is None else out + slab
    y = x + mod_ref[0, 5:6, :] * out
    o_ref[0] = _rms(y) * gfin_ref[...]


def _mlp(x, delta, mod, gain, w1, w2, gain_final):
    b, s, _ = x.shape
    tm = min(TM_MLP, s)
    tok = pl.BlockSpec((1, tm, D_MODEL), lambda bi, ti: (bi, ti, 0))
    const = lambda shape: pl.BlockSpec(shape, lambda bi, ti: (0,) * len(shape))
    weight = lambda shape: pl.BlockSpec(shape, lambda bi, ti: (0,) * len(shape), pipeline_mode=pl.Buffered(1))
    return pl.pallas_call(
        _mlp_kernel,
        grid=(b, s // tm),
        in_specs=[tok, tok, pl.BlockSpec((1, N_MOD, D_MODEL), lambda bi, ti: (bi, 0, 0)),
                  const((1, D_MODEL)), weight((D_MODEL, D_FF)), weight((D_FF, D_MODEL)), const((1, D_MODEL))],
        out_specs=tok,
        out_shape=jax.ShapeDtypeStruct((b, s, D_MODEL), F32),
        compiler_params=_params(("parallel", "parallel")),
        name="mlp",
    )(x, delta, mod, gain, w1, w2, gain_final)


def _trunk(x, mod, weights, tables):
    gain_mix, w_in, m_chan, w_out, gain_mlp, w1, w2, gain_final = weights
    intra, xi_f, xi_b, zeta_f, zeta_b, dec_f, dec_b = tables
    casts = (w1, w2) if w1.dtype != BF16 else ()
    (u, q, k, v, g, sb), cast = _inproj(x, mod, gain_mix, w_in, zeta_b, dec_b, casts)
    if casts:
        w1, w2 = cast
    f = _fnet(u, m_chan)
    delta = _mixer(mod, f, q, k, v, g, sb, w_out, (intra, xi_f, xi_b, zeta_f, dec_f))
    return _mlp(x, delta, mod, gain_mlp, w1, w2, gain_final), w1, w2


def kernel(x_prompt, x_sample, c_prompt, c_sample, ada_w, ada_b, norm_mix, w_in, w_fnet, w_out,
           norm_mlp, w_mlp_in, w_mlp_out, norm_final):
    assert ada_w.shape[0] == 1, "one encoder layer"
    bp, bs = x_prompt.shape[0], x_sample.shape[0]
    c_all = jnp.concatenate([c_prompt, c_sample], axis=0)
    mod, (w_in_b, w_out_b), m_chan = _modulation_and_casts(c_all, ada_w[0], ada_b[0], (w_in[0], w_out[0]),
                                                           w_fnet[0])
    mod = mod.reshape(bp + bs, N_MOD, D_MODEL)
    weights = (norm_mix[0].reshape(1, -1), w_in_b, m_chan,
               w_out_b, norm_mlp[0].reshape(1, -1), w_mlp_in[0], w_mlp_out[0], norm_final.reshape(1, -1))
    tables = _retention_tables()
    y_prompt, w1_b, w2_b = _trunk(x_prompt, mod[:bp], weights, tables)
    y_sample, _, _ = _trunk(x_sample, mod[bp:], weights[:5] + (w1_b, w2_b) + weights[7:], tables)
    return (y_prompt, y_sample)
```

```python
import functools

import numpy as np
import jax
import jax.numpy as jnp
from jax import lax
from jax.experimental import pallas as pl
from jax.experimental.pallas import tpu as pltpu

F32 = jnp.float32
BF16 = jnp.bfloat16

D_MODEL = 1024
FNET_WIDTH = 512
FNET_GROUPS = 4
GROUP_DIM = 128
RET_WIDTH = 512
RET_HEADS = 4
HEAD_DIM = 128
D_FF = 4096
CHUNK = 128
ROPE_BASE = 10000.0
EPS = 1e-6
N_MOD = 6
IN_WIDTH = FNET_WIDTH + 4 * RET_WIDTH
DECAY_OFFSET_FWD = 0.0
DECAY_OFFSET_BWD = 0.5

DFT_N1 = 128
MXU_DIM = 256
BF16_ROWS = 16
F32_ROWS = 8
WORD_ROWS = BF16_ROWS // 2
HALF_GROUPS = MXU_DIM // GROUP_DIM
Y_K1 = 72
FNET_UNROLL = 32
STAGE_A_UNROLL = 2
VMEM_LIMIT = 56 * 1024 * 1024

TM_INPROJ = 1024
TM_MIXER = 2048
TM_MLP = 1024
MLP_FF_PARTS = 4
MOD_STEPS = 8


def _params(semantics):
    return pltpu.CompilerParams(dimension_semantics=semantics, vmem_limit_bytes=VMEM_LIMIT)


def _retention_tables():
    scale = HEAD_DIM ** -0.5
    h = np.arange(RET_HEADS, dtype=np.float64)
    lg_f = np.log1p(-np.exp2(-5.0 - DECAY_OFFSET_FWD - h))[:, None, None]
    lg_b = np.log1p(-np.exp2(-5.0 - DECAY_OFFSET_BWD - h))[:, None, None]
    i = np.arange(CHUNK, dtype=np.float64)[None, :, None]
    j = np.arange(CHUNK, dtype=np.float64)[None, None, :]
    ones = np.ones((1, 1, HEAD_DIM))
    intra = scale * np.where(i >= j, np.exp(lg_f * np.maximum(i - j, 0.0)), np.exp(lg_b * np.maximum(j - i, 0.0)))
    xi_f = np.exp(lg_f * (i + 1.0)) * ones
    xi_b = np.exp(lg_b * (CHUNK - i)) * ones
    zeta_f = scale * np.exp(lg_f * (CHUNK - 1.0 - i)) * ones
    zeta_b = scale * np.exp(lg_b * i) * ones
    dec_f = tuple(float(v) for v in np.exp(lg_f[:, 0, 0] * CHUNK))
    dec_b = tuple(float(v) for v in np.exp(lg_b[:, 0, 0] * CHUNK))
    f = lambda a: jnp.asarray(a.astype(np.float32))
    return f(intra), f(xi_f), f(xi_b), f(zeta_f), f(zeta_b), dec_f, dec_b


def _f32(a):
    return jnp.asarray(np.asarray(a).astype(np.float32))


def _rotary_tables(seq, tm):
    half = HEAD_DIM // 2
    inv = ROPE_BASE ** (-np.arange(half, dtype=np.float64) / half)
    inv = np.concatenate([inv, inv])
    sign = np.concatenate([-np.ones(half), np.ones(half)])
    base = np.arange(0, seq, tm, dtype=np.float64)[:, None] * inv[None, :]
    local = np.arange(tm, dtype=np.float64)[:, None] * inv[None, :]
    base_rows = np.stack([np.cos(base), np.sin(base), sign * np.cos(base), sign * np.sin(base)], axis=1)
    return _f32(base_rows), _f32(np.stack([np.cos(local), np.sin(local)], axis=0))


def _dft_stage_a(n1):
    k = np.arange(n1, dtype=np.int64)
    ang = 2.0 * np.pi * ((k[:Y_K1, None] * k[None, :]) % n1) / n1
    base = np.stack([np.cos(ang), -np.sin(ang)], axis=1).reshape(2 * Y_K1, n1) * (n1 ** -0.5)
    even, odd = np.zeros((2 * Y_K1, 2 * n1)), np.zeros((2 * Y_K1, 2 * n1))
    even[:, 0::2] = base
    odd[:, 1::2] = base
    return _f32(even).astype(BF16), _f32(odd).astype(BF16)


def _dft_stage_b(n1, n2):
    reps = MXU_DIM // (2 * n2)
    seq = n1 * n2
    idx = np.arange(MXU_DIM, dtype=np.int64)
    row_c, row_r, row_k = idx // (reps * n2), (idx // n2) % reps, idx % n2
    col_r, col_s, col_c = idx // (2 * n2), (idx // 2) % n2, idx % 2
    alpha = 2.0 * np.pi * ((row_k[:, None] * col_s[None, :]) % n2) / n2
    ca, sa = np.cos(alpha), np.sin(alpha)
    c_out, c_in = row_c[:, None], col_c[None, :]
    p_mat = np.where(c_out == c_in, ca, np.where(c_out < c_in, sa, -sa))
    q_mat = np.where(c_out == c_in, -sa, np.where(c_out < c_in, ca, -ca))
    keep = (row_r[:, None] == col_r[None, :]) * ((n2 * GROUP_DIM) ** -0.5)
    blk = np.arange(n1 // reps, dtype=np.int64)[:, None]
    k1 = blk * reps + col_r[None, :]
    beta = 2.0 * np.pi * ((col_s[None, :] * k1) % seq) / seq
    conj = np.where((k1 >= Y_K1) & (col_c[None, :] == 1), -1.0, 1.0)
    return (_f32(p_mat * keep), _f32(q_mat * keep),
            _f32((np.cos(beta) * conj)[:, None, :]), _f32((np.sin(beta) * conj)[:, None, :]))


def _channel_dft():
    c = np.arange(GROUP_DIM, dtype=np.int64)
    ang = 2.0 * np.pi * ((c[:, None] * c[None, :]) % GROUP_DIM) / GROUP_DIM
    return jnp.asarray(np.concatenate([np.cos(ang), np.sin(ang)], axis=0).astype(np.float32))


def _mod_kernel(c_ref, w_ref, b_ref, cs_ref, wf_ref, *refs):
    refs[-1][0] = jnp.dot(cs_ref[...], wf_ref[0], preferred_element_type=F32,
                          precision=lax.Precision.HIGHEST).astype(BF16)
    refs = refs[:-1]
    c = c_ref[...]
    s = c * jax.nn.sigmoid(c)
    w = w_ref[...]
    s_hi, w_hi = s.astype(BF16), w.astype(BF16)
    s_lo, w_lo = (s - s_hi.astype(F32)).astype(BF16), (w - w_hi.astype(F32)).astype(BF16)
    dot = functools.partial(jnp.dot, preferred_element_type=F32)
    n_cast = (len(refs) - 1) // 2
    o_ref = refs[n_cast]
    o_ref[...] = dot(s_hi, w_hi) + (dot(s_hi, w_lo) + dot(s_lo, w_hi)) + b_ref[...]
    for src, dst in zip(refs[:n_cast], refs[n_cast + 1:]):
        dst[...] = src[...].astype(BF16)


def _modulation_and_casts(c, ada_w, ada_b, weights, w_fnet):
    rows = c.shape[0]
    steps = MOD_STEPS
    per = steps // FNET_GROUPS
    tn = N_MOD * D_MODEL // steps
    row_block = lambda w: pl.BlockSpec((w.shape[0] // steps, w.shape[1]), lambda n: (n, 0))
    out = pl.pallas_call(
        _mod_kernel,
        grid=(steps,),
        in_specs=[pl.BlockSpec((rows, D_MODEL), lambda n: (0, 0)),
                  pl.BlockSpec((D_MODEL, tn), lambda n: (0, n)),
                  pl.BlockSpec((1, tn), lambda n: (0, n)),
                  pl.BlockSpec((2 * GROUP_DIM, GROUP_DIM), lambda n: (0, 0)),
                  pl.BlockSpec((1, GROUP_DIM, GROUP_DIM), lambda n: (n // per, 0, 0))]
                 + [row_block(w) for w in weights],
        out_specs=[pl.BlockSpec((rows, tn), lambda n: (0, n))] + [row_block(w) for w in weights]
                  + [pl.BlockSpec((1, 2 * GROUP_DIM, GROUP_DIM), lambda n: (n // per, 0, 0))],
        out_shape=[jax.ShapeDtypeStruct((rows, N_MOD * D_MODEL), F32)]
                  + [jax.ShapeDtypeStruct(w.shape, BF16) for w in weights]
                  + [jax.ShapeDtypeStruct((FNET_GROUPS, 2 * GROUP_DIM, GROUP_DIM), BF16)],
        compiler_params=_params(("arbitrary",)),
        name="adaln_mod",
    )(c, ada_w, ada_b.reshape(1, -1), _channel_dft(), w_fnet, *weights)
    return out[0], out[1:-1], out[-1]


def _rms(x):
    return x * lax.rsqrt(jnp.mean(x * x, axis=-1, keepdims=True) + EPS)


def _kv_outer(k, vz):
    return lax.dot_general(k, vz, (((0,), (0,)), ((), ())), preferred_element_type=F32)


def _inproj_kernel(x_ref, mod_ref, gain_ref, w_ref, rbase_ref, rlocal_ref, zeta_ref, *refs, n_chunks, dec_b, n_cast):
    u_ref, q_ref, k_ref, v_ref, g_ref, sb_ref = refs[n_cast:n_cast + 6]
    state_ref = refs[-1]
    for src, dst in zip(refs[:n_cast], refs[n_cast + 6:-1]):
        dst[...] = src[...].astype(BF16)

    @pl.when(pl.program_id(1) == 0)
    def _():
        state_ref[...] = jnp.zeros_like(state_ref)

    x = x_ref[0]
    a = gain_ref[...] * (1.0 + mod_ref[0, 1:2, :])
    h = (_rms(x) * a + mod_ref[0, 0:1, :]).astype(BF16)
    proj = jnp.dot(h, w_ref[...], preferred_element_type=F32)
    u_ref[0] = proj[:, :FNET_WIDTH].astype(BF16)
    cos_l, sin_l = rlocal_ref[0], rlocal_ref[1]
    cos = rbase_ref[0, 0:1, :] * cos_l - rbase_ref[0, 1:2, :] * sin_l
    sin = rbase_ref[0, 3:4, :] * cos_l + rbase_ref[0, 2:3, :] * sin_l
    for hd in range(RET_HEADS):
        lo = hd * HEAD_DIM
        for off, ref in ((FNET_WIDTH, q_ref), (FNET_WIDTH + RET_WIDTH, k_ref)):
            t = proj[:, off + lo:off + lo + HEAD_DIM]
            ref[0, :, lo:lo + HEAD_DIM] = (t * cos + pltpu.roll(t, HEAD_DIM // 2, 1) * sin).astype(BF16)
    v_ref[0] = proj[:, FNET_WIDTH + 2 * RET_WIDTH:FNET_WIDTH + 3 * RET_WIDTH].astype(BF16)
    g_ref[0] = proj[:, FNET_WIDTH + 3 * RET_WIDTH:].astype(BF16)

    for c in reversed(range(n_chunks)):
        rows = pl.ds(c * CHUNK, CHUNK)
        for hd in range(RET_HEADS):
            lanes = slice(hd * HEAD_DIM, (hd + 1) * HEAD_DIM)
            sb_ref[0, c, hd] = state_ref[hd].astype(BF16)
            vz = (v_ref[0, rows, lanes].astype(F32) * zeta_ref[hd]).astype(BF16)
            state_ref[hd] = state_ref[hd] * dec_b[hd] + _kv_outer(k_ref[0, rows, lanes], vz)


def _inproj(x, mod, gain, w_in, zeta_b, dec_b, casts=(), row0=0):
    b, s, _ = x.shape
    tm = min(TM_INPROJ, s)
    nt = s // tm
    nc = tm // CHUNK
    rot_base, rot_local = _rotary_tables(s, tm)
    rev = lambda bi, ti: (bi, nt - 1 - ti, 0)
    tok = lambda width: pl.BlockSpec((1, tm, width), rev)
    const = lambda shape: pl.BlockSpec(shape, lambda bi, ti: (0,) * len(shape))
    act = jax.ShapeDtypeStruct((b, s, RET_WIDTH), BF16)
    states = pl.BlockSpec((1, nc, RET_HEADS, HEAD_DIM, HEAD_DIM), lambda bi, ti: (bi, nt - 1 - ti, 0, 0, 0))
    state_shape = jax.ShapeDtypeStruct((b, s // CHUNK, RET_HEADS, HEAD_DIM, HEAD_DIM), BF16)
    cast_specs = [pl.BlockSpec((w.shape[0] // (b * nt), w.shape[1]), lambda bi, ti: (bi * nt + ti, 0)) for w in casts]
    out = pl.pallas_call(
        functools.partial(_inproj_kernel, n_chunks=nc, dec_b=dec_b, n_cast=len(casts)),
        grid=(b, nt),
        in_specs=[tok(D_MODEL),
                  pl.BlockSpec((1, N_MOD, D_MODEL), lambda bi, ti: (bi + row0, 0, 0)),
                  const((1, D_MODEL)),
                  const((D_MODEL, IN_WIDTH)),
                  pl.BlockSpec((1, 4, HEAD_DIM), lambda bi, ti: (nt - 1 - ti, 0, 0)),
                  const((2, tm, HEAD_DIM)),
                  const((RET_HEADS, CHUNK, HEAD_DIM))] + cast_specs,
        out_specs=[tok(FNET_WIDTH), tok(RET_WIDTH), tok(RET_WIDTH), tok(RET_WIDTH), tok(RET_WIDTH),
                   states] + cast_specs,
        out_shape=[jax.ShapeDtypeStruct((b, s, FNET_WIDTH), BF16), act, act, act, act,
                   state_shape] + [jax.ShapeDtypeStruct(w.shape, BF16) for w in casts],
        scratch_shapes=[pltpu.VMEM((RET_HEADS, HEAD_DIM, HEAD_DIM), F32)],
        compiler_params=_params(("parallel", "arbitrary")),
        name="inproj",
    )(x, mod, gain, w_in, rot_base, rot_local, zeta_b, *casts)
    return out[:6], out[6:]


def _group_lanes():
    return [(g, slice(g * GROUP_DIM, (g + 1) * GROUP_DIM)) for g in range(FNET_GROUPS)]


def _unpack_rows(load, dst_ref, slab=0):
    units = dst_ref.shape[2] // BF16_ROWS
    half = units * F32_ROWS
    for g, lanes in _group_lanes():
        rows = load(lanes).reshape(units, BF16_ROWS, GROUP_DIM).astype(F32)
        dst_ref[slab, g, 0:half, :] = rows[:, :F32_ROWS, :].reshape(half, GROUP_DIM)
        dst_ref[slab, g, half:, :] = rows[:, F32_ROWS:, :].reshape(half, GROUP_DIM)


def _strided_rows(src_ref, row, slab=0):
    units = src_ref.shape[2] // BF16_ROWS
    start = (row // F32_ROWS) * (units * F32_ROWS) + row % F32_ROWS
    rows = pl.ds(start, units, stride=F32_ROWS)
    return jnp.concatenate([src_ref[slab, g, rows, :] for g in range(FNET_GROUPS)], axis=1).astype(BF16)


def _word_rows(src_ref, lead, word, units):
    rows = pl.ds(word, units, stride=WORD_ROWS)
    return jnp.concatenate([pltpu.bitcast(src_ref[lead, g, rows, :], BF16) for g in range(HALF_GROUPS)], axis=1)


def _fnet_kernel(x_ref, even_ref, odd_ref, p_ref, q_ref, cb_ref, sb_ref, m_ref, o_ref, xw_ref, yw_ref, *, n2, reps):
    n1 = DFT_N1
    half = MXU_DIM // 2
    groups = [slice(g * GROUP_DIM, (g + 1) * GROUP_DIM) for g in range(HALF_GROUPS)]

    def aligned(start, step):
        return start * step if isinstance(start, int) else pl.multiple_of(start * step, step)

    def stage_a(sl, slot):
        s2_base = aligned(sl, BF16_ROWS)
        for g, lanes in enumerate(groups):
            block = x_ref[0, :, pl.ds(s2_base, BF16_ROWS), lanes]
            xw_ref[slot, g] = pltpu.bitcast(block.reshape(n1 * BF16_ROWS, GROUP_DIM), jnp.uint32)
        for i in range(WORD_ROWS):
            pair = _word_rows(xw_ref, slot, i, n1)
            for parity, mat_ref in enumerate((even_ref, odd_ref)):
                y = jnp.dot(mat_ref[...], pair, preferred_element_type=F32).astype(BF16)
                words = pltpu.bitcast(y, jnp.uint32)
                row0 = aligned(s2_base + 2 * i + parity, WORD_ROWS)
                for ks in range(Y_K1 // WORD_ROWS):
                    for g, lanes in enumerate(groups):
                        yw_ref[ks, g, pl.ds(row0, WORD_ROWS), :] = words[ks * WORD_ROWS:(ks + 1) * WORD_ROWS, lanes]

    def stage_b(blk, slot):
        t = (p_ref[...] * cb_ref[blk] + q_ref[...] * sb_ref[blk]).astype(BF16)
        pieces = []
        for r in range(reps):
            k1 = blk * reps + r
            src = (n1 - k1 if k1 >= Y_K1 else k1) if isinstance(k1, int) else jnp.where(k1 >= Y_K1, n1 - k1, k1)
            pieces.append(_word_rows(yw_ref, src // WORD_ROWS, src % WORD_ROWS, n2))
        yb = pieces[0] if reps == 1 else jnp.concatenate(pieces, axis=0)
        z = jnp.dot(t, yb, preferred_element_type=F32)
        for g, lanes in enumerate(groups):
            zz = jnp.concatenate([z[:half, lanes], z[half:, lanes]], axis=1).astype(BF16)
            res = jnp.dot(zz, m_ref[g], preferred_element_type=F32)
            for r in range(reps):
                o_ref[0, blk * reps + r, :, lanes] = res[r * n2:(r + 1) * n2].astype(BF16)

    _for_each(n2 // BF16_ROWS, stage_a, STAGE_A_UNROLL)
    _for_each(n1 // reps, stage_b, FNET_UNROLL)


def _for_each(count, body, group):
    if count <= group:
        for i in range(count):
            body(i, i)
        return

    def step(j, carry):
        for u in range(group):
            body(j * group + u, u)
        return carry

    lax.fori_loop(0, count // group, step, 0)


def _fnet(u, m_chan):
    b, s, _ = u.shape
    n1 = DFT_N1
    n2 = s // n1
    assert s == n1 * n2 and MXU_DIM % (2 * n2) == 0 and n2 % BF16_ROWS == 0, s
    reps = MXU_DIM // (2 * n2)
    p_mat, q_mat, cos_b, sin_b = _dft_stage_b(n1, n2)
    const = lambda shape: pl.BlockSpec(shape, lambda bi, hi: (0,) * len(shape))
    tokens = pl.BlockSpec((1, n1, n2, MXU_DIM), lambda bi, hi: (bi, 0, 0, hi))
    return pl.pallas_call(
        functools.partial(_fnet_kernel, n2=n2, reps=reps),
        grid=(b, FNET_WIDTH // MXU_DIM),
        in_specs=[tokens, const((2 * Y_K1, 2 * n1)), const((2 * Y_K1, 2 * n1)),
                  const((MXU_DIM, MXU_DIM)), const((MXU_DIM, MXU_DIM)),
                  const((n1 // reps, 1, MXU_DIM)), const((n1 // reps, 1, MXU_DIM)),
                  pl.BlockSpec((HALF_GROUPS, 2 * GROUP_DIM, GROUP_DIM), lambda bi, hi: (hi, 0, 0))],
        out_specs=tokens,
        out_shape=jax.ShapeDtypeStruct((b, n1, n2, FNET_WIDTH), BF16),
        scratch_shapes=[pltpu.VMEM((STAGE_A_UNROLL, HALF_GROUPS, n1 * WORD_ROWS, GROUP_DIM), jnp.uint32),
                        pltpu.VMEM((Y_K1 // WORD_ROWS, HALF_GROUPS, n2 * WORD_ROWS, GROUP_DIM), jnp.uint32)],
        compiler_params=_params(("parallel", "parallel")),
        name="fnet",
    )(u.reshape(b, n1, n2, FNET_WIDTH), *_dft_stage_a(n1), p_mat, q_mat, cos_b, sin_b, m_chan)


def _mixer_kernel(mod_ref, f_ref, q_ref, k_ref, v_ref, g_ref, sb_ref, w_ref,
                  intra_ref, xif_ref, xib_ref, zeta_ref, o_ref, state_ref, r_ref, fs_ref, *, n_chunks, dec_f):
    @pl.when(pl.program_id(1) == 0)
    def _():
        state_ref[...] = jnp.zeros_like(state_ref)

    tiles = BF16_ROWS // n_chunks
    sub = pl.program_id(1) % tiles

    @pl.when(sub == 0)
    def _():
        _unpack_rows(lambda lanes: f_ref[0, :, :, lanes], fs_ref)

    f_rows = [_strided_rows(fs_ref, sub * n_chunks + c) for c in range(n_chunks)]
    mix = jnp.dot(jnp.concatenate(f_rows, axis=0), w_ref[:FNET_WIDTH, :], preferred_element_type=F32)

    for c in range(n_chunks):
        rows = pl.ds(c * CHUNK, CHUNK)
        for hd in range(RET_HEADS):
            lanes = slice(hd * HEAD_DIM, (hd + 1) * HEAD_DIM)
            q = q_ref[0, rows, lanes]
            k = k_ref[0, rows, lanes]
            v = v_ref[0, rows, lanes]
            qf = q.astype(F32)
            scores = lax.dot_general(q, k, (((1,), (1,)), ((), ())), preferred_element_type=F32)
            r = jnp.dot((scores * intra_ref[hd]).astype(BF16), v, preferred_element_type=F32)
            r += jnp.dot((qf * xif_ref[hd]).astype(BF16), state_ref[hd].astype(BF16),
                         preferred_element_type=F32)
            r += jnp.dot((qf * xib_ref[hd]).astype(BF16), sb_ref[0, c, hd], preferred_element_type=F32)
            vz = (v.astype(F32) * zeta_ref[hd]).astype(BF16)
            state_ref[hd] = state_ref[hd] * dec_f[hd] + _kv_outer(k, vz)
            gate = g_ref[0, rows, lanes].astype(F32)
            r_ref[rows, lanes] = (_rms(r) * (gate * jax.nn.sigmoid(gate))).astype(BF16)

    mix += jnp.dot(r_ref[...], w_ref[FNET_WIDTH:, :], preferred_element_type=F32)
    o_ref[0] = (mod_ref[0, 2:3, :] * mix).astype(BF16)


def _mixer(mod, f, q, k, v, g, sb, w_out, tables, row0=0):
    intra, xi_f, xi_b, zeta_f, dec_f = tables
    b, s, _ = q.shape
    tm = min(TM_MIXER, s // 2)
    nc = tm // CHUNK
    assert f.shape[1] == CHUNK and BF16_ROWS % nc == 0 and (s // tm) % (BF16_ROWS // nc) == 0
    tiles = BF16_ROWS // nc
    tok = lambda width: pl.BlockSpec((1, tm, width), lambda bi, ti: (bi, ti, 0))
    const = lambda shape: pl.BlockSpec(shape, lambda bi, ti: (0,) * len(shape))
    tab = const((RET_HEADS, CHUNK, HEAD_DIM))
    states = pl.BlockSpec((1, nc, RET_HEADS, HEAD_DIM, HEAD_DIM), lambda bi, ti: (bi, ti, 0, 0, 0))
    return pl.pallas_call(
        functools.partial(_mixer_kernel, n_chunks=nc, dec_f=dec_f),
        grid=(b, s // tm),
        in_specs=[pl.BlockSpec((1, N_MOD, D_MODEL), lambda bi, ti: (bi + row0, 0, 0)),
                  pl.BlockSpec((1, CHUNK, BF16_ROWS, FNET_WIDTH), lambda bi, ti: (bi, 0, ti // tiles, 0)),
                  tok(RET_WIDTH), tok(RET_WIDTH), tok(RET_WIDTH), tok(RET_WIDTH),
                  states,
                  const((D_MODEL, D_MODEL)), tab, tab, tab, tab],
        out_specs=tok(D_MODEL),
        out_shape=jax.ShapeDtypeStruct((b, s, D_MODEL), BF16),
        scratch_shapes=[pltpu.VMEM((RET_HEADS, HEAD_DIM, HEAD_DIM), F32),
                        pltpu.VMEM((tm, RET_WIDTH), BF16),
                        pltpu.VMEM((1, FNET_GROUPS, CHUNK * BF16_ROWS, GROUP_DIM), F32)],
        compiler_params=_params(("parallel", "arbitrary")),
        name="mixer",
    )(mod, f, q, k, v, g, sb, w_out, intra, xi_f, xi_b, zeta_f)


def _mlp_kernel(x_ref, d_ref, mod_ref, gain_ref, w1_ref, w2_ref, gfin_ref, o_ref):
    x = x_ref[0] + d_ref[0].astype(F32)
    a = gain_ref[...] * (1.0 + mod_ref[0, 4:5, :])
    h = (_rms(x) * a + mod_ref[0, 3:4, :]).astype(BF16)
    out = None
    for part in range(MLP_FF_PARTS):
        cols = slice(part * (D_FF // MLP_FF_PARTS), (part + 1) * (D_FF // MLP_FF_PARTS))
        mid = jnp.dot(h, w1_ref[:, cols], preferred_element_type=F32)
        mid = jnp.square(jnp.maximum(mid, 0.0)).astype(BF16)
        slab = jnp.dot(mid, w2_ref[cols, :], preferred_element_type=F32)
        out = slab if out is None else out + slab
    y = x + mod_ref[0, 5:6, :] * out
    o_ref[0] = _rms(y) * gfin_ref[...]


def _mlp(x, delta, mod, gain, w1, w2, gain_final, row0=0):
    b, s, _ = x.shape
    tm = min(TM_MLP, s)
    tok = pl.BlockSpec((1, tm, D_MODEL), lambda bi, ti: (bi, ti, 0))
    const = lambda shape: pl.BlockSpec(shape, lambda bi, ti: (0,) * len(shape))
    weight = lambda shape: pl.BlockSpec(shape, lambda bi, ti: (0,) * len(shape), pipeline_mode=pl.Buffered(1))
    return pl.pallas_call(
        _mlp_kernel,
        grid=(b, s // tm),
        in_specs=[tok, tok, pl.BlockSpec((1, N_MOD, D_MODEL), lambda bi, ti: (bi + row0, 0, 0)),
                  const((1, D_MODEL)), weight((D_MODEL, D_FF)), weight((D_FF, D_MODEL)), const((1, D_MODEL))],
        out_specs=tok,
        out_shape=jax.ShapeDtypeStruct((b, s, D_MODEL), F32),
        compiler_params=_params(("parallel", "parallel")),
        name="mlp",
    )(x, delta, mod, gain, w1, w2, gain_final)


def _trunk(x, mod, weights, tables, row0=0):
    gain_mix, w_in, m_chan, w_out, gain_mlp, w1, w2, gain_final = weights
    intra, xi_f, xi_b, zeta_f, zeta_b, dec_f, dec_b = tables
    casts = (w1, w2) if w1.dtype != BF16 else ()
    (u, q, k, v, g, sb), cast = _inproj(x, mod, gain_mix, w_in, zeta_b, dec_b, casts, row0)
    if casts:
        w1, w2 = cast
    f = _fnet(u, m_chan)
    delta = _mixer(mod, f, q, k, v, g, sb, w_out, (intra, xi_f, xi_b, zeta_f, dec_f), row0)
    return _mlp(x, delta, mod, gain_mlp, w1, w2, gain_final, row0), w1, w2


def kernel(x_prompt, x_sample, c_prompt, c_sample, ada_w, ada_b, norm_mix, w_in, w_fnet, w_out,
           norm_mlp, w_mlp_in, w_mlp_out, norm_final):
    assert ada_w.shape[0] == 1, "one encoder layer"
    bp, bs = x_prompt.shape[0], x_sample.shape[0]
    c_all = jnp.concatenate([c_prompt, c_sample], axis=0)
    mod, (w_in_b, w_out_b), m_chan = _modulation_and_casts(c_all, ada_w[0], ada_b[0], (w_in[0], w_out[0]),
                                                           w_fnet[0])
    mod = mod.reshape(bp + bs, N_MOD, D_MODEL)
    weights = (norm_mix[0].reshape(1, -1), w_in_b, m_chan,
               w_out_b, norm_mlp[0].reshape(1, -1), w_mlp_in[0], w_mlp_out[0], norm_final.reshape(1, -1))
    tables = _retention_tables()
    y_prompt, w1_b, w2_b = _trunk(x_prompt, mod, weights, tables)
    y_sample, _, _ = _trunk(x_sample, mod, weights[:5] + (w1_b, w2_b) + weights[7:], tables, bp)
    return (y_prompt, y_sample)
```
